```python
import math
import jax, jax.numpy as jnp
from jax import lax
import numpy as np

D_MODEL = 2048
BATCH = 1
SEQ = 8192
DEPTH = 4

GRID_W = 64
CTX_LEN = 256
N_MIXERS = 3
N_A = (DEPTH + 2) // 3
N_B = (DEPTH + 1) // 3
N_C = DEPTH // 3
EPS = 1e-6

NA_WIDTH = D_MODEL
NA_HEAD_DIM = 128
NA_HEADS = NA_WIDTH // NA_HEAD_DIM
WIN_H = 8
WIN_W = 16

ML_WIDTH = 2 * D_MODEL
ML_HEADS = 8
ML_HEAD_DIM = ML_WIDTH // ML_HEADS
ML_CONV = 3
ML_BLOCK = 4
ML_CHUNK = 64

S5_WIDTH = D_MODEL
S5_GROUP = 16
S5_GROUPS = S5_WIDTH // S5_GROUP
S5_STATE = 64
DT_MIN = 1e-3
DT_MAX = 1e-1

kernel_name = 'hybrid_natten_mlstm_s5_flow_backbone'


def rmsnorm(x, g):
    xf = x.astype(jnp.float32)
    y = xf * lax.rsqrt(jnp.mean(xf * xf, -1, keepdims=True) + EPS)
    return (y * g.astype(jnp.float32)).astype(x.dtype)


def neighbourhood_attention(q, k, v, kc, vc, rpb):
    B, S, H, dh = q.shape
    rows = S // GRID_W
    wh = min(WIN_H, rows)
    n_win = wh * WIN_W
    scale = dh ** -0.5
    qg = q.reshape(B, rows, GRID_W, H, dh)
    kg = k.reshape(B, rows, GRID_W, H, dh)
    vg = v.reshape(B, rows, GRID_W, H, dh)
    cols = jnp.arange(GRID_W)
    c0 = jnp.clip(cols - WIN_W // 2, 0, GRID_W - WIN_W)
    col_idx = c0[:, None] + jnp.arange(WIN_W)[None, :]
    col_bias_idx = col_idx - cols[:, None] + (WIN_W - 1)

    def row_block(r):
        r0 = jnp.clip(r - wh // 2, 0, rows - wh)
        q_r = lax.dynamic_index_in_dim(qg, r, axis=1, keepdims=False)
        k_band = lax.dynamic_slice_in_dim(kg, r0, wh, axis=1)
        v_band = lax.dynamic_slice_in_dim(vg, r0, wh, axis=1)
        k_win = k_band[:, :, col_idx]
        v_win = v_band[:, :, col_idx]
        row_bias_idx = r0 + jnp.arange(wh) - r + (WIN_H - 1)
        bias = rpb[:, row_bias_idx[None, :, None], col_bias_idx[:, None, :]]
        s_win = jnp.einsum('bqhd,brqwhd->bhqrw', q_r, k_win).astype(jnp.float32) * scale \
            + bias.astype(jnp.float32)
        s_ctx = jnp.einsum('bqhd,bchd->bhqc', q_r, kc).astype(jnp.float32) * scale
        s = jnp.concatenate([s_win.reshape(B, H, GRID_W, n_win), s_ctx], -1)
        p = jax.nn.softmax(s, -1).astype(v.dtype)
        p_win = p[..., :n_win].reshape(B, H, GRID_W, wh, WIN_W)
        return (jnp.einsum('bhqrw,brqwhd->bqhd', p_win, v_win)
                + jnp.einsum('bhqc,bchd->bqhd', p[..., n_win:], vc))

    out = lax.map(row_block, jnp.arange(rows))
    return jnp.moveaxis(out, 0, 1).reshape(B, S, H, dh)


def context_attention(q, k, v):
    s = jnp.einsum('bqhd,bkhd->bhqk', q, k).astype(jnp.float32) * (q.shape[-1] ** -0.5)
    p = jax.nn.softmax(s, -1).astype(v.dtype)
    return jnp.einsum('bhqk,bkhd->bqhd', p, v)


def na_mixer(hx, hc, w_in, rpb, w_out, with_ctx_out):
    B, S, _ = hx.shape
    E = NA_WIDTH

    def heads(t):
        return t.reshape(t.shape[0], t.shape[1], NA_HEADS, NA_HEAD_DIM)

    qx, kx, vx, zx = jnp.split(hx @ w_in, 4, -1)
    if with_ctx_out:
        qc, kc, vc, zc = jnp.split(hc @ w_in, 4, -1)
    else:
        kc, vc = jnp.split(hc @ w_in[:, E:3 * E], 2, -1)
    o_x = neighbourhood_attention(heads(qx), heads(kx), heads(vx), heads(kc), heads(vc), rpb)
    y_x = (o_x.reshape(B, S, E) * jax.nn.silu(zx)) @ w_out
    if not with_ctx_out:
        return y_x, None
    o_c = context_attention(heads(qc), heads(kc), heads(vc))
    y_c = (o_c.reshape(hc.shape[0], hc.shape[1], E) * jax.nn.silu(zc)) @ w_out
    return y_x, y_c


def dwconv_centred(x, w, b):
    k = w.shape[0]
    left = k // 2
    t = x.shape[1]
    xp = jnp.pad(x, ((0, 0), (left, k - 1 - left), (0, 0)))
    y = b + xp[:, 0:t] * w[0]
    for j in range(1, k):
        y = y + xp[:, j:j + t] * w[j]
    return y


def headwise(x, w):
    B, T, E = x.shape
    xb = x.reshape(B, T, E // ML_BLOCK, ML_BLOCK)
    return jnp.einsum('btnc,ncd->btnd', xb, w).reshape(B, T, E)


def mlstm_chunkwise(q, k, v, i_pre, log_f, state):
    B, T, NH, DH = q.shape
    L = ML_CHUNK
    nc = T // L
    f32 = jnp.float32

    def chunk(t):
        return t.astype(f32).reshape(B, nc, L, NH, -1).transpose(1, 0, 3, 2, 4)

    def chunk_g(t):
        return t.astype(f32).reshape(B, nc, L, NH).transpose(1, 0, 3, 2)

    tril = jnp.tril(jnp.ones((L, L), dtype=bool))

    def step(carry, inp):
        C, n, m = carry
        qc, kc, vc, ic, fc = inp
        b = jnp.cumsum(fc, -1)
        a = b + m[..., None]
        dmat = jnp.where(tril, b[..., :, None] - b[..., None, :] + ic[..., None, :], -jnp.inf)
        m_t = jnp.maximum(a, dmat.max(-1))
        w_inter = jnp.exp(a - m_t)
        s = jnp.einsum('bhtd,bhsd->bhts', qc, kc) * jnp.exp(dmat - m_t[..., None])
        num = w_inter[..., None] * jnp.einsum('bhvd,bhtd->bhtv', C, qc) \
            + jnp.einsum('bhts,bhsv->bhtv', s, vc)
        den = w_inter * jnp.einsum('bhd,bhtd->bht', n, qc) + s.sum(-1)
        h = num / jnp.maximum(jnp.abs(den), jnp.exp(-m_t))[..., None]
        b_end = b[..., -1]
        g = b_end[..., None] - b + ic
        m_new = jnp.maximum(b_end + m, g.max(-1))
        decay = jnp.exp(b_end + m - m_new)
        w_s = jnp.exp(g - m_new[..., None])
        C = decay[..., None, None] * C + jnp.einsum('bhsv,bhsd->bhvd', vc * w_s[..., None], kc)
        n = decay[..., None] * n + jnp.einsum('bhs,bhsd->bhd', w_s, kc)
        return (C, n, m_new), h

    state, hs = lax.scan(step, state, (chunk(q), chunk(k), chunk(v), chunk_g(i_pre), chunk_g(log_f)))
    h = hs.transpose(1, 0, 3, 2, 4).reshape(B, T, NH, DH)
    return h, state


def mlstm_bidirectional(q, k, v, gates, state_f, state_b):
    i_f, f_f, i_b, f_b = jnp.split(gates.astype(jnp.float32), 4, -1)
    h_f, state_f = mlstm_chunkwise(q, k, v, i_f, jax.nn.log_sigmoid(f_f), state_f)

    def rev(t):
        return jnp.flip(t, 1)

    h_b, state_b = mlstm_chunkwise(rev(q), rev(k), rev(v), rev(i_b),
                                   rev(jax.nn.log_sigmoid(f_b)), state_b)
    return h_f + rev(h_b), state_f, state_b


def mlstm_mixer(hx, hc, w_in, conv_w, conv_b, wq, wk, wv, w_gate, b_gate, skip, norm_g,
                w_out, with_ctx_out):
    B = hx.shape[0]
    f32 = jnp.float32

    def heads(t):
        return t.reshape(t.shape[0], t.shape[1], ML_HEADS, ML_HEAD_DIM)

    def branch(h):
        x_in, z = jnp.split(h @ w_in, 2, -1)
        x_conv = jax.nn.silu(dwconv_centred(x_in, conv_w, conv_b))
        q = headwise(x_conv, wq)
        k = headwise(x_conv, wk)
        v = headwise(x_in, wv)
        gates = jnp.concatenate([q, k, v], -1) @ w_gate + b_gate
        return x_conv, z, heads(q), heads(k * (ML_HEAD_DIM ** -0.5)), heads(v), gates

    def readout(h, x_conv, z):
        mu = jnp.mean(h, -1, keepdims=True)
        var = jnp.mean(jnp.square(h - mu), -1, keepdims=True)
        hn = ((h - mu) * lax.rsqrt(var + EPS)).reshape(x_conv.shape).astype(x_conv.dtype) * norm_g
        return ((hn + skip * x_conv) * jax.nn.silu(z)) @ w_out

    zero = (jnp.zeros((B, ML_HEADS, ML_HEAD_DIM, ML_HEAD_DIM), f32),
            jnp.zeros((B, ML_HEADS, ML_HEAD_DIM), f32),
            jnp.zeros((B, ML_HEADS), f32))
    xc_c, z_c, q_c, k_c, v_c, g_c = branch(hc)
    h_c, st_f, st_b = mlstm_bidirectional(q_c, k_c, v_c, g_c, zero, zero)
    xc_x, z_x, q_x, k_x, v_x, g_x = branch(hx)
    h_x, _, _ = mlstm_bidirectional(q_x, k_x, v_x, g_x, st_f, st_b)
    y_x = readout(h_x, xc_x, z_x)
    y_c = readout(h_c, xc_c, z_c) if with_ctx_out else None
    return y_x, y_c


def _linear_combine(e1, e2):
    a1, b1 = e1
    a2, b2 = e2
    return a1 * a2, a2 * b1 + b2


def s5_scan(bu, a_bar, x0, reverse):
    if reverse:
        bu = bu.at[:, -1].add(a_bar * x0)
    else:
        bu = bu.at[:, 0].add(a_bar * x0)
    a = jnp.broadcast_to(a_bar, bu.shape)
    _, xs = lax.associative_scan(_linear_combine, (a, bu), axis=1, reverse=reverse)
    return xs


def s5_mixer(hx, hc, w_in, lam_re, lam_im, log_dt, b_re, b_im, c_re, c_im, d_skip, w_glu,
             b_glu, w_out, with_ctx_out):
    f32 = jnp.float32
    B = hx.shape[0]
    b_mat = lax.complex(b_re.astype(f32), b_im.astype(f32))
    dirs = []
    for j in range(2):
        lam = lax.complex(lam_re[j].astype(f32), lam_im[j].astype(f32))
        dt = jnp.exp(log_dt[j].astype(f32))[:, None]
        a_bar = jnp.exp(lam * dt)
        b_bar = ((a_bar - 1.0) / lam)[..., None] * b_mat
        dirs.append((a_bar, b_bar))

    def drive(u, b_bar):
        ug = u.astype(f32).reshape(u.shape[0], u.shape[1], S5_GROUPS, S5_GROUP)
        return lax.complex(jnp.einsum('gpc,btgc->btgp', jnp.real(b_bar), ug),
                           jnp.einsum('gpc,btgc->btgp', jnp.imag(b_bar), ug))

    def scan_both(u, x0_f, x0_b):
        xs_f = s5_scan(drive(u, dirs[0][1]), dirs[0][0], x0_f, reverse=False)
        xs_b = s5_scan(drive(u, dirs[1][1]), dirs[1][0], x0_b, reverse=True)
        return xs_f, xs_b

    def readout(xs, u, z):
        y = jnp.einsum('gcp,btgp->btgc', c_re.astype(f32), jnp.real(xs)) \
            - jnp.einsum('gcp,btgp->btgc', c_im.astype(f32), jnp.imag(xs))
        y = y.reshape(u.shape) + d_skip.astype(f32) * u.astype(f32)
        s = jax.nn.gelu(y).astype(u.dtype)
        s = s * jax.nn.sigmoid(s @ w_glu + b_glu)
        return (s * jax.nn.silu(z)) @ w_out

    u_c, z_c = jnp.split(hc @ w_in, 2, -1)
    zero = jnp.zeros((B, S5_GROUPS, S5_STATE), jnp.complex64)
    xc_f, xc_b = scan_both(u_c, zero, zero)
    u_x, z_x = jnp.split(hx @ w_in, 2, -1)
    xx_f, xx_b = scan_both(u_x, xc_f[:, -1], xc_b[:, 0])
    y_x = readout(xx_f + xx_b, u_x, z_x)
    y_c = readout(xc_f + xc_b, u_c, z_c) if with_ctx_out else None
    return y_x, y_c


def setup_inputs(seed: int = 0) -> dict:
    key = jax.random.key(seed)
    keys = list(jax.random.split(key, 48))

    def nrm(shape, std):
        return std * jax.random.normal(keys.pop(), shape, jnp.float32)

    D = D_MODEL
    EA, EB, EC = NA_WIDTH, ML_WIDTH, S5_WIDTH
    NH, G, P, GS = ML_HEADS, S5_GROUPS, S5_STATE, S5_GROUP
    gate_i = nrm((N_B, 2, 1, NH), 0.1)
    gate_f = jnp.linspace(3.0, 6.0, NH, dtype=jnp.float32) + nrm((N_B, 2, 1, NH), 0.1)
    ml_b_gate = jnp.concatenate([gate_i, gate_f], axis=2).reshape(N_B, 4 * NH)
    s5_log_dt = jax.random.uniform(keys.pop(), (N_C, 2, G), jnp.float32,
                                   math.log(DT_MIN), math.log(DT_MAX))
    s5_lam_re = -0.5 + nrm((N_C, 2, G, P), 0.01)
    s5_lam_im = jnp.pi * jnp.arange(P, dtype=jnp.float32) + nrm((N_C, 2, G, P), 0.01)
    return {
        'x': nrm((BATCH, SEQ, D), 1.0),
        'c': nrm((BATCH, D), 1.0),
        'ctx': nrm((BATCH, CTX_LEN, D), 1.0),
        'c_ctx': nrm((D,), 1.0),
        'norm_g': 1.0 + nrm((DEPTH, D), 0.02),
        'ada_w': nrm((DEPTH, D, 3 * D), 0.5 * D ** -0.5),
        'ada_b': nrm((DEPTH, 3 * D), 0.02),
        'na_w_in': nrm((N_A, D, 4 * EA), D ** -0.5),
        'na_rpb': nrm((N_A, NA_HEADS, 2 * WIN_H - 1, 2 * WIN_W - 1), 0.1),
        'na_w_out': nrm((N_A, EA, D), EA ** -0.5),
        'ml_w_in': nrm((N_B, D, 2 * EB), D ** -0.5),
        'ml_conv_w': nrm((N_B, ML_CONV, EB), 0.5),
        'ml_conv_b': nrm((N_B, EB), 0.02),
        'ml_wq': nrm((N_B, EB // ML_BLOCK, ML_BLOCK, ML_BLOCK), ML_BLOCK ** -0.5),
        'ml_wk': nrm((N_B, EB // ML_BLOCK, ML_BLOCK, ML_BLOCK), ML_BLOCK ** -0.5),
        'ml_wv': nrm((N_B, EB // ML_BLOCK, ML_BLOCK, ML_BLOCK), ML_BLOCK ** -0.5),
        'ml_w_gate': nrm((N_B, 3 * EB, 4 * NH), 0.1 * (3 * EB) ** -0.5),
        'ml_b_gate': ml_b_gate,
        'ml_skip': 1.0 + nrm((N_B, EB), 0.02),
        'ml_norm_g': 1.0 + nrm((N_B, EB), 0.02),
        'ml_w_out': nrm((N_B, EB, D), EB ** -0.5),
        's5_w_in': nrm((N_C, D, 2 * EC), D ** -0.5),
        's5_lam_re': s5_lam_re,
        's5_lam_im': s5_lam_im,
        's5_log_dt': s5_log_dt,
        's5_b_re': nrm((N_C, G, P, GS), (2 * GS) ** -0.5),
        's5_b_im': nrm((N_C, G, P, GS), (2 * GS) ** -0.5),
        's5_c_re': nrm((N_C, G, GS, P), (2 * P) ** -0.5),
        's5_c_im': nrm((N_C, G, GS, P), (2 * P) ** -0.5),
        's5_d': nrm((N_C, EC), 1.0),
        's5_w_glu': nrm((N_C, EC, EC), EC ** -0.5),
        's5_b_glu': nrm((N_C, EC), 0.02),
        's5_w_out': nrm((N_C, EC, D), EC ** -0.5),
        'final_g': 1.0 + nrm((D,), 0.02),
    }


def reference(x, c, ctx, c_ctx, norm_g, ada_w, ada_b, na_w_in, na_rpb, na_w_out,
              ml_w_in, ml_conv_w, ml_conv_b, ml_wq, ml_wk, ml_wv, ml_w_gate, ml_b_gate,
              ml_skip, ml_norm_g, ml_w_out, s5_w_in, s5_lam_re, s5_lam_im, s5_log_dt,
              s5_b_re, s5_b_im, s5_c_re, s5_c_im, s5_d, s5_w_glu, s5_b_glu, s5_w_out,
              final_g):
    ia = ib = ic = 0
    for layer in range(DEPTH):
        with_ctx_out = layer < DEPTH - 1
        mod_x = jax.nn.silu(c) @ ada_w[layer] + ada_b[layer]
        sh_x, sc_x, g_x = jnp.split(mod_x[:, None, :], 3, -1)
        mod_c = jax.nn.silu(c_ctx) @ ada_w[layer] + ada_b[layer]
        sh_c, sc_c, g_c = jnp.split(mod_c, 3, -1)
        hx = rmsnorm(x, norm_g[layer]) * (1.0 + sc_x) + sh_x
        hc = rmsnorm(ctx, norm_g[layer]) * (1.0 + sc_c) + sh_c
        kind = layer % N_MIXERS
        if kind == 0:
            y_x, y_c = na_mixer(hx, hc, na_w_in[ia], na_rpb[ia], na_w_out[ia], with_ctx_out)
            ia += 1
        elif kind == 1:
            y_x, y_c = mlstm_mixer(hx, hc, ml_w_in[ib], ml_conv_w[ib], ml_conv_b[ib], ml_wq[ib],
                                   ml_wk[ib], ml_wv[ib], ml_w_gate[ib], ml_b_gate[ib],
                                   ml_skip[ib], ml_norm_g[ib], ml_w_out[ib], with_ctx_out)
            ib += 1
        else:
            y_x, y_c = s5_mixer(hx, hc, s5_w_in[ic], s5_lam_re[ic], s5_lam_im[ic], s5_log_dt[ic],
                                s5_b_re[ic], s5_b_im[ic], s5_c_re[ic], s5_c_im[ic], s5_d[ic],
                                s5_w_glu[ic], s5_b_glu[ic], s5_w_out[ic], with_ctx_out)
            ic += 1
        x = x + g_x * y_x
        if with_ctx_out:
            ctx = ctx + g_c * y_c
    return rmsnorm(x, final_g)
```

```python
import functools
import math

import numpy as np
import jax
import jax.numpy as jnp
from jax import lax
from jax.experimental import pallas as pl
from jax.experimental.pallas import tpu as pltpu

F32 = jnp.float32
BF16 = jnp.bfloat16

D_MODEL = 2048
SEQ = 8192
CTX_LEN = 256
TOKENS = CTX_LEN + SEQ
DEPTH = 4
EPS = 1e-6
NEG = -1e30

GRID_W = 64
GRID_ROWS = SEQ // GRID_W
NA_HEADS = 16
NA_HEAD_DIM = 128
WIN_H = 8
WIN_W = 16
NA_QROWS = 4
NA_BLK = NA_QROWS * GRID_W
NA_WIN_BLKS = 3
NA_HEADS_PER_STEP = 4

ML_WIDTH = 2 * D_MODEL
ML_HEADS = 8
ML_HEAD_DIM = ML_WIDTH // ML_HEADS
ML_BLOCK = 4
ML_CHUNK = 256
ML_COLS = 256

S5_WIDTH = D_MODEL
S5_GROUP = 16
S5_GROUPS = S5_WIDTH // S5_GROUP
S5_STATE = 64
S5_L = 16
S5_SEG = 8
S5_CHUNKS = TOKENS // S5_L
S5_CTX_CHUNKS = CTX_LEN // S5_L
S5_X_CHUNKS = SEQ // S5_L
S5_PAIR = 2 * S5_L * S5_GROUP

VMEM_LIMIT = 56 * 1024 * 1024


def _cparams(*sem):
    return pltpu.CompilerParams(dimension_semantics=sem, vmem_limit_bytes=VMEM_LIMIT)


def _dot(a, b):
    return jnp.dot(a, b, preferred_element_type=F32)


def _dot_nt(a, b):
    return lax.dot_general(a, b, (((1,), (1,)), ((), ())), preferred_element_type=F32)


def _silu(x):
    return x * jax.nn.sigmoid(x)


def _row_select(i, tm, ctx_row, x_row):
    rows = i * tm + lax.broadcasted_iota(jnp.int32, (tm, 1), 0)
    return jnp.where(rows < CTX_LEN, ctx_row, x_row)


def _ada_kernel(c_ref, w_ref, b_ref, o_ref):
    cs = _silu(c_ref[...]).astype(BF16)
    o_ref[0] = _dot(cs, w_ref[0].astype(BF16)) + b_ref[0]


def ada_mods(c8, ada_w, ada_b):
    tn = 512
    n = 3 * D_MODEL
    return pl.pallas_call(
        _ada_kernel,
        out_shape=jax.ShapeDtypeStruct((DEPTH, 8, n), F32),
        grid=(DEPTH, n // tn),
        in_specs=[pl.BlockSpec((8, D_MODEL), lambda l, j: (0, 0)),
                  pl.BlockSpec((1, D_MODEL, tn), lambda l, j: (l, 0, j)),
                  pl.BlockSpec((1, 1, tn), lambda l, j: (l, 0, j))],
        out_specs=pl.BlockSpec((1, 8, tn), lambda l, j: (l, 0, j)),
        compiler_params=_cparams("arbitrary", "arbitrary"),
    )(c8, ada_w, ada_b.reshape(DEPTH, 1, n))


def _norm_mm_kernel(x_ref, g_ref, sh_ref, sc_ref, w_ref, o_ref, h_ref, *, tm):
    i = pl.program_id(0)

    @pl.when(pl.program_id(1) == 0)
    def _():
        x = x_ref[...]
        y = x * lax.rsqrt(jnp.mean(x * x, axis=-1, keepdims=True) + EPS) * g_ref[...]
        sc = _row_select(i, tm, sc_ref[1:2, :], sc_ref[0:1, :])
        sh = _row_select(i, tm, sh_ref[1:2, :], sh_ref[0:1, :])
        h_ref[...] = (y * (1.0 + sc) + sh).astype(BF16)

    o_ref[...] = _dot(h_ref[...], w_ref[...]).astype(o_ref.dtype)


def norm_mm(xc, g, mods, w, col0, ncols, out_dtype, tm=768, tn=1024):
    d = D_MODEL
    cb = col0 // tn
    return pl.pallas_call(
        functools.partial(_norm_mm_kernel, tm=tm),
        out_shape=jax.ShapeDtypeStruct((TOKENS, ncols), out_dtype),
        grid=(TOKENS // tm, ncols // tn),
        in_specs=[pl.BlockSpec((tm, d), lambda i, j: (i, 0)),
                  pl.BlockSpec((1, d), lambda i, j: (0, 0)),
                  pl.BlockSpec((8, d), lambda i, j: (0, 0)),
                  pl.BlockSpec((8, d), lambda i, j: (0, 1)),
                  pl.BlockSpec((d, tn), lambda i, j: (0, cb + j))],
        out_specs=pl.BlockSpec((tm, tn), lambda i, j: (i, j)),
        scratch_shapes=[pltpu.VMEM((tm, d), BF16)],
        compiler_params=_cparams("arbitrary", "arbitrary"),
    )(xc, g.reshape(1, d), mods, mods, w)


def _gated_out_kernel(a_ref, z_ref, w_ref, x_ref, gate_ref, o_ref, *, tm):
    i = pl.program_id(0)
    lhs = (a_ref[...].astype(F32) * _silu(z_ref[...])).astype(BF16)
    gate = _row_select(i, tm, gate_ref[1:2, :], gate_ref[0:1, :])
    o_ref[...] = x_ref[...] + gate * _dot(lhs, w_ref[...])


def gated_out(a, zsrc, zcol0, w, xc, mods, tm=384):
    k = a.shape[1]
    d = D_MODEL
    zb = zcol0 // k
    return pl.pallas_call(
        functools.partial(_gated_out_kernel, tm=tm),
        out_shape=jax.ShapeDtypeStruct((TOKENS, d), F32),
        grid=(TOKENS // tm,),
        in_specs=[pl.BlockSpec((tm, k), lambda i: (i, 0)),
                  pl.BlockSpec((tm, k), lambda i: (i, zb)),
                  pl.BlockSpec((k, d), lambda i: (0, 0)),
                  pl.BlockSpec((tm, d), lambda i: (i, 0)),
                  pl.BlockSpec((8, d), lambda i: (0, 2))],
        out_specs=pl.BlockSpec((tm, d), lambda i: (i, 0)),
        compiler_params=_cparams("arbitrary"),
    )(a, zsrc, w, xc, mods)


def _na_bias_index():
    n_rel = (2 * WIN_H - 1) * (2 * WIN_W - 1)
    kb = NA_WIN_BLKS * NA_QROWS
    idx = np.full((4, NA_QROWS, GRID_W, kb, GRID_W), n_rel, np.int32)
    cq = np.arange(GRID_W)
    c0 = np.clip(cq - WIN_W // 2, 0, GRID_W - WIN_W)
    ck = np.arange(GRID_W)
    col_ok = (ck[None, :] >= c0[:, None]) & (ck[None, :] < c0[:, None] + WIN_W)
    col_rel = ck[None, :] - cq[:, None] + (WIN_W - 1)
    nblk = GRID_ROWS // NA_QROWS
    for kind, g in ((0, 0), (1, 1), (2, nblk - 1)):
        band0 = NA_QROWS * int(np.clip(g - 1, 0, nblk - NA_WIN_BLKS))
        for i in range(NA_QROWS):
            rq = NA_QROWS * g + i
            r0 = int(np.clip(rq - WIN_H // 2, 0, GRID_ROWS - WIN_H))
            for j in range(kb):
                rk = band0 + j
                if r0 <= rk < r0 + WIN_H:
                    rel = (rk - rq + WIN_H - 1) * (2 * WIN_W - 1) + col_rel
                    idx[kind, i, :, j, :] = np.where(col_ok, rel, n_rel)
    return idx.reshape(4, NA_BLK, kb * GRID_W)


def na_bias_table(rpb):
    idx = _na_bias_index()
    flat = jnp.concatenate([rpb.reshape(NA_HEADS, -1).astype(F32),
                            jnp.full((NA_HEADS, 1), NEG, F32)], axis=1)
    tab = jnp.take(flat, jnp.asarray(idx.reshape(-1)), axis=1)
    return tab.reshape(NA_HEADS, 4, NA_BLK, NA_WIN_BLKS * NA_BLK).transpose(1, 0, 2, 3)


def _na_kernel(q_ref, kc_ref, k0_ref, k1_ref, k2_ref, vc_ref, v0_ref, v1_ref, v2_ref,
               bias_ref, o_ref, *, heads, scale):
    for h in range(heads):
        sl = slice(h * NA_HEAD_DIM, (h + 1) * NA_HEAD_DIM)
        q = q_ref[:, sl]
        s = [_dot_nt(q, kc_ref[:, sl]) * scale]
        for b, k_ref in enumerate((k0_ref, k1_ref, k2_ref)):
            s.append(_dot_nt(q, k_ref[:, sl]) * scale
                     + bias_ref[0, h, :, b * NA_BLK:(b + 1) * NA_BLK])
        m = s[0].max(axis=-1, keepdims=True)
        for t in s[1:]:
            m = jnp.maximum(m, t.max(axis=-1, keepdims=True))
        acc = None
        den = None
        for t, v_ref in zip(s, (vc_ref, v0_ref, v1_ref, v2_ref)):
            p = jnp.exp(t - m)
            l = p.sum(axis=-1, keepdims=True)
            pv = _dot(p.astype(BF16), v_ref[:, sl])
            acc = pv if acc is None else acc + pv
            den = l if den is None else den + l
        o_ref[:, sl] = acc / den


def na_attention(qkv, bias_tab, with_ctx_out):
    hs = NA_HEADS_PER_STEP
    hw = hs * NA_HEAD_DIM
    nh = NA_HEADS // hs
    nq = TOKENS // NA_BLK
    q0 = 0 if with_ctx_out else 1
    nlat = GRID_ROWS // NA_QROWS

    def band(qb):
        return 1 + jnp.clip(qb - 2, 0, nlat - NA_WIN_BLKS)

    def kind(qb):
        return jnp.where(qb == 0, 3, jnp.where(qb == 1, 0, jnp.where(qb == nq - 1, 2, 1)))

    def spec(col_base, fn):
        return pl.BlockSpec((NA_BLK, hw), lambda h, g: (fn(g + q0), col_base + h))

    kspecs = [spec(nh, lambda qb: 0 * qb)] + [
        spec(nh, lambda qb, b=b: band(qb) + b) for b in range(NA_WIN_BLKS)]
    vspecs = [spec(2 * nh, lambda qb: 0 * qb)] + [
        spec(2 * nh, lambda qb, b=b: band(qb) + b) for b in range(NA_WIN_BLKS)]
    return pl.pallas_call(
        functools.partial(_na_kernel, heads=hs, scale=NA_HEAD_DIM ** -0.5),
        out_shape=jax.ShapeDtypeStruct((TOKENS, D_MODEL), F32),
        grid=(nh, nq - q0),
        in_specs=[spec(0, lambda qb: qb)] + kspecs + vspecs + [
            pl.BlockSpec((1, hs, NA_BLK, NA_WIN_BLKS * NA_BLK),
                         lambda h, g: (kind(g + q0), h, 0, 0))],
        out_specs=pl.BlockSpec((NA_BLK, hw), lambda h, g: (g + q0, h)),
        compiler_params=_cparams("arbitrary", "arbitrary"),
    )(*([qkv] * 9), bias_tab)


def _ml_pre_kernel(x_ref, prev_ref, next_ref, cw_ref, cb_ref, wq_ref, wk_ref, wv_ref, wvt_ref,
                   wg_ref, bg_ref, xc_ref, q_ref, k_ref, vt_ref, g_ref, *, tm):
    i = pl.program_id(0)
    j = pl.program_id(1)
    x = x_ref[...]
    rows = i * tm + lax.broadcasted_iota(jnp.int32, (tm, 1), 0)
    local = lax.broadcasted_iota(jnp.int32, (tm, 1), 0)
    x_prev = jnp.where(local == 0, prev_ref[7:8, :], pltpu.roll(x, 1, 0))
    x_prev = jnp.where((rows == 0) | (rows == CTX_LEN), 0.0, x_prev)
    x_next = jnp.where(local == tm - 1, next_ref[0:1, :], pltpu.roll(x, tm - 1, 0))
    x_next = jnp.where((rows == CTX_LEN - 1) | (rows == TOKENS - 1), 0.0, x_next)
    conv = cb_ref[...] + x_prev * cw_ref[0:1, :] + x * cw_ref[1:2, :] + x_next * cw_ref[2:3, :]
    xconv = _silu(conv)
    xc_ref[...] = xconv
    xcb = xconv.astype(BF16)
    xb = x.astype(BF16)
    q = _dot(xcb, wq_ref[0])
    k = _dot(xcb, wk_ref[0])
    v = _dot(xb, wv_ref[0])
    qb, kb, vb = q.astype(BF16), k.astype(BF16), v.astype(BF16)
    q_ref[...] = qb
    k_ref[...] = (k * (ML_HEAD_DIM ** -0.5)).astype(BF16)
    vt_ref[...] = _dot_nt(wvt_ref[0], xb).astype(BF16)
    part = _dot_nt(wg_ref[0], qb) + _dot_nt(wg_ref[1], kb) + _dot_nt(wg_ref[2], vb)

    @pl.when(j == 0)
    def _():
        g_ref[...] = jnp.broadcast_to(bg_ref[...], g_ref.shape)

    g_ref[...] += part


def ml_pre(xz, conv_w, conv_b, wq_bd, wk_bd, wv_bd, wvt_bd, wg_t, b_gate, tm=768):
    e = ML_WIDTH
    c = ML_COLS
    nc = e // c
    last8 = TOKENS // 8 - 1
    tok = lambda i, j: (i, j)
    outs = pl.pallas_call(
        functools.partial(_ml_pre_kernel, tm=tm),
        out_shape=[jax.ShapeDtypeStruct((TOKENS, e), F32),
                   jax.ShapeDtypeStruct((TOKENS, e), BF16),
                   jax.ShapeDtypeStruct((TOKENS, e), BF16),
                   jax.ShapeDtypeStruct((e, TOKENS), BF16),
                   jax.ShapeDtypeStruct((4 * ML_HEADS, TOKENS), F32)],
        grid=(TOKENS // tm, nc),
        in_specs=[pl.BlockSpec((tm, c), tok),
                  pl.BlockSpec((8, c), lambda i, j: (jnp.maximum(i * (tm // 8) - 1, 0), j)),
                  pl.BlockSpec((8, c), lambda i, j: (jnp.minimum((i + 1) * (tm // 8), last8), j)),
                  pl.BlockSpec((3, c), lambda i, j: (0, j)),
                  pl.BlockSpec((1, c), lambda i, j: (0, j)),
                  pl.BlockSpec((1, c, c), lambda i, j: (j, 0, 0)),
                  pl.BlockSpec((1, c, c), lambda i, j: (j, 0, 0)),
                  pl.BlockSpec((1, c, c), lambda i, j: (j, 0, 0)),
                  pl.BlockSpec((1, c, c), lambda i, j: (j, 0, 0)),
                  pl.BlockSpec((3, 4 * ML_HEADS, c), lambda i, j: (0, 0, j)),
                  pl.BlockSpec((4 * ML_HEADS, 1), lambda i, j: (0, 0))],
        out_specs=[pl.BlockSpec((tm, c), tok), pl.BlockSpec((tm, c), tok),
                   pl.BlockSpec((tm, c), tok),
                   pl.BlockSpec((c, tm), lambda i, j: (j, i)),
                   pl.BlockSpec((4 * ML_HEADS, tm), lambda i, j: (0, i))],
        compiler_params=_cparams("arbitrary", "arbitrary"),
    )(xz, xz, xz, conv_w, conv_b.reshape(1, e), wq_bd, wk_bd, wv_bd, wvt_bd, wg_t,
      b_gate.reshape(4 * ML_HEADS, 1))
    return outs


def _split3(x):
    hi = x.astype(BF16).astype(F32)
    r1 = x - hi
    mid = r1.astype(BF16).astype(F32)
    lo = (r1 - mid).astype(BF16).astype(F32)
    return hi, mid, lo


def _ml_step(q, k, vt, i_row, f_row, c_ref, n_ref, m_ref, h_ref, reverse):
    L = ML_CHUNK
    logf = jax.nn.log_sigmoid(f_row)
    hi, mid, lo = _split3(logf)
    rid = lax.broadcasted_iota(jnp.int32, (16, L), 0)
    r16 = jnp.where(rid == 0, hi, jnp.where(rid == 1, mid, jnp.where(rid == 2, lo, 0.0)))
    r16 = r16.astype(BF16)
    si = lax.broadcasted_iota(jnp.int32, (L, L), 0)
    ti = lax.broadcasted_iota(jnp.int32, (L, L), 1)
    tri = (si >= ti) if reverse else (si <= ti)
    rsum = _dot(r16, jnp.where(tri, 1.0, 0.0).astype(BF16))
    b_row = rsum[0:1] + rsum[1:2] + rsum[2:3]
    tri_t = (ti >= si) if reverse else (ti <= si)
    csum = _dot_nt(jnp.where(tri_t, 1.0, 0.0).astype(BF16), r16)
    b_col = csum[:, 0:1] + csum[:, 1:2] + csum[:, 2:3]
    m_prev = m_ref[...]
    a_col = b_col + m_prev
    valid = tri_t
    dmat = jnp.where(valid, b_col - b_row + i_row, NEG)
    m_t = jnp.maximum(a_col, dmat.max(axis=-1, keepdims=True))
    w_inter = jnp.exp(a_col - m_t)
    s = _dot_nt(q, k) * jnp.exp(dmat - m_t)
    num = w_inter * _dot_nt(q, c_ref[...].astype(BF16)) + _dot_nt(s.astype(BF16), vt)
    qn = _dot_nt(q, n_ref[...].astype(BF16))[:, 0:1]
    den = w_inter * qn + s.sum(axis=-1, keepdims=True)
    h_ref[...] = num / jnp.maximum(jnp.abs(den), jnp.exp(-m_t))
    b_end = b_row[:, 0:1] if reverse else b_row[:, L - 1:L]
    g_row = b_end - b_row + i_row
    m_new = jnp.maximum(b_end + m_prev, g_row.max(axis=-1, keepdims=True))
    decay = jnp.exp(b_end + m_prev - m_new)
    w_row = jnp.exp(g_row - m_new)
    c_ref[...] = decay * c_ref[...] + _dot((vt.astype(F32) * w_row).astype(BF16), k)
    w8 = jnp.broadcast_to(w_row, (8, L)).astype(BF16)
    n_ref[...] = decay * n_ref[...] + _dot(w8, k)
    m_ref[...] = m_new


def _ml_rec_kernel(qf_ref, kf_ref, vtf_ref, gf_ref, qb_ref, kb_ref, vtb_ref, gb_ref,
                   hf_ref, hb_ref, cf_ref, nf_ref, mf_ref, cb_ref, nb_ref, mb_ref):
    h = pl.program_id(0)

    @pl.when(pl.program_id(1) == 0)
    def _():
        for r in (cf_ref, nf_ref, mf_ref, cb_ref, nb_ref, mb_ref):
            r[...] = jnp.zeros(r.shape, F32)

    nh = ML_HEADS
    _ml_step(qf_ref[...], kf_ref[...], vtf_ref[...],
             gf_ref[pl.ds(h, 1), :], gf_ref[pl.ds(nh + h, 1), :],
             cf_ref, nf_ref, mf_ref, hf_ref, reverse=False)
    _ml_step(qb_ref[...], kb_ref[...], vtb_ref[...],
             gb_ref[pl.ds(2 * nh + h, 1), :], gb_ref[pl.ds(3 * nh + h, 1), :],
             cb_ref, nb_ref, mb_ref, hb_ref, reverse=True)


def ml_recurrence(q, k, vt, gates_t):
    L = ML_CHUNK
    dh = ML_HEAD_DIM
    nchunk = TOKENS // L
    fwd = lambda j: j
    bwd = lambda j: jnp.where(j == 0, 0, nchunk - j)
    specs = []
    for cm in (fwd, bwd):
        specs += [pl.BlockSpec((L, dh), lambda h, j, cm=cm: (cm(j), h)),
                  pl.BlockSpec((L, dh), lambda h, j, cm=cm: (cm(j), h)),
                  pl.BlockSpec((dh, L), lambda h, j, cm=cm: (h, cm(j))),
                  pl.BlockSpec((4 * ML_HEADS, L), lambda h, j, cm=cm: (0, cm(j)))]
    state = [pltpu.VMEM((dh, dh), F32), pltpu.VMEM((8, dh), F32), pltpu.VMEM((1, 1), F32)]
    return pl.pallas_call(
        _ml_rec_kernel,
        out_shape=[jax.ShapeDtypeStruct((TOKENS, ML_WIDTH), F32)] * 2,
        grid=(ML_HEADS, nchunk),
        in_specs=specs,
        out_specs=[pl.BlockSpec((L, dh), lambda h, j: (fwd(j), h)),
                   pl.BlockSpec((L, dh), lambda h, j: (bwd(j), h))],
        scratch_shapes=state + state,
        compiler_params=_cparams("arbitrary", "arbitrary"),
    )(q, k, vt, gates_t, q, k, vt, gates_t)


def _ml_out_kernel(hf_ref, hb_ref, xc_ref, z_ref, ng_ref, skip_ref, w_ref, x_ref, gate_ref,
                   o_ref, *, tm):
    i = pl.program_id(0)
    kk = pl.program_id(1)
    h = hf_ref[...] + hb_ref[...]
    mu = jnp.mean(h, axis=-1, keepdims=True)
    hc = h - mu
    var = jnp.mean(hc * hc, axis=-1, keepdims=True)
    hn = hc * lax.rsqrt(var + EPS) * ng_ref[...]
    lhs = ((hn + skip_ref[...] * xc_ref[...]) * _silu(z_ref[...])).astype(BF16)
    part = _dot(lhs, w_ref[...])

    @pl.when(kk == 0)
    def _():
        o_ref[...] = part

    @pl.when(kk > 0)
    def _():
        o_ref[...] += part

    @pl.when(kk == pl.num_programs(1) - 1)
    def _():
        gate = _row_select(i, tm, gate_ref[1:2, :], gate_ref[0:1, :])
        o_ref[...] = x_ref[...] + gate * o_ref[...]


def ml_out(hf, hb, xconv, xz, norm_g, skip, w_out, xc, mods, tm=768):
    e = ML_WIDTH
    dh = ML_HEAD_DIM
    d = D_MODEL
    nh = ML_HEADS
    hd = lambda i, k: (i, k)
    return pl.pallas_call(
        functools.partial(_ml_out_kernel, tm=tm),
        out_shape=jax.ShapeDtypeStruct((TOKENS, d), F32),
        grid=(TOKENS // tm, nh),
        in_specs=[pl.BlockSpec((tm, dh), hd), pl.BlockSpec((tm, dh), hd),
                  pl.BlockSpec((tm, dh), hd),
                  pl.BlockSpec((tm, dh), lambda i, k: (i, nh + k)),
                  pl.BlockSpec((1, dh), lambda i, k: (0, k)),
                  pl.BlockSpec((1, dh), lambda i, k: (0, k)),
                  pl.BlockSpec((dh, d), lambda i, k: (k, 0)),
                  pl.BlockSpec((tm, d), lambda i, k: (i, 0)),
                  pl.BlockSpec((8, d), lambda i, k: (0, 2))],
        out_specs=pl.BlockSpec((tm, d), lambda i, k: (i, 0)),
        compiler_params=_cparams("arbitrary", "arbitrary"),
    )(hf, hb, xconv, xz, norm_g.reshape(1, e), skip.reshape(1, e), w_out, xc, mods)


def _cmul(ar, ai, br, bi):
    return ar * br - ai * bi, ar * bi + ai * br


def s5_operators(lam_re, lam_im, log_dt, b_re, b_im, c_re, c_im):
    hp = lax.Precision.HIGHEST
    G, P, GS, L = S5_GROUPS, S5_STATE, S5_GROUP, S5_L
    bre, bim = b_re.astype(F32), b_im.astype(F32)
    cre, cim = c_re.astype(F32), c_im.astype(F32)
    msum = 0.0
    bcs, ccs, arows = [], [], []
    for j in range(2):
        lr, li = lam_re[j].astype(F32), lam_im[j].astype(F32)
        dt = jnp.exp(log_dt[j].astype(F32))[:, None]
        mag = jnp.exp(lr * dt)
        ar, ai = mag * jnp.cos(li * dt), mag * jnp.sin(li * dt)
        den = lr * lr + li * li
        kr = ((ar - 1.0) * lr + ai * li) / den
        ki = (ai * lr - (ar - 1.0) * li) / den
        bbr = kr[..., None] * bre - ki[..., None] * bim
        bbi = kr[..., None] * bim + ki[..., None] * bre
        pr, pi = [jnp.ones_like(ar)], [jnp.zeros_like(ar)]
        for _ in range(L):
            r, i = _cmul(pr[-1], pi[-1], ar, ai)
            pr.append(r)
            pi.append(i)
        pr, pi = jnp.stack(pr), jnp.stack(pi)
        car = cre[None] * pr[:L, :, None, :] - cim[None] * pi[:L, :, None, :]
        cai = cre[None] * pi[:L, :, None, :] + cim[None] * pr[:L, :, None, :]
        kk = (jnp.einsum('tgcp,gpd->tgcd', car, bbr, precision=hp)
              - jnp.einsum('tgcp,gpd->tgcd', cai, bbi, precision=hp))
        kk = jnp.concatenate([kk, jnp.zeros((1, G, GS, GS), F32)], 0)
        s_i, t_i = np.meshgrid(np.arange(L), np.arange(L), indexing='ij')
        lag = (t_i - s_i) if j == 0 else (s_i - t_i)
        lag = np.where(lag >= 0, lag, L)
        m = jnp.take(kk, jnp.asarray(lag.reshape(-1)), axis=0).reshape(L, L, G, GS, GS)
        msum = msum + m.transpose(2, 0, 4, 1, 3).reshape(G, L * GS, L * GS)
        pw = np.arange(L - 1, -1, -1) if j == 0 else np.arange(L)
        er, ei = _cmul(pr[pw][..., None], pi[pw][..., None], bbr[None], bbi[None])
        bc_re = er.transpose(1, 0, 3, 2).reshape(G, L * GS, P)
        bc_im = ei.transpose(1, 0, 3, 2).reshape(G, L * GS, P)
        bcs.append((bc_re, bc_im))
        pw = np.arange(1, L + 1) if j == 0 else np.arange(L, 0, -1)
        fr = cre[None] * pr[pw][:, :, None, :] - cim[None] * pi[pw][:, :, None, :]
        fi = cre[None] * pi[pw][:, :, None, :] + cim[None] * pr[pw][:, :, None, :]
        cc_re = fr.transpose(1, 3, 0, 2).reshape(G, P, L * GS)
        cc_im = (-fi).transpose(1, 3, 0, 2).reshape(G, P, L * GS)
        ccs.append((cc_re, cc_im))
        pows = {}
        r, i = pr[L], pi[L]
        pows[1] = (r, i)
        n = 1
        while n < S5_X_CHUNKS // S5_SEG:
            r, i = _cmul(r, i, r, i)
            n *= 2
            pows[n] = (r, i)
        arows.append({kq: (vr.reshape(1, G * P), vi.reshape(1, G * P))
                      for kq, (vr, vi) in pows.items()})

    def pair_diag(x):
        g, r, c = x.shape
        x = x.reshape(g // 2, 2, r, c)
        eye = jnp.eye(2, dtype=x.dtype)
        return jnp.einsum('hirc,ij->hirjc', x, eye).reshape(g // 2, 2 * r, 2 * c)

    msum_p = pair_diag(msum).astype(BF16)
    bc = jnp.stack([jnp.concatenate([pair_diag(br_), pair_diag(bi_)], -1)
                    for br_, bi_ in bcs]).astype(BF16)
    cc = jnp.stack([jnp.stack([pair_diag(cr_), pair_diag(ci_)])
                    for cr_, ci_ in ccs]).astype(BF16)
    return msum_p, bc, cc, arows


def _s5_drive_kernel(u_ref, bcf_ref, bcb_ref, efr_ref, efi_ref, ebr_ref, ebi_ref, *, pairs):
    for p in range(pairs):
        u = u_ref[:, p * S5_PAIR:(p + 1) * S5_PAIR].astype(BF16)
        sl = slice(p * 128, (p + 1) * 128)
        ef = _dot(u, bcf_ref[0, p])
        efr_ref[:, sl] = ef[:, :128]
        efi_ref[:, sl] = ef[:, 128:]
        eb = _dot(u, bcb_ref[0, p])
        ebr_ref[:, sl] = eb[:, :128]
        ebi_ref[:, sl] = eb[:, 128:]


def s5_drive(u_chunks, bc, pairs=4):
    npair = S5_GROUPS // 2
    gp = S5_GROUPS * S5_STATE
    out = jax.ShapeDtypeStruct((S5_CHUNKS, gp), F32)
    ospec = pl.BlockSpec((S5_CHUNKS, pairs * 128), lambda i: (0, i))
    return pl.pallas_call(
        functools.partial(_s5_drive_kernel, pairs=pairs),
        out_shape=[out] * 4,
        grid=(npair // pairs,),
        in_specs=[pl.BlockSpec((S5_CHUNKS, pairs * S5_PAIR), lambda i: (0, i)),
                  pl.BlockSpec((1, pairs, S5_PAIR, 256), lambda i: (0, i, 0, 0)),
                  pl.BlockSpec((1, pairs, S5_PAIR, 256), lambda i: (1, i, 0, 0))],
        out_specs=[ospec] * 4,
        compiler_params=_cparams("arbitrary"),
    )(u_chunks, bc, bc)


def _s5_scan_kernel(er_ref, ei_ref, a1r_ref, a1i_ref, acr_ref, aci_ref, axr_ref, axi_ref,
                    xr_ref, xi_ref, inr_ref, ini_ref, *, reverse):
    a1r, a1i = a1r_ref[...], a1i_ref[...]
    cols = er_ref.shape[1]
    zero = jnp.zeros((S5_SEG, cols), F32)

    def part(row0, nj, apr, api, x0r, x0i):
        def tile(j):
            jj = (nj - 1 - j) if reverse else j
            return pl.ds(pl.multiple_of(row0 + jj * S5_SEG, S5_SEG), S5_SEG)

        def pass1(j, carry):
            xr, xi = carry
            nr, ni = _cmul(a1r, a1i, xr, xi)
            return nr + er_ref[tile(j), :], ni + ei_ref[tile(j), :]

        endr, endi = lax.fori_loop(0, nj, pass1, (zero, zero))
        order = range(S5_SEG - 1, -1, -1) if reverse else range(S5_SEG)
        cr, ci = x0r, x0i
        for s in order:
            inr_ref[s:s + 1, :] = cr
            ini_ref[s:s + 1, :] = ci
            nr, ni = _cmul(apr, api, cr, ci)
            cr, ci = nr + endr[s:s + 1, :], ni + endi[s:s + 1, :]

        def pass2(j, carry):
            xr, xi = carry
            xr_ref[tile(j), :] = xr
            xi_ref[tile(j), :] = xi
            nr, ni = _cmul(a1r, a1i, xr, xi)
            return nr + er_ref[tile(j), :], ni + ei_ref[tile(j), :]

        lax.fori_loop(0, nj, pass2, (inr_ref[...], ini_ref[...]))
        return cr, ci

    z1 = jnp.zeros((1, cols), F32)
    cr, ci = part(0, S5_CTX_CHUNKS // S5_SEG, acr_ref[...], aci_ref[...], z1, z1)
    part(S5_CTX_CHUNKS, S5_X_CHUNKS // S5_SEG, axr_ref[...], axi_ref[...], cr, ci)


def s5_scan(e_re, e_im, apow, reverse, cols=1024):
    gp = S5_GROUPS * S5_STATE
    blk = pl.BlockSpec((S5_CHUNKS, cols), lambda i: (0, i))
    row = pl.BlockSpec((1, cols), lambda i: (0, i))
    out = jax.ShapeDtypeStruct((S5_CHUNKS, gp), F32)
    a1, ac, ax = apow[1], apow[S5_CTX_CHUNKS // S5_SEG], apow[S5_X_CHUNKS // S5_SEG]
    return pl.pallas_call(
        functools.partial(_s5_scan_kernel, reverse=reverse),
        out_shape=[out, out],
        grid=(gp // cols,),
        in_specs=[blk, blk] + [row] * 6,
        out_specs=[blk, blk],
        scratch_shapes=[pltpu.VMEM((S5_SEG, cols), F32), pltpu.VMEM((S5_SEG, cols), F32)],
        compiler_params=_cparams("arbitrary"),
    )(e_re, e_im, a1[0], a1[1], ac[0], ac[1], ax[0], ax[1])


def _gelu_tanh(x):
    return 0.5 * x * (1.0 + jnp.tanh(math.sqrt(2.0 / math.pi) * (x + 0.044715 * (x * x * x))))


def _s5_read_kernel(u_ref, m_ref, cc_ref, xfr_ref, xfi_ref, xbr_ref, xbi_ref, d_ref, o_ref,
                    *, pairs):
    for p in range(pairs):
        cs = slice(p * S5_PAIR, (p + 1) * S5_PAIR)
        sl = slice(p * 128, (p + 1) * 128)
        u = u_ref[:, cs]
        y = _dot(u.astype(BF16), m_ref[p])
        y += _dot(xfr_ref[:, sl].astype(BF16), cc_ref[0, 0, p])
        y += _dot(xfi_ref[:, sl].astype(BF16), cc_ref[0, 1, p])
        y += _dot(xbr_ref[:, sl].astype(BF16), cc_ref[1, 0, p])
        y += _dot(xbi_ref[:, sl].astype(BF16), cc_ref[1, 1, p])
        o_ref[:, cs] = _gelu_tanh(y + d_ref[:, cs] * u)


def s5_read(u_chunks, msum, cc, xs, d_chunks, pairs=4):
    npair = S5_GROUPS // 2
    ublk = pl.BlockSpec((S5_CHUNKS, pairs * S5_PAIR), lambda i: (0, i))
    xblk = pl.BlockSpec((S5_CHUNKS, pairs * 128), lambda i: (0, i))
    return pl.pallas_call(
        functools.partial(_s5_read_kernel, pairs=pairs),
        out_shape=jax.ShapeDtypeStruct(u_chunks.shape, F32),
        grid=(npair // pairs,),
        in_specs=[ublk,
                  pl.BlockSpec((pairs, S5_PAIR, S5_PAIR), lambda i: (i, 0, 0)),
                  pl.BlockSpec((2, 2, pairs, 128, S5_PAIR), lambda i: (0, 0, i, 0, 0)),
                  xblk, xblk, xblk, xblk,
                  pl.BlockSpec((1, pairs * S5_PAIR), lambda i: (0, i))],
        out_specs=ublk,
        compiler_params=_cparams("arbitrary"),
    )(u_chunks, msum, cc, *xs, d_chunks)


def s5_to_chunks(u):
    def part(t, nj):
        t = t.reshape(S5_SEG, nj, S5_L, S5_GROUPS, S5_GROUP)
        return t.transpose(1, 0, 3, 2, 4).reshape(nj * S5_SEG, S5_GROUPS * S5_L * S5_GROUP)
    return jnp.concatenate([part(u[:CTX_LEN], S5_CTX_CHUNKS // S5_SEG),
                            part(u[CTX_LEN:], S5_X_CHUNKS // S5_SEG)], axis=0)


def s5_from_chunks(s):
    def part(t, nj):
        t = t.reshape(nj, S5_SEG, S5_GROUPS, S5_L, S5_GROUP)
        return t.transpose(1, 0, 3, 2, 4).reshape(nj * S5_SEG * S5_L, S5_WIDTH)
    return jnp.concatenate([part(s[:S5_CTX_CHUNKS], S5_CTX_CHUNKS // S5_SEG),
                            part(s[S5_CTX_CHUNKS:], S5_X_CHUNKS // S5_SEG)], axis=0)


def _glu_kernel(s_ref, w_ref, b_ref, o_ref, lhs_ref, *, tn):
    j = pl.program_id(1)

    @pl.when(j == 0)
    def _():
        lhs_ref[...] = s_ref[...].astype(BF16)

    v = _dot(lhs_ref[...], w_ref[...]) + b_ref[...]
    sj = s_ref[:, pl.ds(pl.multiple_of(j * tn, tn), tn)]
    o_ref[...] = sj * jax.nn.sigmoid(v)


def s5_glu(s, w_glu, b_glu, tm=768, tn=1024):
    d = S5_WIDTH
    return pl.pallas_call(
        functools.partial(_glu_kernel, tn=tn),
        out_shape=jax.ShapeDtypeStruct((TOKENS, d), F32),
        grid=(TOKENS // tm, d // tn),
        in_specs=[pl.BlockSpec((tm, d), lambda i, j: (i, 0)),
                  pl.BlockSpec((d, tn), lambda i, j: (0, j)),
                  pl.BlockSpec((1, tn), lambda i, j: (0, j))],
        out_specs=pl.BlockSpec((tm, tn), lambda i, j: (i, j)),
        scratch_shapes=[pltpu.VMEM((tm, d), BF16)],
        compiler_params=_cparams("arbitrary", "arbitrary"),
    )(s, w_glu, b_glu.reshape(1, d))


def _final_norm_kernel(x_ref, g_ref, o_ref):
    x = x_ref[...]
    o_ref[...] = x * lax.rsqrt(jnp.mean(x * x, axis=-1, keepdims=True) + EPS) * g_ref[...]


def final_norm(xc, g, tm=256):
    off = CTX_LEN // tm
    return pl.pallas_call(
        _final_norm_kernel,
        out_shape=jax.ShapeDtypeStruct((SEQ, D_MODEL), F32),
        grid=(SEQ // tm,),
        in_specs=[pl.BlockSpec((tm, D_MODEL), lambda i: (i + off, 0)),
                  pl.BlockSpec((1, D_MODEL), lambda i: (0, 0))],
        out_specs=pl.BlockSpec((tm, D_MODEL), lambda i: (i, 0)),
        compiler_params=_cparams("arbitrary"),
    )(xc, g.reshape(1, D_MODEL))


def _block_diag(w, size):
    nb = size // ML_BLOCK
    w = w.reshape(-1, nb, ML_BLOCK, ML_BLOCK)
    eye = jnp.eye(nb, dtype=w.dtype)
    return jnp.einsum('jncd,nm->jncmd', w, eye).reshape(-1, size, size)


def na_layer(xc, g, mods, w_in, rpb, w_out, with_ctx_out):
    e = D_MODEL
    w_in = w_in.astype(BF16)
    qkv = norm_mm(xc, g, mods, w_in, 0, 3 * e, BF16)
    z = norm_mm(xc, g, mods, w_in, 3 * e, e, F32)
    o = na_attention(qkv, na_bias_table(rpb), with_ctx_out)
    return gated_out(o, z, 0, w_out.astype(BF16), xc, mods)


def mlstm_layer(xc, g, mods, w_in, conv_w, conv_b, wq, wk, wv, w_gate, b_gate, skip, norm_g,
                w_out):
    e = ML_WIDTH
    xz = norm_mm(xc, g, mods, w_in.astype(BF16), 0, 2 * e, F32)
    wq_bd = _block_diag(wq, ML_COLS).astype(BF16)
    wk_bd = _block_diag(wk, ML_COLS).astype(BF16)
    wv_bd = _block_diag(wv, ML_COLS).astype(BF16)
    wvt_bd = wv_bd.transpose(0, 2, 1)
    wg_t = w_gate.reshape(3, e, 4 * ML_HEADS).transpose(0, 2, 1).astype(BF16)
    xconv, q, k, vt, gates_t = ml_pre(xz, conv_w, conv_b, wq_bd, wk_bd, wv_bd, wvt_bd, wg_t,
                                      b_gate)
    hf, hb = ml_recurrence(q, k, vt, gates_t)
    return ml_out(hf, hb, xconv, xz, norm_g, skip, w_out.astype(BF16), xc, mods)


def s5_layer(xc, g, mods, w_in, lam_re, lam_im, log_dt, b_re, b_im, c_re, c_im, d_skip,
             w_glu, b_glu, w_out):
    e = S5_WIDTH
    uz = norm_mm(xc, g, mods, w_in.astype(BF16), 0, 2 * e, F32)
    msum, bc, cc, apow = s5_operators(lam_re, lam_im, log_dt, b_re, b_im, c_re, c_im)
    u_chunks = s5_to_chunks(uz[:, :e])
    efr, efi, ebr, ebi = s5_drive(u_chunks, bc)
    xfr, xfi = s5_scan(efr, efi, apow[0], reverse=False)
    xbr, xbi = s5_scan(ebr, ebi, apow[1], reverse=True)
    d_chunks = jnp.broadcast_to(d_skip.astype(F32).reshape(1, S5_GROUPS, 1, S5_GROUP),
                                (1, S5_GROUPS, S5_L, S5_GROUP)).reshape(1, -1)
    s_chunks = s5_read(u_chunks, msum, cc, (xfr, xfi, xbr, xbi), d_chunks)
    s = s5_from_chunks(s_chunks)
    s2 = s5_glu(s, w_glu.astype(BF16), b_glu)
    return gated_out(s2, uz, e, w_out.astype(BF16), xc, mods)


def kernel(x, c, ctx, c_ctx, norm_g, ada_w, ada_b, na_w_in, na_rpb, na_w_out, ml_w_in, ml_conv_w, ml_conv_b, ml_wq, ml_wk, ml_wv, ml_w_gate, ml_b_gate, ml_skip, ml_norm_g, ml_w_out, s5_w_in, s5_lam_re, s5_lam_im, s5_log_dt, s5_b_re, s5_b_im, s5_c_re, s5_c_im, s5_d, s5_w_glu, s5_b_glu, s5_w_out, final_g):
    xc = jnp.concatenate([ctx[0], x[0]], axis=0)
    c8 = jnp.concatenate([c, c_ctx[None, :], jnp.zeros((6, D_MODEL), F32)], axis=0)
    mods = ada_mods(c8, ada_w, ada_b)
    ia = ib = ic = 0
    for layer in range(DEPTH):
        with_ctx_out = True
        kind = layer % 3
        if kind == 0:
            xc = na_layer(xc, norm_g[layer], mods[layer], na_w_in[ia], na_rpb[ia], na_w_out[ia],
                          with_ctx_out)
            ia += 1
        elif kind == 1:
            xc = mlstm_layer(xc, norm_g[layer], mods[layer], ml_w_in[ib], ml_conv_w[ib],
                             ml_conv_b[ib], ml_wq[ib], ml_wk[ib], ml_wv[ib], ml_w_gate[ib],
                             ml_b_gate[ib], ml_skip[ib], ml_norm_g[ib], ml_w_out[ib])
            ib += 1
        else:
            xc = s5_layer(xc, norm_g[layer], mods[layer], s5_w_in[ic], s5_lam_re[ic],
                          s5_lam_im[ic], s5_log_dt[ic], s5_b_re[ic], s5_b_im[ic], s5_c_re[ic],
                          s5_c_im[ic], s5_d[ic], s5_w_glu[ic], s5_b_glu[ic], s5_w_out[ic])
            ic += 1
    return final_norm(xc, final_g)[None]
```

```python
import functools
import math

import numpy as np
import jax
import jax.numpy as jnp
from jax import lax
from jax.experimental import pallas as pl
from jax.experimental.pallas import tpu as pltpu

F32 = jnp.float32
BF16 = jnp.bfloat16

D_MODEL = 2048
SEQ = 8192
CTX_LEN = 256
TOKENS = CTX_LEN + SEQ
DEPTH = 4
EPS = 1e-6
NEG = -1e30

GRID_W = 64
GRID_ROWS = SEQ // GRID_W
NA_HEADS = 16
NA_HEAD_DIM = 128
WIN_H = 8
WIN_W = 16
NA_QROWS = 4
NA_BLK = NA_QROWS * GRID_W
NA_WIN_BLKS = 3
NA_HEADS_PER_STEP = 4

ML_WIDTH = 2 * D_MODEL
ML_HEADS = 8
ML_HEAD_DIM = ML_WIDTH // ML_HEADS
ML_BLOCK = 4
ML_CHUNK = 256
ML_COLS = 256

S5_WIDTH = D_MODEL
S5_GROUP = 16
S5_GROUPS = S5_WIDTH // S5_GROUP
S5_STATE = 64
S5_L = 16
S5_SEG = 8
S5_CHUNKS = TOKENS // S5_L
S5_CTX_CHUNKS = CTX_LEN // S5_L
S5_X_CHUNKS = SEQ // S5_L
LANES = 128
S5_TILE_GROUPS = LANES // S5_GROUP
S5_TILES = S5_WIDTH // LANES
S5_CAT = S5_L * LANES
S5_TILE_STATE = S5_TILE_GROUPS * 2 * S5_STATE
S5_ROW_SPLIT = 2

VMEM_LIMIT = 56 * 1024 * 1024


def _cparams(*sem):
    return pltpu.CompilerParams(dimension_semantics=sem, vmem_limit_bytes=VMEM_LIMIT)


def _dot(a, b):
    return jnp.dot(a, b, preferred_element_type=F32)


def _dot_nt(a, b):
    return lax.dot_general(a, b, (((1,), (1,)), ((), ())), preferred_element_type=F32)


def _silu(x):
    return x * jax.nn.sigmoid(x)


def _row_select(i, tm, ctx_row, x_row):
    rows = i * tm + lax.broadcasted_iota(jnp.int32, (tm, 1), 0)
    return jnp.where(rows < CTX_LEN, ctx_row, x_row)


def _ada_kernel(c_ref, w_ref, b_ref, o_ref):
    cs = _silu(c_ref[...]).astype(BF16)
    o_ref[0] = _dot(cs, w_ref[0].astype(BF16)) + b_ref[0]


def ada_mods(c8, ada_w, ada_b):
    tn = 512
    n = 3 * D_MODEL
    return pl.pallas_call(
        _ada_kernel,
        out_shape=jax.ShapeDtypeStruct((DEPTH, 8, n), F32),
        grid=(DEPTH, n // tn),
        in_specs=[pl.BlockSpec((8, D_MODEL), lambda l, j: (0, 0)),
                  pl.BlockSpec((1, D_MODEL, tn), lambda l, j: (l, 0, j)),
                  pl.BlockSpec((1, 1, tn), lambda l, j: (l, 0, j))],
        out_specs=pl.BlockSpec((1, 8, tn), lambda l, j: (l, 0, j)),
        compiler_params=_cparams("arbitrary", "arbitrary"),
    )(c8, ada_w, ada_b.reshape(DEPTH, 1, n))


def _norm_mm_kernel(x_ref, g_ref, sh_ref, sc_ref, w_ref, o_ref, h_ref, *, tm):
    i = pl.program_id(0)

    @pl.when(pl.program_id(1) == 0)
    def _():
        x = x_ref[...]
        y = x * lax.rsqrt(jnp.mean(x * x, axis=-1, keepdims=True) + EPS) * g_ref[...]
        sc = _row_select(i, tm, sc_ref[1:2, :], sc_ref[0:1, :])
        sh = _row_select(i, tm, sh_ref[1:2, :], sh_ref[0:1, :])
        h_ref[...] = (y * (1.0 + sc) + sh).astype(BF16)

    o_ref[...] = _dot(h_ref[...], w_ref[...]).astype(o_ref.dtype)


def norm_mm(xc, g, mods, w, col0, ncols, out_dtype, tm=768, tn=1024):
    d = D_MODEL
    cb = col0 // tn
    return pl.pallas_call(
        functools.partial(_norm_mm_kernel, tm=tm),
        out_shape=jax.ShapeDtypeStruct((TOKENS, ncols), out_dtype),
        grid=(TOKENS // tm, ncols // tn),
        in_specs=[pl.BlockSpec((tm, d), lambda i, j: (i, 0)),
                  pl.BlockSpec((1, d), lambda i, j: (0, 0)),
                  pl.BlockSpec((8, d), lambda i, j: (0, 0)),
                  pl.BlockSpec((8, d), lambda i, j: (0, 1)),
                  pl.BlockSpec((d, tn), lambda i, j: (0, cb + j))],
        out_specs=pl.BlockSpec((tm, tn), lambda i, j: (i, j)),
        scratch_shapes=[pltpu.VMEM((tm, d), BF16)],
        compiler_params=_cparams("arbitrary", "arbitrary"),
    )(xc, g.reshape(1, d), mods, mods, w)


def _gated_out_kernel(a_ref, z_ref, w_ref, x_ref, gate_ref, o_ref, *, tm):
    i = pl.program_id(0)
    lhs = (a_ref[...].astype(F32) * _silu(z_ref[...])).astype(BF16)
    gate = _row_select(i, tm, gate_ref[1:2, :], gate_ref[0:1, :])
    o_ref[...] = x_ref[...] + gate * _dot(lhs, w_ref[...])


def gated_out(a, zsrc, zcol0, w, xc, mods, tm=384):
    k = a.shape[1]
    d = D_MODEL
    zb = zcol0 // k
    return pl.pallas_call(
        functools.partial(_gated_out_kernel, tm=tm),
        out_shape=jax.ShapeDtypeStruct((TOKENS, d), F32),
        grid=(TOKENS // tm,),
        in_specs=[pl.BlockSpec((tm, k), lambda i: (i, 0)),
                  pl.BlockSpec((tm, k), lambda i: (i, zb)),
                  pl.BlockSpec((k, d), lambda i: (0, 0)),
                  pl.BlockSpec((tm, d), lambda i: (i, 0)),
                  pl.BlockSpec((8, d), lambda i: (0, 2))],
        out_specs=pl.BlockSpec((tm, d), lambda i: (i, 0)),
        compiler_params=_cparams("arbitrary"),
    )(a, zsrc, w, xc, mods)


NA_REL_ROWS = 2 * WIN_H - 1
NA_A_LO = WIN_H - 1 - WIN_H // 2
NA_A_HI = NA_A_LO + WIN_H - 1
NA_PAIR_BOTH = NA_REL_ROWS - 1
NA_PAIR_LEFT_MASKED = NA_PAIR_BOTH
NA_PAIR_RIGHT_MASKED = NA_PAIR_BOTH + 1
NA_PAIR_MASKED = NA_PAIR_BOTH + 2
NA_PAIR_ENTRIES = NA_PAIR_BOTH + 3
NA_BAND_ROWS = NA_WIN_BLKS * NA_QROWS


def _na_pair_plan():
    nblk = GRID_ROWS // NA_QROWS
    plan = np.full((4, NA_QROWS, NA_BAND_ROWS // 2), NA_PAIR_MASKED, np.int32)
    for kind, g in ((0, 0), (1, 1), (2, nblk - 1)):
        band0 = NA_QROWS * int(np.clip(g - 1, 0, nblk - NA_WIN_BLKS))
        for i in range(NA_QROWS):
            rq = NA_QROWS * g + i
            r0 = int(np.clip(rq - WIN_H // 2, 0, GRID_ROWS - WIN_H))
            rel = [rk - rq + WIN_H - 1 if r0 <= rk < r0 + WIN_H else None
                   for rk in range(band0, band0 + NA_BAND_ROWS)]
            for p in range(NA_BAND_ROWS // 2):
                lo, hi = rel[2 * p], rel[2 * p + 1]
                if lo is not None and hi is not None:
                    plan[kind, i, p] = lo
                elif hi is not None:
                    assert hi == NA_A_LO
                    plan[kind, i, p] = NA_PAIR_LEFT_MASKED
                elif lo is not None:
                    assert lo == NA_A_HI
                    plan[kind, i, p] = NA_PAIR_RIGHT_MASKED
    return plan


def na_pair_table(rpb):
    cq = np.arange(GRID_W)[:, None]
    ck = np.arange(GRID_W)[None, :]
    c0 = np.clip(cq - WIN_W // 2, 0, GRID_W - WIN_W)
    ok = (ck >= c0) & (ck < c0 + WIN_W)
    rel = np.arange(2 * WIN_W - 1)[:, None, None]
    onehot = ((ck - cq + WIN_W - 1)[None] == rel) & ok[None]
    t = jnp.einsum('hab,bqk->haqk', rpb.astype(F32), jnp.asarray(onehot, F32),
                   precision=lax.Precision.HIGHEST)
    t = jnp.where(jnp.asarray(ok), t, NEG)
    masked = jnp.full((NA_HEADS, 1, GRID_W, GRID_W), NEG, F32)
    left = jnp.concatenate([t[:, :-1], masked, t[:, NA_A_HI:NA_A_HI + 1], masked], axis=1)
    right = jnp.concatenate([t[:, 1:], t[:, NA_A_LO:NA_A_LO + 1], masked, masked], axis=1)
    return jnp.concatenate([left, right], axis=-1)


def _na_kernel(q_ref, kc_ref, k0_ref, k1_ref, k2_ref, vc_ref, v0_ref, v1_ref, v2_ref,
               pair_ref, o_ref, bias_ref, *, heads, scale, q0, nq):
    qb = pl.program_id(1) + q0
    plan = _na_pair_plan()
    pw = 2 * GRID_W

    for kind, first in ((3, 0), (0, 1), (1, 2), (2, nq - 1)):
        @pl.when(qb == first)
        def _(kind=kind):
            for h in range(heads):
                for i in range(NA_QROWS):
                    for p in range(NA_BAND_ROWS // 2):
                        bias_ref[h, i * GRID_W:(i + 1) * GRID_W, p * pw:(p + 1) * pw] = (
                            pair_ref[h, int(plan[kind, i, p])])

    for h in range(heads):
        sl = slice(h * NA_HEAD_DIM, (h + 1) * NA_HEAD_DIM)
        q = q_ref[:, sl]
        s = [_dot_nt(q, kc_ref[:, sl]) * scale]
        for b, k_ref in enumerate((k0_ref, k1_ref, k2_ref)):
            s.append(_dot_nt(q, k_ref[:, sl]) * scale
                     + bias_ref[h, :, b * NA_BLK:(b + 1) * NA_BLK])
        m = s[0].max(axis=-1, keepdims=True)
        for t in s[1:]:
            m = jnp.maximum(m, t.max(axis=-1, keepdims=True))
        acc = None
        den = None
        for t, v_ref in zip(s, (vc_ref, v0_ref, v1_ref, v2_ref)):
            p = jnp.exp(t - m)
            l = p.sum(axis=-1, keepdims=True)
            pv = _dot(p.astype(BF16), v_ref[:, sl])
            acc = pv if acc is None else acc + pv
            den = l if den is None else den + l
        o_ref[:, sl] = acc / den


def na_attention(qkv, pair_tab, with_ctx_out):
    hs = NA_HEADS_PER_STEP
    hw = hs * NA_HEAD_DIM
    nh = NA_HEADS // hs
    nq = TOKENS // NA_BLK
    q0 = 0 if with_ctx_out else 1
    nlat = GRID_ROWS // NA_QROWS

    def band(qb):
        return 1 + jnp.clip(qb - 2, 0, nlat - NA_WIN_BLKS)

    def spec(col_base, fn):
        return pl.BlockSpec((NA_BLK, hw), lambda h, g: (fn(g + q0), col_base + h))

    kspecs = [spec(nh, lambda qb: 0 * qb)] + [
        spec(nh, lambda qb, b=b: band(qb) + b) for b in range(NA_WIN_BLKS)]
    vspecs = [spec(2 * nh, lambda qb: 0 * qb)] + [
        spec(2 * nh, lambda qb, b=b: band(qb) + b) for b in range(NA_WIN_BLKS)]
    return pl.pallas_call(
        functools.partial(_na_kernel, heads=hs, scale=NA_HEAD_DIM ** -0.5, q0=q0, nq=nq),
        out_shape=jax.ShapeDtypeStruct((TOKENS, D_MODEL), F32),
        grid=(nh, nq - q0),
        in_specs=[spec(0, lambda qb: qb)] + kspecs + vspecs + [
            pl.BlockSpec((hs, NA_PAIR_ENTRIES, GRID_W, 2 * GRID_W),
                         lambda h, g: (h, 0, 0, 0))],
        out_specs=pl.BlockSpec((NA_BLK, hw), lambda h, g: (g + q0, h)),
        scratch_shapes=[pltpu.VMEM((hs, NA_BLK, NA_WIN_BLKS * NA_BLK), F32)],
        compiler_params=_cparams("arbitrary", "arbitrary"),
    )(*([qkv] * 9), pair_tab)


def _ml_pre_kernel(x_ref, prev_ref, next_ref, cw_ref, cb_ref, wq_ref, wk_ref, wv_ref, wvt_ref,
                   wg_ref, bg_ref, xc_ref, q_ref, k_ref, vt_ref, g_ref, *, tm):
    i = pl.program_id(0)
    j = pl.program_id(1)
    x = x_ref[...]
    rows = i * tm + lax.broadcasted_iota(jnp.int32, (tm, 1), 0)
    local = lax.broadcasted_iota(jnp.int32, (tm, 1), 0)
    x_prev = jnp.where(local == 0, prev_ref[7:8, :], pltpu.roll(x, 1, 0))
    x_prev = jnp.where((rows == 0) | (rows == CTX_LEN), 0.0, x_prev)
    x_next = jnp.where(local == tm - 1, next_ref[0:1, :], pltpu.roll(x, tm - 1, 0))
    x_next = jnp.where((rows == CTX_LEN - 1) | (rows == TOKENS - 1), 0.0, x_next)
    conv = cb_ref[...] + x_prev * cw_ref[0:1, :] + x * cw_ref[1:2, :] + x_next * cw_ref[2:3, :]
    xconv = _silu(conv)
    xc_ref[...] = xconv
    xcb = xconv.astype(BF16)
    xb = x.astype(BF16)
    q = _dot(xcb, wq_ref[0])
    k = _dot(xcb, wk_ref[0])
    v = _dot(xb, wv_ref[0])
    qb, kb, vb = q.astype(BF16), k.astype(BF16), v.astype(BF16)
    q_ref[...] = qb
    k_ref[...] = (k * (ML_HEAD_DIM ** -0.5)).astype(BF16)
    vt_ref[...] = _dot_nt(wvt_ref[0], xb).astype(BF16)
    part = _dot_nt(wg_ref[0], qb) + _dot_nt(wg_ref[1], kb) + _dot_nt(wg_ref[2], vb)

    @pl.when(j == 0)
    def _():
        g_ref[...] = jnp.broadcast_to(bg_ref[...], g_ref.shape)

    g_ref[...] += part


def ml_pre(xz, conv_w, conv_b, wq_bd, wk_bd, wv_bd, wvt_bd, wg_t, b_gate, tm=768):
    e = ML_WIDTH
    c = ML_COLS
    nc = e // c
    last8 = TOKENS // 8 - 1
    tok = lambda i, j: (i, j)
    outs = pl.pallas_call(
        functools.partial(_ml_pre_kernel, tm=tm),
        out_shape=[jax.ShapeDtypeStruct((TOKENS, e), F32),
                   jax.ShapeDtypeStruct((TOKENS, e), BF16),
                   jax.ShapeDtypeStruct((TOKENS, e), BF16),
                   jax.ShapeDtypeStruct((e, TOKENS), BF16),
                   jax.ShapeDtypeStruct((4 * ML_HEADS, TOKENS), F32)],
        grid=(TOKENS // tm, nc),
        in_specs=[pl.BlockSpec((tm, c), tok),
                  pl.BlockSpec((8, c), lambda i, j: (jnp.maximum(i * (tm // 8) - 1, 0), j)),
                  pl.BlockSpec((8, c), lambda i, j: (jnp.minimum((i + 1) * (tm // 8), last8), j)),
                  pl.BlockSpec((3, c), lambda i, j: (0, j)),
                  pl.BlockSpec((1, c), lambda i, j: (0, j)),
                  pl.BlockSpec((1, c, c), lambda i, j: (j, 0, 0)),
                  pl.BlockSpec((1, c, c), lambda i, j: (j, 0, 0)),
                  pl.BlockSpec((1, c, c), lambda i, j: (j, 0, 0)),
                  pl.BlockSpec((1, c, c), lambda i, j: (j, 0, 0)),
                  pl.BlockSpec((3, 4 * ML_HEADS, c), lambda i, j: (0, 0, j)),
                  pl.BlockSpec((4 * ML_HEADS, 1), lambda i, j: (0, 0))],
        out_specs=[pl.BlockSpec((tm, c), tok), pl.BlockSpec((tm, c), tok),
                   pl.BlockSpec((tm, c), tok),
                   pl.BlockSpec((c, tm), lambda i, j: (j, i)),
                   pl.BlockSpec((4 * ML_HEADS, tm), lambda i, j: (0, i))],
        compiler_params=_cparams("arbitrary", "arbitrary"),
    )(xz, xz, xz, conv_w, conv_b.reshape(1, e), wq_bd, wk_bd, wv_bd, wvt_bd, wg_t,
      b_gate.reshape(4 * ML_HEADS, 1))
    return outs


def _split3(x):
    hi = x.astype(BF16).astype(F32)
    r1 = x - hi
    mid = r1.astype(BF16).astype(F32)
    lo = (r1 - mid).astype(BF16).astype(F32)
    return hi, mid, lo


def _ml_step(q, k, vt, i_row, f_row, c_ref, n_ref, m_ref, h_ref, reverse):
    L = ML_CHUNK
    logf = jax.nn.log_sigmoid(f_row)
    hi, mid, lo = _split3(logf)
    rid = lax.broadcasted_iota(jnp.int32, (16, L), 0)
    r16 = jnp.where(rid == 0, hi, jnp.where(rid == 1, mid, jnp.where(rid == 2, lo, 0.0)))
    r16 = r16.astype(BF16)
    si = lax.broadcasted_iota(jnp.int32, (L, L), 0)
    ti = lax.broadcasted_iota(jnp.int32, (L, L), 1)
    tri = (si >= ti) if reverse else (si <= ti)
    rsum = _dot(r16, jnp.where(tri, 1.0, 0.0).astype(BF16))
    b_row = rsum[0:1] + rsum[1:2] + rsum[2:3]
    tri_t = (ti >= si) if reverse else (ti <= si)
    csum = _dot_nt(jnp.where(tri_t, 1.0, 0.0).astype(BF16), r16)
    b_col = csum[:, 0:1] + csum[:, 1:2] + csum[:, 2:3]
    m_prev = m_ref[...]
    a_col = b_col + m_prev
    valid = tri_t
    dmat = jnp.where(valid, b_col - b_row + i_row, NEG)
    m_t = jnp.maximum(a_col, dmat.max(axis=-1, keepdims=True))
    w_inter = jnp.exp(a_col - m_t)
    s = _dot_nt(q, k) * jnp.exp(dmat - m_t)
    num = w_inter * _dot_nt(q, c_ref[...].astype(BF16)) + _dot_nt(s.astype(BF16), vt)
    qn = _dot_nt(q, n_ref[...].astype(BF16))[:, 0:1]
    den = w_inter * qn + s.sum(axis=-1, keepdims=True)
    h_ref[...] = num / jnp.maximum(jnp.abs(den), jnp.exp(-m_t))
    b_end = b_row[:, 0:1] if reverse else b_row[:, L - 1:L]
    g_row = b_end - b_row + i_row
    m_new = jnp.maximum(b_end + m_prev, g_row.max(axis=-1, keepdims=True))
    decay = jnp.exp(b_end + m_prev - m_new)
    w_row = jnp.exp(g_row - m_new)
    c_ref[...] = decay * c_ref[...] + _dot((vt.astype(F32) * w_row).astype(BF16), k)
    w8 = jnp.broadcast_to(w_row, (8, L)).astype(BF16)
    n_ref[...] = decay * n_ref[...] + _dot(w8, k)
    m_ref[...] = m_new


def _ml_rec_kernel(qf_ref, kf_ref, vtf_ref, gf_ref, qb_ref, kb_ref, vtb_ref, gb_ref,
                   hf_ref, hb_ref, cf_ref, nf_ref, mf_ref, cb_ref, nb_ref, mb_ref):
    h = pl.program_id(0)

    @pl.when(pl.program_id(1) == 0)
    def _():
        for r in (cf_ref, nf_ref, mf_ref, cb_ref, nb_ref, mb_ref):
            r[...] = jnp.zeros(r.shape, F32)

    nh = ML_HEADS
    _ml_step(qf_ref[...], kf_ref[...], vtf_ref[...],
             gf_ref[pl.ds(h, 1), :], gf_ref[pl.ds(nh + h, 1), :],
             cf_ref, nf_ref, mf_ref, hf_ref, reverse=False)
    _ml_step(qb_ref[...], kb_ref[...], vtb_ref[...],
             gb_ref[pl.ds(2 * nh + h, 1), :], gb_ref[pl.ds(3 * nh + h, 1), :],
             cb_ref, nb_ref, mb_ref, hb_ref, reverse=True)


def ml_recurrence(q, k, vt, gates_t):
    L = ML_CHUNK
    dh = ML_HEAD_DIM
    nchunk = TOKENS // L
    fwd = lambda j: j
    bwd = lambda j: jnp.where(j == 0, 0, nchunk - j)
    specs = []
    for cm in (fwd, bwd):
        specs += [pl.BlockSpec((L, dh), lambda h, j, cm=cm: (cm(j), h)),
                  pl.BlockSpec((L, dh), lambda h, j, cm=cm: (cm(j), h)),
                  pl.BlockSpec((dh, L), lambda h, j, cm=cm: (h, cm(j))),
                  pl.BlockSpec((4 * ML_HEADS, L), lambda h, j, cm=cm: (0, cm(j)))]
    state = [pltpu.VMEM((dh, dh), F32), pltpu.VMEM((8, dh), F32), pltpu.VMEM((1, 1), F32)]
    return pl.pallas_call(
        _ml_rec_kernel,
        out_shape=[jax.ShapeDtypeStruct((TOKENS, ML_WIDTH), F32)] * 2,
        grid=(ML_HEADS, nchunk),
        in_specs=specs,
        out_specs=[pl.BlockSpec((L, dh), lambda h, j: (fwd(j), h)),
                   pl.BlockSpec((L, dh), lambda h, j: (bwd(j), h))],
        scratch_shapes=state + state,
        compiler_params=_cparams("arbitrary", "arbitrary"),
    )(q, k, vt, gates_t, q, k, vt, gates_t)


def _ml_out_kernel(hf_ref, hb_ref, xc_ref, z_ref, ng_ref, skip_ref, w_ref, x_ref, gate_ref,
                   o_ref, *, tm):
    i = pl.program_id(0)
    kk = pl.program_id(1)
    h = hf_ref[...] + hb_ref[...]
    mu = jnp.mean(h, axis=-1, keepdims=True)
    hc = h - mu
    var = jnp.mean(hc * hc, axis=-1, keepdims=True)
    hn = hc * lax.rsqrt(var + EPS) * ng_ref[...]
    lhs = ((hn + skip_ref[...] * xc_ref[...]) * _silu(z_ref[...])).astype(BF16)
    part = _dot(lhs, w_ref[...])

    @pl.when(kk == 0)
    def _():
        o_ref[...] = part

    @pl.when(kk > 0)
    def _():
        o_ref[...] += part

    @pl.when(kk == pl.num_programs(1) - 1)
    def _():
        gate = _row_select(i, tm, gate_ref[1:2, :], gate_ref[0:1, :])
        o_ref[...] = x_ref[...] + gate * o_ref[...]


def ml_out(hf, hb, xconv, xz, norm_g, skip, w_out, xc, mods, tm=768):
    e = ML_WIDTH
    dh = ML_HEAD_DIM
    d = D_MODEL
    nh = ML_HEADS
    hd = lambda i, k: (i, k)
    return pl.pallas_call(
        functools.partial(_ml_out_kernel, tm=tm),
        out_shape=jax.ShapeDtypeStruct((TOKENS, d), F32),
        grid=(TOKENS // tm, nh),
        in_specs=[pl.BlockSpec((tm, dh), hd), pl.BlockSpec((tm, dh), hd),
                  pl.BlockSpec((tm, dh), hd),
                  pl.BlockSpec((tm, dh), lambda i, k: (i, nh + k)),
                  pl.BlockSpec((1, dh), lambda i, k: (0, k)),
                  pl.BlockSpec((1, dh), lambda i, k: (0, k)),
                  pl.BlockSpec((dh, d), lambda i, k: (k, 0)),
                  pl.BlockSpec((tm, d), lambda i, k: (i, 0)),
                  pl.BlockSpec((8, d), lambda i, k: (0, 2))],
        out_specs=pl.BlockSpec((tm, d), lambda i, k: (i, 0)),
        compiler_params=_cparams("arbitrary", "arbitrary"),
    )(hf, hb, xconv, xz, norm_g.reshape(1, e), skip.reshape(1, e), w_out, xc, mods)


def _cmul(ar, ai, br, bi):
    return ar * br - ai * bi, ar * bi + ai * br


def s5_operators(lam_re, lam_im, log_dt, b_re, b_im, c_re, c_im):
    hp = lax.Precision.HIGHEST
    G, P, GS, L = S5_GROUPS, S5_STATE, S5_GROUP, S5_L
    bre, bim = b_re.astype(F32), b_im.astype(F32)
    cre, cim = c_re.astype(F32), c_im.astype(F32)
    tau = jnp.arange(L + 1, dtype=F32)[:, None, None]
    seg_pows = jnp.asarray([1, S5_CTX_CHUNKS // S5_SEG, S5_X_CHUNKS // S5_SEG], F32) * L
    eye = jnp.eye(S5_TILE_GROUPS, dtype=F32)
    dmaps, bcs, ccts, rows = [], [], [], []
    for j in range(2):
        lr, li = lam_re[j].astype(F32), lam_im[j].astype(F32)
        dt = jnp.exp(log_dt[j].astype(F32))[:, None]
        pr = jnp.exp(lr * dt * tau) * jnp.cos(li * dt * tau)
        pi = jnp.exp(lr * dt * tau) * jnp.sin(li * dt * tau)
        ar, ai = pr[1], pi[1]
        den = lr * lr + li * li
        kr = ((ar - 1.0) * lr + ai * li) / den
        ki = (ai * lr - (ar - 1.0) * li) / den
        bbr = kr[..., None] * bre - ki[..., None] * bim
        bbi = kr[..., None] * bim + ki[..., None] * bre
        car = cre[None] * pr[:L, :, None, :] - cim[None] * pi[:L, :, None, :]
        cai = cre[None] * pi[:L, :, None, :] + cim[None] * pr[:L, :, None, :]
        kk = (jnp.einsum('tgcp,gpd->tgcd', car, bbr, precision=hp)
              - jnp.einsum('tgcp,gpd->tgcd', cai, bbi, precision=hp))
        kk = kk.reshape(L, S5_TILES, S5_TILE_GROUPS, GS, GS)
        dmaps.append(jnp.einsum('lqgcd,gh->qlgdhc', kk, eye).reshape(S5_TILES, L, LANES, LANES))
        pw = np.arange(L - 1, -1, -1) if j == 0 else np.arange(L)
        er, ei = _cmul(pr[pw][..., None], pi[pw][..., None], bbr[None], bbi[None])
        bcs.append(jnp.concatenate([er.transpose(1, 0, 3, 2), ei.transpose(1, 0, 3, 2)], -1)
                   .reshape(G, L * GS, 2 * P))
        pw = np.arange(1, L + 1) if j == 0 else np.arange(L, 0, -1)
        fr = cre[None] * pr[pw][:, :, None, :] - cim[None] * pi[pw][:, :, None, :]
        fi = cre[None] * pi[pw][:, :, None, :] + cim[None] * pr[pw][:, :, None, :]
        ccts.append(jnp.concatenate([fr, -fi], -1).transpose(1, 0, 2, 3).reshape(G, L * GS, 2 * P))
        sp = seg_pows[:, None, None]
        sr = jnp.exp(lr * dt * sp) * jnp.cos(li * dt * sp)
        si = jnp.exp(lr * dt * sp) * jnp.sin(li * dt * sp)
        a1 = jnp.concatenate([sr, sr], -1).reshape(3, 1, G * 2 * P)
        a2 = jnp.concatenate([-si, si], -1).reshape(3, 1, G * 2 * P)
        rows.append(jnp.stack([a1, a2], axis=1))
    df, db = dmaps
    dsum = jnp.concatenate([jnp.flip(db[:, 1:], axis=1), df[:, :1] + db[:, :1], df[:, 1:]], axis=1)
    return (dsum.astype(BF16), jnp.stack(bcs).astype(BF16), jnp.stack(ccts).astype(BF16),
            jnp.stack(rows))


def _s5_chunk_cat(u_ref, dtype):
    n = u_ref.shape[0] // S5_L
    return jnp.concatenate([u_ref[pl.ds(s, n, stride=S5_L), :].astype(dtype)
                            for s in range(S5_L)], axis=-1)


def _s5_place_group_rows(dst_ref, src_ref, col0):
    for gl in range(S5_TILE_GROUPS):
        for s in range(S5_L):
            r = s * LANES + gl * S5_GROUP
            dst_ref[r:r + S5_GROUP, col0 + gl * LANES:col0 + (gl + 1) * LANES] = (
                src_ref[gl, s * S5_GROUP:(s + 1) * S5_GROUP, :])


def _s5_drive_kernel(u_ref, bcf_ref, bcb_ref, ef_ref, eb_ref, op_ref):
    first = (pl.program_id(0) == 0) & (pl.program_id(1) == 0)

    @pl.when(first)
    def _():
        op_ref[...] = jnp.zeros(op_ref.shape, BF16)

    @pl.when(pl.program_id(1) == 0)
    def _():
        _s5_place_group_rows(op_ref, bcf_ref, 0)
        _s5_place_group_rows(op_ref, bcb_ref, S5_TILE_STATE)

    e = _dot(_s5_chunk_cat(u_ref, BF16), op_ref[...])
    ef_ref[...] = e[:, :S5_TILE_STATE]
    eb_ref[...] = e[:, S5_TILE_STATE:]


def s5_drive(uz, bc):
    rows = TOKENS // S5_ROW_SPLIT
    out = jax.ShapeDtypeStruct((S5_CHUNKS, S5_GROUPS * 2 * S5_STATE), F32)
    ospec = pl.BlockSpec((rows // S5_L, S5_TILE_STATE), lambda q, i: (i, q))

    def bspec(d):
        return pl.BlockSpec((None, S5_TILE_GROUPS, S5_L * S5_GROUP, 2 * S5_STATE),
                            lambda q, i: (d, q, 0, 0))
    return pl.pallas_call(
        _s5_drive_kernel,
        out_shape=[out, out],
        grid=(S5_TILES, S5_ROW_SPLIT),
        in_specs=[pl.BlockSpec((rows, LANES), lambda q, i: (i, q)), bspec(0), bspec(1)],
        out_specs=[ospec, ospec],
        scratch_shapes=[pltpu.VMEM((S5_CAT, 2 * S5_TILE_STATE), BF16)],
        compiler_params=_cparams("arbitrary", "arbitrary"),
    )(uz, bc, bc)


def _s5_scan_kernel(*refs, reverse):
    ng = S5_TILE_GROUPS
    e_refs, a_ref, x_refs, in_ref = refs[:ng], refs[ng], refs[ng + 1:2 * ng + 1], refs[-1]
    lanes = [slice(g * LANES, (g + 1) * LANES) for g in range(ng)]

    def advance(a, zs):
        return [a[0][:, sl] * z + a[1][:, sl] * pltpu.roll(z, S5_STATE, 1)
                for sl, z in zip(lanes, zs)]

    a_chunk = (a_ref[0, 0], a_ref[0, 1])

    def part(row0, nj, a_seg, x0):
        def tile(j):
            jj = (nj - 1 - j) if reverse else j
            return pl.ds(row0 + jj, S5_SEG, stride=nj)

        def step(j, zs):
            return tuple(z + e[tile(j), :] for z, e in zip(advance(a_chunk, zs), e_refs))

        zero = tuple(jnp.zeros((S5_SEG, LANES), F32) for _ in range(ng))
        end = lax.fori_loop(0, nj, step, zero)
        order = range(S5_SEG - 1, -1, -1) if reverse else range(S5_SEG)
        cur = x0
        for s in order:
            for sl, c in zip(lanes, cur):
                in_ref[s:s + 1, sl] = c
            cur = [c + e[s:s + 1, :] for c, e in zip(advance(a_seg, cur), end)]

        def emit(j, zs):
            for x, z in zip(x_refs, zs):
                x[tile(j), :] = z
            return step(j, zs)

        lax.fori_loop(0, nj, emit, tuple(in_ref[:, sl] for sl in lanes))
        return cur

    ctx_end = part(0, S5_CTX_CHUNKS // S5_SEG, (a_ref[1, 0], a_ref[1, 1]),
                   [jnp.zeros((1, LANES), F32) for _ in range(ng)])
    part(S5_CTX_CHUNKS, S5_X_CHUNKS // S5_SEG, (a_ref[2, 0], a_ref[2, 1]), ctx_end)


def s5_scan(e, rows, direction):
    ng = S5_TILE_GROUPS
    cols = S5_TILE_STATE
    blks = [pl.BlockSpec((S5_CHUNKS, LANES), lambda i, g=g: (0, ng * i + g)) for g in range(ng)]
    return pl.pallas_call(
        functools.partial(_s5_scan_kernel, reverse=direction == 1),
        out_shape=[jax.ShapeDtypeStruct((S5_CHUNKS, S5_TILES * LANES), F32)] * ng,
        grid=(S5_TILES,),
        in_specs=blks + [pl.BlockSpec((None, 3, 2, 1, cols),
                                      lambda i: (direction, 0, 0, 0, i))],
        out_specs=[pl.BlockSpec((S5_CHUNKS, LANES), lambda i: (0, i))] * ng,
        scratch_shapes=[pltpu.VMEM((S5_SEG, cols), F32)],
        compiler_params=_cparams("arbitrary"),
    )(*([e] * ng), rows)


def _gelu_tanh(x):
    return 0.5 * x * (1.0 + jnp.tanh(math.sqrt(2.0 / math.pi) * (x + 0.044715 * (x * x * x))))


def _s5_read_kernel(u_ref, dsum_ref, cctf_ref, cctb_ref, d_ref, *refs):
    x_refs, (o_ref, m_ref, c_ref) = refs[:2 * S5_TILE_GROUPS], refs[2 * S5_TILE_GROUPS:]
    first = (pl.program_id(0) == 0) & (pl.program_id(1) == 0)

    @pl.when(first)
    def _():
        c_ref[...] = jnp.zeros(c_ref.shape, BF16)

    @pl.when(pl.program_id(1) == 0)
    def _():
        for s in range(S5_L):
            for t in range(S5_L):
                m_ref[s * LANES:(s + 1) * LANES, t * LANES:(t + 1) * LANES] = (
                    dsum_ref[t - s + S5_L - 1])
        _s5_place_group_rows(c_ref, cctf_ref, 0)
        _s5_place_group_rows(c_ref, cctb_ref, S5_TILE_STATE)

    n = u_ref.shape[0] // S5_L
    xs = jnp.concatenate([x[...].astype(BF16) for x in x_refs], axis=-1)
    y = _dot(_s5_chunk_cat(u_ref, BF16), m_ref[...]) + _dot_nt(xs, c_ref[...])
    for t in range(S5_L):
        rows = pl.ds(t, n, stride=S5_L)
        o_ref[rows, :] = _gelu_tanh(y[:, t * LANES:(t + 1) * LANES] + d_ref[...] * u_ref[rows, :])


def s5_read(uz, dsum, cct, xf, xb, d_skip):
    rows = TOKENS // S5_ROW_SPLIT
    ublk = pl.BlockSpec((rows, LANES), lambda q, i: (i, q))
    xblk = pl.BlockSpec((rows // S5_L, LANES), lambda q, i: (i, q))

    def cspec(d):
        return pl.BlockSpec((None, S5_TILE_GROUPS, S5_L * S5_GROUP, 2 * S5_STATE),
                            lambda q, i: (d, q, 0, 0))
    return pl.pallas_call(
        _s5_read_kernel,
        out_shape=jax.ShapeDtypeStruct((TOKENS, S5_WIDTH), F32),
        grid=(S5_TILES, S5_ROW_SPLIT),
        in_specs=[ublk,
                  pl.BlockSpec((None, 2 * S5_L - 1, LANES, LANES), lambda q, i: (q, 0, 0, 0)),
                  cspec(0), cspec(1),
                  pl.BlockSpec((1, LANES), lambda q, i: (0, q))] + [xblk] * (len(xf) + len(xb)),
        out_specs=ublk,
        scratch_shapes=[pltpu.VMEM((S5_CAT, S5_CAT), BF16),
                        pltpu.VMEM((S5_CAT, 2 * S5_TILE_STATE), BF16)],
        compiler_params=_cparams("arbitrary", "arbitrary"),
    )(uz, dsum, cct, cct, d_skip.astype(F32).reshape(1, S5_WIDTH), *xf, *xb)


def _glu_kernel(s_ref, w_ref, b_ref, o_ref, lhs_ref, *, tn):
    j = pl.program_id(1)

    @pl.when(j == 0)
    def _():
        lhs_ref[...] = s_ref[...].astype(BF16)

    v = _dot(lhs_ref[...], w_ref[...]) + b_ref[...]
    sj = s_ref[:, pl.ds(pl.multiple_of(j * tn, tn), tn)]
    o_ref[...] = sj * jax.nn.sigmoid(v)


def s5_glu(s, w_glu, b_glu, tm=768, tn=1024):
    d = S5_WIDTH
    return pl.pallas_call(
        functools.partial(_glu_kernel, tn=tn),
        out_shape=jax.ShapeDtypeStruct((TOKENS, d), F32),
        grid=(TOKENS // tm, d // tn),
        in_specs=[pl.BlockSpec((tm, d), lambda i, j: (i, 0)),
                  pl.BlockSpec((d, tn), lambda i, j: (0, j)),
                  pl.BlockSpec((1, tn), lambda i, j: (0, j))],
        out_specs=pl.BlockSpec((tm, tn), lambda i, j: (i, j)),
        scratch_shapes=[pltpu.VMEM((tm, d), BF16)],
        compiler_params=_cparams("arbitrary", "arbitrary"),
    )(s, w_glu, b_glu.reshape(1, d))


def _final_norm_kernel(x_ref, g_ref, o_ref):
    x = x_ref[...]
    o_ref[...] = x * lax.rsqrt(jnp.mean(x * x, axis=-1, keepdims=True) + EPS) * g_ref[...]


def final_norm(xc, g, tm=256):
    off = CTX_LEN // tm
    return pl.pallas_call(
        _final_norm_kernel,
        out_shape=jax.ShapeDtypeStruct((SEQ, D_MODEL), F32),
        grid=(SEQ // tm,),
        in_specs=[pl.BlockSpec((tm, D_MODEL), lambda i: (i + off, 0)),
                  pl.BlockSpec((1, D_MODEL), lambda i: (0, 0))],
        out_specs=pl.BlockSpec((tm, D_MODEL), lambda i: (i, 0)),
        compiler_params=_cparams("arbitrary"),
    )(xc, g.reshape(1, D_MODEL))


def _block_diag(w, size):
    nb = size // ML_BLOCK
    w = w.reshape(-1, nb, ML_BLOCK, ML_BLOCK)
    eye = jnp.eye(nb, dtype=w.dtype)
    return jnp.einsum('jncd,nm->jncmd', w, eye).reshape(-1, size, size)


def na_layer(xc, g, mods, w_in, rpb, w_out, with_ctx_out):
    e = D_MODEL
    w_in = w_in.astype(BF16)
    qkv = norm_mm(xc, g, mods, w_in, 0, 3 * e, BF16)
    z = norm_mm(xc, g, mods, w_in, 3 * e, e, F32)
    o = na_attention(qkv, na_pair_table(rpb), with_ctx_out)
    return gated_out(o, z, 0, w_out.astype(BF16), xc, mods)


def mlstm_layer(xc, g, mods, w_in, conv_w, conv_b, wq, wk, wv, w_gate, b_gate, skip, norm_g,
                w_out):
    e = ML_WIDTH
    xz = norm_mm(xc, g, mods, w_in.astype(BF16), 0, 2 * e, F32)
    wq_bd = _block_diag(wq, ML_COLS).astype(BF16)
    wk_bd = _block_diag(wk, ML_COLS).astype(BF16)
    wv_bd = _block_diag(wv, ML_COLS).astype(BF16)
    wvt_bd = wv_bd.transpose(0, 2, 1)
    wg_t = w_gate.reshape(3, e, 4 * ML_HEADS).transpose(0, 2, 1).astype(BF16)
    xconv, q, k, vt, gates_t = ml_pre(xz, conv_w, conv_b, wq_bd, wk_bd, wv_bd, wvt_bd, wg_t,
                                      b_gate)
    hf, hb = ml_recurrence(q, k, vt, gates_t)
    return ml_out(hf, hb, xconv, xz, norm_g, skip, w_out.astype(BF16), xc, mods)


def s5_layer(xc, g, mods, w_in, lam_re, lam_im, log_dt, b_re, b_im, c_re, c_im, d_skip,
             w_glu, b_glu, w_out):
    e = S5_WIDTH
    uz = norm_mm(xc, g, mods, w_in.astype(BF16), 0, 2 * e, F32)
    dsum, bc, cct, rows = s5_operators(lam_re, lam_im, log_dt, b_re, b_im, c_re, c_im)
    ef, eb = s5_drive(uz, bc)
    xf = s5_scan(ef, rows, 0)
    xb = s5_scan(eb, rows, 1)
    s = s5_read(uz, dsum, cct, xf, xb, d_skip)
    s2 = s5_glu(s, w_glu.astype(BF16), b_glu)
    return gated_out(s2, uz, e, w_out.astype(BF16), xc, mods)


def kernel(x, c, ctx, c_ctx, norm_g, ada_w, ada_b, na_w_in, na_rpb, na_w_out, ml_w_in, ml_conv_w, ml_conv_b, ml_wq, ml_wk, ml_wv, ml_w_gate, ml_b_gate, ml_skip, ml_norm_g, ml_w_out, s5_w_in, s5_lam_re, s5_lam_im, s5_log_dt, s5_b_re, s5_b_im, s5_c_re, s5_c_im, s5_d, s5_w_glu, s5_b_glu, s5_w_out, final_g):
    xc = jnp.concatenate([ctx[0], x[0]], axis=0)
    c8 = jnp.concatenate([c, c_ctx[None, :], jnp.zeros((6, D_MODEL), F32)], axis=0)
    mods = ada_mods(c8, ada_w, ada_b)
    ia = ib = ic = 0
    for layer in range(DEPTH):
        with_ctx_out = True
        kind = layer % 3
        if kind == 0:
            xc = na_layer(xc, norm_g[layer], mods[layer], na_w_in[ia], na_rpb[ia], na_w_out[ia],
                          with_ctx_out)
            ia += 1
        elif kind == 1:
            xc = mlstm_layer(xc, norm_g[layer], mods[layer], ml_w_in[ib], ml_conv_w[ib],
                             ml_conv_b[ib], ml_wq[ib], ml_wk[ib], ml_wv[ib], ml_w_gate[ib],
                             ml_b_gate[ib], ml_skip[ib], ml_norm_g[ib], ml_w_out[ib])
            ib += 1
        else:
            xc = s5_layer(xc, norm_g[layer], mods[layer], s5_w_in[ic], s5_lam_re[ic],
                          s5_lam_im[ic], s5_log_dt[ic], s5_b_re[ic], s5_b_im[ic], s5_c_re[ic],
                          s5_c_im[ic], s5_d[ic], s5_w_glu[ic], s5_b_glu[ic], s5_w_out[ic])
            ic += 1
    return final_norm(xc, final_g)[None]
```

```python
import functools
import math

import numpy as np
import jax
import jax.numpy as jnp
from jax import lax
from jax.experimental import pallas as pl
from jax.experimental.pallas import tpu as pltpu

F32 = jnp.float32
BF16 = jnp.bfloat16

D_MODEL = 2048
SEQ = 8192
CTX_LEN = 256
TOKENS = CTX_LEN + SEQ
DEPTH = 4
EPS = 1e-6
NEG = -1e30

GRID_W = 64
GRID_ROWS = SEQ // GRID_W
NA_HEADS = 16
NA_HEAD_DIM = 128
WIN_H = 8
WIN_W = 16
NA_QROWS = 4
NA_BLK = NA_QROWS * GRID_W
NA_WIN_BLKS = 3
NA_HEADS_PER_STEP = 8
LOG2E = math.log2(math.e)
NA_Q_SCALE = NA_HEAD_DIM ** -0.5 * LOG2E

ML_WIDTH = 2 * D_MODEL
ML_HEADS = 8
ML_HEAD_DIM = ML_WIDTH // ML_HEADS
ML_BLOCK = 4
ML_CHUNK = 256
ML_COLS = 512

S5_WIDTH = D_MODEL
S5_GROUP = 16
S5_GROUPS = S5_WIDTH // S5_GROUP
S5_STATE = 64
S5_L = 16
S5_SEG = 8
S5_CHUNKS = TOKENS // S5_L
S5_CTX_CHUNKS = CTX_LEN // S5_L
S5_SEG_CHUNKS = S5_CHUNKS // S5_SEG
LANES = 128
S5_TILE_GROUPS = LANES // S5_GROUP
S5_TILES = S5_WIDTH // LANES
S5_CAT = S5_L * LANES
S5_TILE_STATE = S5_TILE_GROUPS * 2 * S5_STATE
S5_ROW_SPLIT = 2

VMEM_LIMIT = 56 * 1024 * 1024


def _cparams(*sem):
    return pltpu.CompilerParams(dimension_semantics=sem, vmem_limit_bytes=VMEM_LIMIT)


def _dot(a, b):
    return jnp.dot(a, b, preferred_element_type=F32)


def _dot_nt(a, b):
    return lax.dot_general(a, b, (((1,), (1,)), ((), ())), preferred_element_type=F32)


def _silu(x):
    return x * jax.nn.sigmoid(x)


def _row_select(i, tm, ctx_row, x_row):
    rows = i * tm + lax.broadcasted_iota(jnp.int32, (tm, 1), 0)
    return jnp.where(rows < CTX_LEN, ctx_row, x_row)


def _ada_kernel(c_ref, w_ref, b_ref, o_ref):
    cs = _silu(c_ref[...]).astype(BF16)
    o_ref[0] = _dot(cs, w_ref[0].astype(BF16)) + b_ref[0]


def ada_mods(c8, ada_w, ada_b):
    tn = 512
    n = 3 * D_MODEL
    return pl.pallas_call(
        _ada_kernel,
        out_shape=jax.ShapeDtypeStruct((DEPTH, 8, n), F32),
        grid=(DEPTH, n // tn),
        in_specs=[pl.BlockSpec((8, D_MODEL), lambda l, j: (0, 0)),
                  pl.BlockSpec((1, D_MODEL, tn), lambda l, j: (l, 0, j)),
                  pl.BlockSpec((1, 1, tn), lambda l, j: (l, 0, j))],
        out_specs=pl.BlockSpec((1, 8, tn), lambda l, j: (l, 0, j)),
        compiler_params=_cparams("arbitrary", "arbitrary"),
    )(c8, ada_w, ada_b.reshape(DEPTH, 1, n))


def _norm_mod_kernel(x_ref, g_ref, sh_ref, sc_ref, o_ref, *, tm):
    i = pl.program_id(0)
    x = x_ref[...]
    y = x * lax.rsqrt(jnp.mean(x * x, axis=-1, keepdims=True) + EPS) * g_ref[...]
    sc = _row_select(i, tm, sc_ref[1:2, :], sc_ref[0:1, :])
    sh = _row_select(i, tm, sh_ref[1:2, :], sh_ref[0:1, :])
    o_ref[...] = (y * (1.0 + sc) + sh).astype(BF16)


def norm_mod(xc, g, mods, tm=384):
    d = D_MODEL
    return pl.pallas_call(
        functools.partial(_norm_mod_kernel, tm=tm),
        out_shape=jax.ShapeDtypeStruct((TOKENS, d), BF16),
        grid=(TOKENS // tm,),
        in_specs=[pl.BlockSpec((tm, d), lambda i: (i, 0)),
                  pl.BlockSpec((1, d), lambda i: (0, 0)),
                  pl.BlockSpec((8, d), lambda i: (0, 0)),
                  pl.BlockSpec((8, d), lambda i: (0, 1))],
        out_specs=pl.BlockSpec((tm, d), lambda i: (i, 0)),
        compiler_params=_cparams("arbitrary"),
    )(xc, g.reshape(1, d), mods, mods)


def _in_proj_kernel(h_ref, w_ref, o_ref, wb_ref, *, scaled_blocks, scale):
    j = pl.program_id(0)

    @pl.when(pl.program_id(1) == 0)
    def _():
        wb_ref[...] = w_ref[...].astype(BF16)

    acc = _dot(h_ref[...], wb_ref[...])
    if scaled_blocks:
        acc = acc * jnp.where(j < scaled_blocks, scale, 1.0)
    o_ref[...] = acc.astype(o_ref.dtype)


def in_proj(h, w, col0, ncols, out_dtype, scaled_cols=0, scale=1.0, tm=768, tn=1024):
    d = D_MODEL
    cb = col0 // tn
    return pl.pallas_call(
        functools.partial(_in_proj_kernel, scaled_blocks=scaled_cols // tn, scale=scale),
        out_shape=jax.ShapeDtypeStruct((TOKENS, ncols), out_dtype),
        grid=(ncols // tn, TOKENS // tm),
        in_specs=[pl.BlockSpec((tm, d), lambda j, i: (i, 0)),
                  pl.BlockSpec((d, tn), lambda j, i: (0, cb + j))],
        out_specs=pl.BlockSpec((tm, tn), lambda j, i: (i, j)),
        scratch_shapes=[pltpu.VMEM((d, tn), BF16)],
        compiler_params=_cparams("arbitrary", "arbitrary"),
    )(h, w)


def _gated_out_kernel(a_ref, z_ref, w_ref, x_ref, gate_ref, o_ref, *, tm):
    i = pl.program_id(0)
    lhs = (a_ref[...].astype(F32) * _silu(z_ref[...])).astype(BF16)
    gate = _row_select(i, tm, gate_ref[1:2, :], gate_ref[0:1, :])
    o_ref[...] = x_ref[...] + gate * _dot(lhs, w_ref[...])


def gated_out(a, zsrc, zcol0, w, xc, mods, tm=384):
    k = a.shape[1]
    d = D_MODEL
    zb = zcol0 // k
    return pl.pallas_call(
        functools.partial(_gated_out_kernel, tm=tm),
        out_shape=jax.ShapeDtypeStruct((TOKENS, d), F32),
        grid=(TOKENS // tm,),
        in_specs=[pl.BlockSpec((tm, k), lambda i: (i, 0)),
                  pl.BlockSpec((tm, k), lambda i: (i, zb)),
                  pl.BlockSpec((k, d), lambda i: (0, 0)),
                  pl.BlockSpec((tm, d), lambda i: (i, 0)),
                  pl.BlockSpec((8, d), lambda i: (0, 2))],
        out_specs=pl.BlockSpec((tm, d), lambda i: (i, 0)),
        compiler_params=_cparams("arbitrary"),
    )(a, zsrc, w, xc, mods)


NA_REL_ROWS = 2 * WIN_H - 1
NA_A_LO = WIN_H - 1 - WIN_H // 2
NA_A_HI = NA_A_LO + WIN_H - 1
NA_PAIR_BOTH = NA_REL_ROWS - 1
NA_PAIR_LEFT_MASKED = NA_PAIR_BOTH
NA_PAIR_RIGHT_MASKED = NA_PAIR_BOTH + 1
NA_PAIR_MASKED = NA_PAIR_BOTH + 2
NA_PAIR_ENTRIES = NA_PAIR_BOTH + 3
NA_BAND_ROWS = NA_WIN_BLKS * NA_QROWS


def _na_pair_plan():
    nblk = GRID_ROWS // NA_QROWS
    plan = np.full((4, NA_QROWS, NA_BAND_ROWS // 2), NA_PAIR_MASKED, np.int32)
    for kind, g in ((0, 0), (1, 1), (2, nblk - 1)):
        band0 = NA_QROWS * int(np.clip(g - 1, 0, nblk - NA_WIN_BLKS))
        for i in range(NA_QROWS):
            rq = NA_QROWS * g + i
            r0 = int(np.clip(rq - WIN_H // 2, 0, GRID_ROWS - WIN_H))
            rel = [rk - rq + WIN_H - 1 if r0 <= rk < r0 + WIN_H else None
                   for rk in range(band0, band0 + NA_BAND_ROWS)]
            for p in range(NA_BAND_ROWS // 2):
                lo, hi = rel[2 * p], rel[2 * p + 1]
                if lo is not None and hi is not None:
                    plan[kind, i, p] = lo
                elif hi is not None:
                    assert hi == NA_A_LO
                    plan[kind, i, p] = NA_PAIR_LEFT_MASKED
                elif lo is not None:
                    assert lo == NA_A_HI
                    plan[kind, i, p] = NA_PAIR_RIGHT_MASKED
    return plan


def na_pair_table(rpb):
    cq = np.arange(GRID_W)[:, None]
    ck = np.arange(GRID_W)[None, :]
    c0 = np.clip(cq - WIN_W // 2, 0, GRID_W - WIN_W)
    ok = (ck >= c0) & (ck < c0 + WIN_W)
    rel = np.arange(2 * WIN_W - 1)[:, None, None]
    onehot = ((ck - cq + WIN_W - 1)[None] == rel) & ok[None]
    t = jnp.einsum('hab,bqk->haqk', rpb.astype(F32), jnp.asarray(onehot, F32),
                   precision=lax.Precision.HIGHEST)
    t = jnp.where(jnp.asarray(ok), t * LOG2E, NEG)
    masked = jnp.full((NA_HEADS, 1, GRID_W, GRID_W), NEG, F32)
    left = jnp.concatenate([t[:, :-1], masked, t[:, NA_A_HI:NA_A_HI + 1], masked], axis=1)
    right = jnp.concatenate([t[:, 1:], t[:, NA_A_LO:NA_A_LO + 1], masked, masked], axis=1)
    return jnp.concatenate([left, right], axis=-1)


def _na_kernel(q_ref, kc_ref, k0_ref, k1_ref, k2_ref, vc_ref, v0_ref, v1_ref, v2_ref,
               pair_ref, o_ref, bias_ref, *, heads, q0, nq):
    qb = pl.program_id(1) + q0
    plan = _na_pair_plan()
    pw = 2 * GRID_W

    for kind, first in ((3, 0), (0, 1), (1, 2), (2, nq - 1)):
        @pl.when(qb == first)
        def _(kind=kind):
            for h in range(heads):
                for i in range(NA_QROWS):
                    for p in range(NA_BAND_ROWS // 2):
                        bias_ref[h, i * GRID_W:(i + 1) * GRID_W, p * pw:(p + 1) * pw] = (
                            pair_ref[h, int(plan[kind, i, p])])

    for h in range(heads):
        sl = slice(h * NA_HEAD_DIM, (h + 1) * NA_HEAD_DIM)
        q = q_ref[:, sl]
        s = [_dot_nt(q, kc_ref[:, sl])]
        for b, k_ref in enumerate((k0_ref, k1_ref, k2_ref)):
            s.append(_dot_nt(q, k_ref[:, sl]) + bias_ref[h, :, b * NA_BLK:(b + 1) * NA_BLK])
        m = s[0].max(axis=-1, keepdims=True)
        for t in s[1:]:
            m = jnp.maximum(m, t.max(axis=-1, keepdims=True))
        acc = None
        den = None
        for t, v_ref in zip(s, (vc_ref, v0_ref, v1_ref, v2_ref)):
            p = jnp.exp2(t - m)
            l = p.sum(axis=-1, keepdims=True)
            pv = _dot(p.astype(BF16), v_ref[:, sl])
            acc = pv if acc is None else acc + pv
            den = l if den is None else den + l
        o_ref[:, sl] = acc / den


def na_attention(qkv, pair_tab, with_ctx_out):
    hs = NA_HEADS_PER_STEP
    hw = hs * NA_HEAD_DIM
    nh = NA_HEADS // hs
    nq = TOKENS // NA_BLK
    q0 = 0 if with_ctx_out else 1
    nlat = GRID_ROWS // NA_QROWS

    def band(qb):
        return 1 + jnp.clip(qb - 2, 0, nlat - NA_WIN_BLKS)

    def spec(col_base, fn):
        return pl.BlockSpec((NA_BLK, hw), lambda h, g: (fn(g + q0), col_base + h))

    kspecs = [spec(nh, lambda qb: 0 * qb)] + [
        spec(nh, lambda qb, b=b: band(qb) + b) for b in range(NA_WIN_BLKS)]
    vspecs = [spec(2 * nh, lambda qb: 0 * qb)] + [
        spec(2 * nh, lambda qb, b=b: band(qb) + b) for b in range(NA_WIN_BLKS)]
    return pl.pallas_call(
        functools.partial(_na_kernel, heads=hs, q0=q0, nq=nq),
        out_shape=jax.ShapeDtypeStruct((TOKENS, D_MODEL), F32),
        grid=(nh, nq - q0),
        in_specs=[spec(0, lambda qb: qb)] + kspecs + vspecs + [
            pl.BlockSpec((hs, NA_PAIR_ENTRIES, GRID_W, 2 * GRID_W),
                         lambda h, g: (h, 0, 0, 0))],
        out_specs=pl.BlockSpec((NA_BLK, hw), lambda h, g: (g + q0, h)),
        scratch_shapes=[pltpu.VMEM((hs, NA_BLK, NA_WIN_BLKS * NA_BLK), F32)],
        compiler_params=_cparams("arbitrary", "arbitrary"),
    )(*([qkv] * 9), pair_tab)


def _ml_pre_kernel(x_ref, prev_ref, next_ref, cw_ref, cb_ref, wq_ref, wk_ref, wv_ref, wvt_ref,
                   wg_ref, bg_ref, xc_ref, q_ref, k_ref, vt_ref, g_ref, *, tm):
    i = pl.program_id(0)
    j = pl.program_id(1)
    x = x_ref[...]
    rows = i * tm + lax.broadcasted_iota(jnp.int32, (tm, 1), 0)
    local = lax.broadcasted_iota(jnp.int32, (tm, 1), 0)
    x_prev = jnp.where(local == 0, prev_ref[7:8, :], pltpu.roll(x, 1, 0))
    x_prev = jnp.where((rows == 0) | (rows == CTX_LEN), 0.0, x_prev)
    x_next = jnp.where(local == tm - 1, next_ref[0:1, :], pltpu.roll(x, tm - 1, 0))
    x_next = jnp.where((rows == CTX_LEN - 1) | (rows == TOKENS - 1), 0.0, x_next)
    conv = cb_ref[...] + x_prev * cw_ref[0:1, :] + x * cw_ref[1:2, :] + x_next * cw_ref[2:3, :]
    xconv = _silu(conv)
    xc_ref[...] = xconv
    xcb = xconv.astype(BF16)
    xb = x.astype(BF16)
    q = _dot(xcb, wq_ref[0])
    k = _dot(xcb, wk_ref[0])
    v = _dot(xb, wv_ref[0])
    qb, kb, vb = q.astype(BF16), k.astype(BF16), v.astype(BF16)
    q_ref[...] = qb
    k_ref[...] = (k * (ML_HEAD_DIM ** -0.5)).astype(BF16)
    vt_ref[...] = _dot_nt(wvt_ref[0], xb).astype(BF16)
    part = _dot_nt(wg_ref[0], qb) + _dot_nt(wg_ref[1], kb) + _dot_nt(wg_ref[2], vb)

    @pl.when(j == 0)
    def _():
        g_ref[...] = jnp.broadcast_to(bg_ref[...], g_ref.shape)

    g_ref[...] += part


def ml_pre(xz, conv_w, conv_b, wq_bd, wk_bd, wv_bd, wvt_bd, wg_t, b_gate, tm=768):
    e = ML_WIDTH
    c = ML_COLS
    nc = e // c
    last8 = TOKENS // 8 - 1
    tok = lambda i, j: (i, j)
    outs = pl.pallas_call(
        functools.partial(_ml_pre_kernel, tm=tm),
        out_shape=[jax.ShapeDtypeStruct((TOKENS, e), F32),
                   jax.ShapeDtypeStruct((TOKENS, e), BF16),
                   jax.ShapeDtypeStruct((TOKENS, e), BF16),
                   jax.ShapeDtypeStruct((e, TOKENS), BF16),
                   jax.ShapeDtypeStruct((4 * ML_HEADS, TOKENS), F32)],
        grid=(TOKENS // tm, nc),
        in_specs=[pl.BlockSpec((tm, c), tok),
                  pl.BlockSpec((8, c), lambda i, j: (jnp.maximum(i * (tm // 8) - 1, 0), j)),
                  pl.BlockSpec((8, c), lambda i, j: (jnp.minimum((i + 1) * (tm // 8), last8), j)),
                  pl.BlockSpec((3, c), lambda i, j: (0, j)),
                  pl.BlockSpec((1, c), lambda i, j: (0, j)),
                  pl.BlockSpec((1, c, c), lambda i, j: (j, 0, 0)),
                  pl.BlockSpec((1, c, c), lambda i, j: (j, 0, 0)),
                  pl.BlockSpec((1, c, c), lambda i, j: (j, 0, 0)),
                  pl.BlockSpec((1, c, c), lambda i, j: (j, 0, 0)),
                  pl.BlockSpec((3, 4 * ML_HEADS, c), lambda i, j: (0, 0, j)),
                  pl.BlockSpec((4 * ML_HEADS, 1), lambda i, j: (0, 0))],
        out_specs=[pl.BlockSpec((tm, c), tok), pl.BlockSpec((tm, c), tok),
                   pl.BlockSpec((tm, c), tok),
                   pl.BlockSpec((c, tm), lambda i, j: (j, i)),
                   pl.BlockSpec((4 * ML_HEADS, tm), lambda i, j: (0, i))],
        compiler_params=_cparams("arbitrary", "arbitrary"),
    )(xz, xz, xz, conv_w, conv_b.reshape(1, e), wq_bd, wk_bd, wv_bd, wvt_bd, wg_t,
      b_gate.reshape(4 * ML_HEADS, 1))
    return outs


def _split3(x):
    hi = x.astype(BF16).astype(F32)
    r1 = x - hi
    mid = r1.astype(BF16).astype(F32)
    lo = (r1 - mid).astype(BF16).astype(F32)
    return hi, mid, lo


def _ml_step(q, k, vt, i_row, f_row, c_ref, n_ref, m_ref, h_ref, reverse):
    L = ML_CHUNK
    logf = jax.nn.log_sigmoid(f_row)
    hi, mid, lo = _split3(logf)
    rid = lax.broadcasted_iota(jnp.int32, (16, L), 0)
    r16 = jnp.where(rid == 0, hi, jnp.where(rid == 1, mid, jnp.where(rid == 2, lo, 0.0)))
    r16 = r16.astype(BF16)
    si = lax.broadcasted_iota(jnp.int32, (L, L), 0)
    ti = lax.broadcasted_iota(jnp.int32, (L, L), 1)
    tri = (si >= ti) if reverse else (si <= ti)
    rsum = _dot(r16, jnp.where(tri, 1.0, 0.0).astype(BF16))
    b_row = rsum[0:1] + rsum[1:2] + rsum[2:3]
    tri_t = (ti >= si) if reverse else (ti <= si)
    csum = _dot_nt(jnp.where(tri_t, 1.0, 0.0).astype(BF16), r16)
    b_col = csum[:, 0:1] + csum[:, 1:2] + csum[:, 2:3]
    m_prev = m_ref[...]
    a_col = b_col + m_prev
    valid = tri_t
    dmat = jnp.where(valid, b_col - b_row + i_row, NEG)
    m_t = jnp.maximum(a_col, dmat.max(axis=-1, keepdims=True))
    w_inter = jnp.exp(a_col - m_t)
    s = _dot_nt(q, k) * jnp.exp(dmat - m_t)
    num = w_inter * _dot_nt(q, c_ref[...].astype(BF16)) + _dot_nt(s.astype(BF16), vt)
    qn = _dot_nt(q, n_ref[...].astype(BF16))[:, 0:1]
    den = w_inter * qn + s.sum(axis=-1, keepdims=True)
    h_ref[...] = num / jnp.maximum(jnp.abs(den), jnp.exp(-m_t))
    b_end = b_row[:, 0:1] if reverse else b_row[:, L - 1:L]
    g_row = b_end - b_row + i_row
    m_new = jnp.maximum(b_end + m_prev, g_row.max(axis=-1, keepdims=True))
    decay = jnp.exp(b_end + m_prev - m_new)
    w_row = jnp.exp(g_row - m_new)
    c_ref[...] = decay * c_ref[...] + _dot((vt.astype(F32) * w_row).astype(BF16), k)
    w8 = jnp.broadcast_to(w_row, (8, L)).astype(BF16)
    n_ref[...] = decay * n_ref[...] + _dot(w8, k)
    m_ref[...] = m_new


def _ml_rec_kernel(qf_ref, kf_ref, vtf_ref, gf_ref, qb_ref, kb_ref, vtb_ref, gb_ref,
                   hf_ref, hb_ref, cf_ref, nf_ref, mf_ref, cb_ref, nb_ref, mb_ref):
    h = pl.program_id(0)

    @pl.when(pl.program_id(1) == 0)
    def _():
        for r in (cf_ref, nf_ref, mf_ref, cb_ref, nb_ref, mb_ref):
            r[...] = jnp.zeros(r.shape, F32)

    nh = ML_HEADS
    _ml_step(qf_ref[...], kf_ref[...], vtf_ref[...],
             gf_ref[pl.ds(h, 1), :], gf_ref[pl.ds(nh + h, 1), :],
             cf_ref, nf_ref, mf_ref, hf_ref, reverse=False)
    _ml_step(qb_ref[...], kb_ref[...], vtb_ref[...],
             gb_ref[pl.ds(2 * nh + h, 1), :], gb_ref[pl.ds(3 * nh + h, 1), :],
             cb_ref, nb_ref, mb_ref, hb_ref, reverse=True)


def ml_recurrence(q, k, vt, gates_t):
    L = ML_CHUNK
    dh = ML_HEAD_DIM
    nchunk = TOKENS // L
    fwd = lambda j: j
    bwd = lambda j: jnp.where(j == 0, 0, nchunk - j)
    specs = []
    for cm in (fwd, bwd):
        specs += [pl.BlockSpec((L, dh), lambda h, j, cm=cm: (cm(j), h)),
                  pl.BlockSpec((L, dh), lambda h, j, cm=cm: (cm(j), h)),
                  pl.BlockSpec((dh, L), lambda h, j, cm=cm: (h, cm(j))),
                  pl.BlockSpec((4 * ML_HEADS, L), lambda h, j, cm=cm: (0, cm(j)))]
    state = [pltpu.VMEM((dh, dh), F32), pltpu.VMEM((8, dh), F32), pltpu.VMEM((1, 1), F32)]
    return pl.pallas_call(
        _ml_rec_kernel,
        out_shape=[jax.ShapeDtypeStruct((TOKENS, ML_WIDTH), F32)] * 2,
        grid=(ML_HEADS, nchunk),
        in_specs=specs,
        out_specs=[pl.BlockSpec((L, dh), lambda h, j: (fwd(j), h)),
                   pl.BlockSpec((L, dh), lambda h, j: (bwd(j), h))],
        scratch_shapes=state + state,
        compiler_params=_cparams("arbitrary", "arbitrary"),
    )(q, k, vt, gates_t, q, k, vt, gates_t)


def _ml_out_kernel(hf_ref, hb_ref, xc_ref, z_ref, ng_ref, skip_ref, w_ref, x_ref, gate_ref,
                   o_ref, *, tm):
    i = pl.program_id(0)
    kk = pl.program_id(1)
    h = hf_ref[...] + hb_ref[...]
    mu = jnp.mean(h, axis=-1, keepdims=True)
    hc = h - mu
    var = jnp.mean(hc * hc, axis=-1, keepdims=True)
    hn = hc * lax.rsqrt(var + EPS) * ng_ref[...]
    lhs = ((hn + skip_ref[...] * xc_ref[...]) * _silu(z_ref[...])).astype(BF16)
    part = _dot(lhs, w_ref[...])

    @pl.when(kk == 0)
    def _():
        o_ref[...] = part

    @pl.when(kk > 0)
    def _():
        o_ref[...] += part

    @pl.when(kk == pl.num_programs(1) - 1)
    def _():
        gate = _row_select(i, tm, gate_ref[1:2, :], gate_ref[0:1, :])
        o_ref[...] = x_ref[...] + gate * o_ref[...]


def ml_out(hf, hb, xconv, xz, norm_g, skip, w_out, xc, mods, tm=768):
    e = ML_WIDTH
    dh = ML_HEAD_DIM
    d = D_MODEL
    nh = ML_HEADS
    hd = lambda i, k: (i, k)
    return pl.pallas_call(
        functools.partial(_ml_out_kernel, tm=tm),
        out_shape=jax.ShapeDtypeStruct((TOKENS, d), F32),
        grid=(TOKENS // tm, nh),
        in_specs=[pl.BlockSpec((tm, dh), hd), pl.BlockSpec((tm, dh), hd),
                  pl.BlockSpec((tm, dh), hd),
                  pl.BlockSpec((tm, dh), lambda i, k: (i, nh + k)),
                  pl.BlockSpec((1, dh), lambda i, k: (0, k)),
                  pl.BlockSpec((1, dh), lambda i, k: (0, k)),
                  pl.BlockSpec((dh, d), lambda i, k: (k, 0)),
                  pl.BlockSpec((tm, d), lambda i, k: (i, 0)),
                  pl.BlockSpec((8, d), lambda i, k: (0, 2))],
        out_specs=pl.BlockSpec((tm, d), lambda i, k: (i, 0)),
        compiler_params=_cparams("arbitrary", "arbitrary"),
    )(hf, hb, xconv, xz, norm_g.reshape(1, e), skip.reshape(1, e), w_out, xc, mods)


def _cmul(ar, ai, br, bi):
    return ar * br - ai * bi, ar * bi + ai * br


def s5_operators(lam_re, lam_im, log_dt, b_re, b_im, c_re, c_im):
    hp = lax.Precision.HIGHEST
    G, P, GS, L = S5_GROUPS, S5_STATE, S5_GROUP, S5_L
    bre, bim = b_re.astype(F32), b_im.astype(F32)
    cre, cim = c_re.astype(F32), c_im.astype(F32)
    tau = jnp.arange(L + 1, dtype=F32)[:, None, None]
    seg_pows = jnp.asarray([1, S5_SEG_CHUNKS], F32) * L
    eye = jnp.eye(S5_TILE_GROUPS, dtype=F32)
    dmaps, bcs, ccts, rows = [], [], [], []
    for j in range(2):
        lr, li = lam_re[j].astype(F32), lam_im[j].astype(F32)
        dt = jnp.exp(log_dt[j].astype(F32))[:, None]
        pr = jnp.exp(lr * dt * tau) * jnp.cos(li * dt * tau)
        pi = jnp.exp(lr * dt * tau) * jnp.sin(li * dt * tau)
        ar, ai = pr[1], pi[1]
        den = lr * lr + li * li
        kr = ((ar - 1.0) * lr + ai * li) / den
        ki = (ai * lr - (ar - 1.0) * li) / den
        bbr = kr[..., None] * bre - ki[..., None] * bim
        bbi = kr[..., None] * bim + ki[..., None] * bre
        car = cre[None] * pr[:L, :, None, :] - cim[None] * pi[:L, :, None, :]
        cai = cre[None] * pi[:L, :, None, :] + cim[None] * pr[:L, :, None, :]
        kk = (jnp.einsum('tgcp,gpd->tgcd', car, bbr, precision=hp)
              - jnp.einsum('tgcp,gpd->tgcd', cai, bbi, precision=hp))
        kk = kk.reshape(L, S5_TILES, S5_TILE_GROUPS, GS, GS)
        dmaps.append(jnp.einsum('lqgcd,gh->qlgdhc', kk, eye).reshape(S5_TILES, L, LANES, LANES))
        pw = np.arange(L - 1, -1, -1) if j == 0 else np.arange(L)
        er, ei = _cmul(pr[pw][..., None], pi[pw][..., None], bbr[None], bbi[None])
        bcs.append(jnp.concatenate([er.transpose(1, 0, 3, 2), ei.transpose(1, 0, 3, 2)], -1)
                   .reshape(G, L * GS, 2 * P))
        pw = np.arange(1, L + 1) if j == 0 else np.arange(L, 0, -1)
        fr = cre[None] * pr[pw][:, :, None, :] - cim[None] * pi[pw][:, :, None, :]
        fi = cre[None] * pi[pw][:, :, None, :] + cim[None] * pr[pw][:, :, None, :]
        ccts.append(jnp.concatenate([fr, -fi], -1).transpose(1, 0, 2, 3).reshape(G, L * GS, 2 * P))
        sp = seg_pows[:, None, None]
        sr = jnp.exp(lr * dt * sp) * jnp.cos(li * dt * sp)
        si = jnp.exp(lr * dt * sp) * jnp.sin(li * dt * sp)
        a1 = jnp.concatenate([sr, sr], -1).reshape(2, 1, G * 2 * P)
        a2 = jnp.concatenate([-si, si], -1).reshape(2, 1, G * 2 * P)
        rows.append(jnp.stack([a1, a2], axis=1))
    df, db = dmaps
    dsum = jnp.concatenate([jnp.flip(db[:, 1:], axis=1), df[:, :1] + db[:, :1], df[:, 1:]], axis=1)
    return (dsum.astype(BF16), jnp.stack(bcs).astype(BF16), jnp.stack(ccts).astype(BF16),
            jnp.stack(rows))


def _s5_chunk_cat(u_ref, dtype):
    n = u_ref.shape[0] // S5_L
    return jnp.concatenate([u_ref[pl.ds(s, n, stride=S5_L), :].astype(dtype)
                            for s in range(S5_L)], axis=-1)


def _s5_place_group_rows(dst_ref, src_ref, col0):
    for gl in range(S5_TILE_GROUPS):
        for s in range(S5_L):
            r = s * LANES + gl * S5_GROUP
            dst_ref[r:r + S5_GROUP, col0 + gl * LANES:col0 + (gl + 1) * LANES] = (
                src_ref[gl, s * S5_GROUP:(s + 1) * S5_GROUP, :])


def _s5_drive_kernel(u_ref, bcf_ref, bcb_ref, ef_ref, eb_ref, op_ref):
    first = (pl.program_id(0) == 0) & (pl.program_id(1) == 0)

    @pl.when(first)
    def _():
        op_ref[...] = jnp.zeros(op_ref.shape, BF16)

    @pl.when(pl.program_id(1) == 0)
    def _():
        _s5_place_group_rows(op_ref, bcf_ref, 0)
        _s5_place_group_rows(op_ref, bcb_ref, S5_TILE_STATE)

    e = _dot(_s5_chunk_cat(u_ref, BF16), op_ref[...])
    ef_ref[...] = e[:, :S5_TILE_STATE]
    eb_ref[...] = e[:, S5_TILE_STATE:]


def s5_drive(uz, bc):
    rows = TOKENS // S5_ROW_SPLIT
    out = jax.ShapeDtypeStruct((S5_CHUNKS, S5_GROUPS * 2 * S5_STATE), F32)
    ospec = pl.BlockSpec((rows // S5_L, S5_TILE_STATE), lambda q, i: (i, q))

    def bspec(d):
        return pl.BlockSpec((None, S5_TILE_GROUPS, S5_L * S5_GROUP, 2 * S5_STATE),
                            lambda q, i: (d, q, 0, 0))
    return pl.pallas_call(
        _s5_drive_kernel,
        out_shape=[out, out],
        grid=(S5_TILES, S5_ROW_SPLIT),
        in_specs=[pl.BlockSpec((rows, LANES), lambda q, i: (i, q)), bspec(0), bspec(1)],
        out_specs=[ospec, ospec],
        scratch_shapes=[pltpu.VMEM((S5_CAT, 2 * S5_TILE_STATE), BF16)],
        compiler_params=_cparams("arbitrary", "arbitrary"),
    )(uz, bc, bc)


def _s5_scan_kernel(*refs, reverse):
    ng = S5_TILE_GROUPS
    nj = S5_SEG_CHUNKS
    e_refs, a_ref, x_refs, in_ref = refs[:ng], refs[ng], refs[ng + 1:2 * ng + 1], refs[-1]
    lanes = [slice(g * LANES, (g + 1) * LANES) for g in range(ng)]
    swap = lambda z: pltpu.roll(z, S5_STATE, 1)
    a1 = [a_ref[0, 0][:, sl] for sl in lanes]
    a2 = [a_ref[0, 1][:, sl] for sl in lanes]
    seg0 = lax.broadcasted_iota(jnp.int32, (S5_SEG, LANES), 0) == 0

    def tile(j):
        jj = (nj - 1 - j) if reverse else j
        return jj, pl.ds(jj, S5_SEG, stride=nj)

    def step(j, carry, emit):
        jj, rows = tile(j)
        out = []
        for g in range(ng):
            z, zs = carry[2 * g], carry[2 * g + 1]
            if reverse:
                reset = seg0 & (jj == S5_CTX_CHUNKS - 1)
                z, zs = jnp.where(reset, 0.0, z), jnp.where(reset, 0.0, zs)
            if emit:
                x_refs[g][rows, :] = z
            e = e_refs[g][rows, :]
            out += [a1[g] * z + a2[g] * zs + e, a1[g] * zs - a2[g] * z + swap(e)]
        return tuple(out)

    zero = tuple(jnp.zeros((S5_SEG, LANES), F32) for _ in range(2 * ng))
    end = lax.fori_loop(0, nj, functools.partial(step, emit=False), zero)[0::2]
    order = range(S5_SEG - 1, -1, -1) if reverse else range(S5_SEG)
    for g, sl in enumerate(lanes):
        cur = end[g][0:1, :] if reverse else jnp.zeros((1, LANES), F32)
        for s in order:
            in_ref[s:s + 1, sl] = cur
            cur = a_ref[1, 0][:, sl] * cur + a_ref[1, 1][:, sl] * swap(cur) + end[g][s:s + 1, :]
    start = []
    for sl in lanes:
        start += [in_ref[:, sl], swap(in_ref[:, sl])]
    lax.fori_loop(0, nj, functools.partial(step, emit=True), tuple(start))


def s5_scan(e, rows, direction):
    ng = S5_TILE_GROUPS
    cols = S5_TILE_STATE
    blks = [pl.BlockSpec((S5_CHUNKS, LANES), lambda i, g=g: (0, ng * i + g)) for g in range(ng)]
    return pl.pallas_call(
        functools.partial(_s5_scan_kernel, reverse=direction == 1),
        out_shape=[jax.ShapeDtypeStruct((S5_CHUNKS, S5_TILES * LANES), F32)] * ng,
        grid=(S5_TILES,),
        in_specs=blks + [pl.BlockSpec((None, 2, 2, 1, cols),
                                      lambda i: (direction, 0, 0, 0, i))],
        out_specs=[pl.BlockSpec((S5_CHUNKS, LANES), lambda i: (0, i))] * ng,
        scratch_shapes=[pltpu.VMEM((S5_SEG, cols), F32)],
        compiler_params=_cparams("arbitrary"),
    )(*([e] * ng), rows)


def _gelu_tanh(x):
    return 0.5 * x * (1.0 + jnp.tanh(math.sqrt(2.0 / math.pi) * (x + 0.044715 * (x * x * x))))


def _s5_read_kernel(u_ref, dsum_ref, cctf_ref, cctb_ref, d_ref, *refs):
    x_refs, (o_ref, m_ref, c_ref) = refs[:2 * S5_TILE_GROUPS], refs[2 * S5_TILE_GROUPS:]
    first = (pl.program_id(0) == 0) & (pl.program_id(1) == 0)

    @pl.when(first)
    def _():
        c_ref[...] = jnp.zeros(c_ref.shape, BF16)

    @pl.when(pl.program_id(1) == 0)
    def _():
        for s in range(S5_L):
            for t in range(S5_L):
                m_ref[s * LANES:(s + 1) * LANES, t * LANES:(t + 1) * LANES] = (
                    dsum_ref[t - s + S5_L - 1])
        _s5_place_group_rows(c_ref, cctf_ref, 0)
        _s5_place_group_rows(c_ref, cctb_ref, S5_TILE_STATE)

    n = u_ref.shape[0] // S5_L
    xs = jnp.concatenate([x[...].astype(BF16) for x in x_refs], axis=-1)
    y = _dot(_s5_chunk_cat(u_ref, BF16), m_ref[...]) + _dot_nt(xs, c_ref[...])
    for t in range(S5_L):
        rows = pl.ds(t, n, stride=S5_L)
        o_ref[rows, :] = _gelu_tanh(y[:, t * LANES:(t + 1) * LANES] + d_ref[...] * u_ref[rows, :])


def s5_read(uz, dsum, cct, xf, xb, d_skip):
    rows = TOKENS // S5_ROW_SPLIT
    ublk = pl.BlockSpec((rows, LANES), lambda q, i: (i, q))
    xblk = pl.BlockSpec((rows // S5_L, LANES), lambda q, i: (i, q))

    def cspec(d):
        return pl.BlockSpec((None, S5_TILE_GROUPS, S5_L * S5_GROUP, 2 * S5_STATE),
                            lambda q, i: (d, q, 0, 0))
    return pl.pallas_call(
        _s5_read_kernel,
        out_shape=jax.ShapeDtypeStruct((TOKENS, S5_WIDTH), F32),
        grid=(S5_TILES, S5_ROW_SPLIT),
        in_specs=[ublk,
                  pl.BlockSpec((None, 2 * S5_L - 1, LANES, LANES), lambda q, i: (q, 0, 0, 0)),
                  cspec(0), cspec(1),
                  pl.BlockSpec((1, LANES), lambda q, i: (0, q))] + [xblk] * (len(xf) + len(xb)),
        out_specs=ublk,
        scratch_shapes=[pltpu.VMEM((S5_CAT, S5_CAT), BF16),
                        pltpu.VMEM((S5_CAT, 2 * S5_TILE_STATE), BF16)],
        compiler_params=_cparams("arbitrary", "arbitrary"),
    )(uz, dsum, cct, cct, d_skip.astype(F32).reshape(1, S5_WIDTH), *xf, *xb)


def _glu_kernel(s_ref, w_ref, b_ref, o_ref, lhs_ref, *, tn):
    j = pl.program_id(1)

    @pl.when(j == 0)
    def _():
        lhs_ref[...] = s_ref[...].astype(BF16)

    v = _dot(lhs_ref[...], w_ref[...]) + b_ref[...]
    sj = s_ref[:, pl.ds(pl.multiple_of(j * tn, tn), tn)]
    o_ref[...] = sj * jax.nn.sigmoid(v)


def s5_glu(s, w_glu, b_glu, tm=768, tn=1024):
    d = S5_WIDTH
    return pl.pallas_call(
        functools.partial(_glu_kernel, tn=tn),
        out_shape=jax.ShapeDtypeStruct((TOKENS, d), F32),
        grid=(TOKENS // tm, d // tn),
        in_specs=[pl.BlockSpec((tm, d), lambda i, j: (i, 0)),
                  pl.BlockSpec((d, tn), lambda i, j: (0, j)),
                  pl.BlockSpec((1, tn), lambda i, j: (0, j))],
        out_specs=pl.BlockSpec((tm, tn), lambda i, j: (i, j)),
        scratch_shapes=[pltpu.VMEM((tm, d), BF16)],
        compiler_params=_cparams("arbitrary", "arbitrary"),
    )(s, w_glu, b_glu.reshape(1, d))


def _final_norm_kernel(x_ref, g_ref, o_ref):
    x = x_ref[...]
    o_ref[...] = x * lax.rsqrt(jnp.mean(x * x, axis=-1, keepdims=True) + EPS) * g_ref[...]


def final_norm(xc, g, tm=256):
    off = CTX_LEN // tm
    return pl.pallas_call(
        _final_norm_kernel,
        out_shape=jax.ShapeDtypeStruct((SEQ, D_MODEL), F32),
        grid=(SEQ // tm,),
        in_specs=[pl.BlockSpec((tm, D_MODEL), lambda i: (i + off, 0)),
                  pl.BlockSpec((1, D_MODEL), lambda i: (0, 0))],
        out_specs=pl.BlockSpec((tm, D_MODEL), lambda i: (i, 0)),
        compiler_params=_cparams("arbitrary"),
    )(xc, g.reshape(1, D_MODEL))


def _block_diag(w, size):
    nb = size // ML_BLOCK
    w = w.reshape(-1, nb, ML_BLOCK, ML_BLOCK)
    eye = jnp.eye(nb, dtype=w.dtype)
    return jnp.einsum('jncd,nm->jncmd', w, eye).reshape(-1, size, size)


def na_layer(xc, g, mods, w_in, rpb, w_out, with_ctx_out):
    e = D_MODEL
    h = norm_mod(xc, g, mods)
    qkv = in_proj(h, w_in, 0, 3 * e, BF16, scaled_cols=e, scale=NA_Q_SCALE)
    z = in_proj(h, w_in, 3 * e, e, F32)
    o = na_attention(qkv, na_pair_table(rpb), with_ctx_out)
    return gated_out(o, z, 0, w_out.astype(BF16), xc, mods)


def mlstm_layer(xc, g, mods, w_in, conv_w, conv_b, wq, wk, wv, w_gate, b_gate, skip, norm_g,
                w_out):
    e = ML_WIDTH
    xz = in_proj(norm_mod(xc, g, mods), w_in, 0, 2 * e, F32)
    wq_bd = _block_diag(wq, ML_COLS).astype(BF16)
    wk_bd = _block_diag(wk, ML_COLS).astype(BF16)
    wv_bd = _block_diag(wv, ML_COLS).astype(BF16)
    wvt_bd = wv_bd.transpose(0, 2, 1)
    wg_t = w_gate.reshape(3, e, 4 * ML_HEADS).transpose(0, 2, 1).astype(BF16)
    xconv, q, k, vt, gates_t = ml_pre(xz, conv_w, conv_b, wq_bd, wk_bd, wv_bd, wvt_bd, wg_t,
                                      b_gate)
    hf, hb = ml_recurrence(q, k, vt, gates_t)
    return ml_out(hf, hb, xconv, xz, norm_g, skip, w_out.astype(BF16), xc, mods)


def s5_layer(xc, g, mods, w_in, lam_re, lam_im, log_dt, b_re, b_im, c_re, c_im, d_skip,
             w_glu, b_glu, w_out):
    e = S5_WIDTH
    uz = in_proj(norm_mod(xc, g, mods), w_in, 0, 2 * e, F32)
    dsum, bc, cct, rows = s5_operators(lam_re, lam_im, log_dt, b_re, b_im, c_re, c_im)
    ef, eb = s5_drive(uz, bc)
    xf = s5_scan(ef, rows, 0)
    xb = s5_scan(eb, rows, 1)
    s = s5_read(uz, dsum, cct, xf, xb, d_skip)
    s2 = s5_glu(s, w_glu.astype(BF16), b_glu)
    return gated_out(s2, uz, e, w_out.astype(BF16), xc, mods)


def kernel(x, c, ctx, c_ctx, norm_g, ada_w, ada_b, na_w_in, na_rpb, na_w_out, ml_w_in, ml_conv_w, ml_conv_b, ml_wq, ml_wk, ml_wv, ml_w_gate, ml_b_gate, ml_skip, ml_norm_g, ml_w_out, s5_w_in, s5_lam_re, s5_lam_im, s5_log_dt, s5_b_re, s5_b_im, s5_c_re, s5_c_im, s5_d, s5_w_glu, s5_b_glu, s5_w_out, final_g):
    xc = jnp.concatenate([ctx[0], x[0]], axis=0)
    c8 = jnp.concatenate([c, c_ctx[None, :], jnp.zeros((6, D_MODEL), F32)], axis=0)
    mods = ada_mods(c8, ada_w, ada_b)
    ia = ib = ic = 0
    for layer in range(DEPTH):
        with_ctx_out = True
        kind = layer % 3
        if kind == 0:
            xc = na_layer(xc, norm_g[layer], mods[layer], na_w_in[ia], na_rpb[ia], na_w_out[ia],
                          with_ctx_out)
            ia += 1
        elif kind == 1:
            xc = mlstm_layer(xc, norm_g[layer], mods[layer], ml_w_in[ib], ml_conv_w[ib],
                             ml_conv_b[ib], ml_wq[ib], ml_wk[ib], ml_wv[ib], ml_w_gate[ib],
                             ml_b_gate[ib], ml_skip[ib], ml_norm_g[ib], ml_w_out[ib])
            ib += 1
        else:
            xc = s5_layer(xc, norm_g[layer], mods[layer], s5_w_in[ic], s5_lam_re[ic],
                          s5_lam_im[ic], s5_log_dt[ic], s5_b_re[ic], s5_b_im[ic], s5_c_re[ic],
                          s5_c_im[ic], s5_d[ic], s5_w_glu[ic], s5_b_glu[ic], s5_w_out[ic])
            ic += 1
    return final_norm(xc, final_g)[None]
```

```python
import functools
import itertools
import math

import numpy as np
import jax
import jax.numpy as jnp
from jax import lax
from jax.experimental import pallas as pl
from jax.experimental.pallas import tpu as pltpu

F32 = jnp.float32
BF16 = jnp.bfloat16

D_MODEL = 2048
SEQ = 8192
CTX_LEN = 256
TOKENS = CTX_LEN + SEQ
DEPTH = 4
EPS = 1e-6
NEG = -1e30

GRID_W = 64
GRID_ROWS = SEQ // GRID_W
NA_HEADS = 16
NA_HEAD_DIM = 128
WIN_H = 8
WIN_W = 16
NA_QROWS = 4
NA_BLK = NA_QROWS * GRID_W
NA_WIN_BLKS = 3
NA_HEADS_PER_STEP = 8
LOG2E = math.log2(math.e)
NA_Q_SCALE = NA_HEAD_DIM ** -0.5 * LOG2E

ML_WIDTH = 2 * D_MODEL
ML_HEADS = 8
ML_HEAD_DIM = ML_WIDTH // ML_HEADS
ML_BLOCK = 4
ML_CHUNK = 256
ML_REC_HEADS = 2
ML_COLS = 512

S5_WIDTH = D_MODEL
S5_GROUP = 16
S5_GROUPS = S5_WIDTH // S5_GROUP
S5_STATE = 64
S5_L = 16
S5_SEG = 8
S5_CHUNKS = TOKENS // S5_L
S5_CTX_CHUNKS = CTX_LEN // S5_L
S5_SEG_CHUNKS = S5_CHUNKS // S5_SEG
LANES = 128
S5_TILE_GROUPS = LANES // S5_GROUP
S5_TILES = S5_WIDTH // LANES
S5_CAT = S5_L * LANES
S5_TILE_STATE = S5_TILE_GROUPS * 2 * S5_STATE
S5_ROW_SPLIT = 2

VMEM_LIMIT = 56 * 1024 * 1024


def _cparams(*sem):
    return pltpu.CompilerParams(dimension_semantics=sem, vmem_limit_bytes=VMEM_LIMIT)


def _dot(a, b):
    return jnp.dot(a, b, preferred_element_type=F32)


def _dot_nt(a, b):
    return lax.dot_general(a, b, (((1,), (1,)), ((), ())), preferred_element_type=F32)


def _silu(x):
    return x * jax.nn.sigmoid(x)


def _row_select(i, tm, ctx_row, x_row):
    rows = i * tm + lax.broadcasted_iota(jnp.int32, (tm, 1), 0)
    return jnp.where(rows < CTX_LEN, ctx_row, x_row)


def _ada_kernel(c_ref, w_ref, b_ref, o_ref):
    cs = _silu(c_ref[...]).astype(BF16)
    o_ref[0] = _dot(cs, w_ref[0].astype(BF16)) + b_ref[0]


def ada_mods(c8, ada_w, ada_b):
    tn = 512
    n = 3 * D_MODEL
    return pl.pallas_call(
        _ada_kernel,
        out_shape=jax.ShapeDtypeStruct((DEPTH, 8, n), F32),
        grid=(DEPTH, n // tn),
        in_specs=[pl.BlockSpec((8, D_MODEL), lambda l, j: (0, 0)),
                  pl.BlockSpec((1, D_MODEL, tn), lambda l, j: (l, 0, j)),
                  pl.BlockSpec((1, 1, tn), lambda l, j: (l, 0, j))],
        out_specs=pl.BlockSpec((1, 8, tn), lambda l, j: (l, 0, j)),
        compiler_params=_cparams("arbitrary", "arbitrary"),
    )(c8, ada_w, ada_b.reshape(DEPTH, 1, n))


def _norm_mod_kernel(x_ref, g_ref, sh_ref, sc_ref, o_ref, *, tm):
    i = pl.program_id(0)
    x = x_ref[...]
    y = x * lax.rsqrt(jnp.mean(x * x, axis=-1, keepdims=True) + EPS) * g_ref[...]
    sc = _row_select(i, tm, sc_ref[1:2, :], sc_ref[0:1, :])
    sh = _row_select(i, tm, sh_ref[1:2, :], sh_ref[0:1, :])
    o_ref[...] = (y * (1.0 + sc) + sh).astype(BF16)


def norm_mod(xc, g, mods, tm=384):
    d = D_MODEL
    return pl.pallas_call(
        functools.partial(_norm_mod_kernel, tm=tm),
        out_shape=jax.ShapeDtypeStruct((TOKENS, d), BF16),
        grid=(TOKENS // tm,),
        in_specs=[pl.BlockSpec((tm, d), lambda i: (i, 0)),
                  pl.BlockSpec((1, d), lambda i: (0, 0)),
                  pl.BlockSpec((8, d), lambda i: (0, 0)),
                  pl.BlockSpec((8, d), lambda i: (0, 1))],
        out_specs=pl.BlockSpec((tm, d), lambda i: (i, 0)),
        compiler_params=_cparams("arbitrary"),
    )(xc, g.reshape(1, d), mods, mods)


def _in_proj_kernel(h_ref, w_ref, o_ref, wb_ref, *, scaled_blocks, scale):
    j = pl.program_id(0)

    @pl.when(pl.program_id(1) == 0)
    def _():
        wb_ref[...] = w_ref[...].astype(BF16)

    acc = _dot(h_ref[...], wb_ref[...])
    if scaled_blocks:
        acc = acc * jnp.where(j < scaled_blocks, scale, 1.0)
    o_ref[...] = acc.astype(o_ref.dtype)


def in_proj(h, w, col0, ncols, out_dtype, scaled_cols=0, scale=1.0, tm=768, tn=1024):
    d = D_MODEL
    cb = col0 // tn
    return pl.pallas_call(
        functools.partial(_in_proj_kernel, scaled_blocks=scaled_cols // tn, scale=scale),
        out_shape=jax.ShapeDtypeStruct((TOKENS, ncols), out_dtype),
        grid=(ncols // tn, TOKENS // tm),
        in_specs=[pl.BlockSpec((tm, d), lambda j, i: (i, 0)),
                  pl.BlockSpec((d, tn), lambda j, i: (0, cb + j))],
        out_specs=pl.BlockSpec((tm, tn), lambda j, i: (i, j)),
        scratch_shapes=[pltpu.VMEM((d, tn), BF16)],
        compiler_params=_cparams("arbitrary", "arbitrary"),
    )(h, w)


def _gated_out_kernel(a_ref, z_ref, w_ref, x_ref, gate_ref, o_ref, *, tm):
    i = pl.program_id(0)
    lhs = (a_ref[...].astype(F32) * _silu(z_ref[...])).astype(BF16)
    gate = _row_select(i, tm, gate_ref[1:2, :], gate_ref[0:1, :])
    o_ref[...] = x_ref[...] + gate * _dot(lhs, w_ref[...])


def gated_out(a, zsrc, zcol0, w, xc, mods, tm=384):
    k = a.shape[1]
    d = D_MODEL
    zb = zcol0 // k
    return pl.pallas_call(
        functools.partial(_gated_out_kernel, tm=tm),
        out_shape=jax.ShapeDtypeStruct((TOKENS, d), F32),
        grid=(TOKENS // tm,),
        in_specs=[pl.BlockSpec((tm, k), lambda i: (i, 0)),
                  pl.BlockSpec((tm, k), lambda i: (i, zb)),
                  pl.BlockSpec((k, d), lambda i: (0, 0)),
                  pl.BlockSpec((tm, d), lambda i: (i, 0)),
                  pl.BlockSpec((8, d), lambda i: (0, 2))],
        out_specs=pl.BlockSpec((tm, d), lambda i: (i, 0)),
        compiler_params=_cparams("arbitrary"),
    )(a, zsrc, w, xc, mods)


NA_REL_ROWS = 2 * WIN_H - 1
NA_A_LO = WIN_H - 1 - WIN_H // 2
NA_A_HI = NA_A_LO + WIN_H - 1
NA_PAIR_BOTH = NA_REL_ROWS - 1
NA_PAIR_LEFT_MASKED = NA_PAIR_BOTH
NA_PAIR_RIGHT_MASKED = NA_PAIR_BOTH + 1
NA_PAIR_MASKED = NA_PAIR_BOTH + 2
NA_PAIR_ENTRIES = NA_PAIR_BOTH + 3
NA_BAND_ROWS = NA_WIN_BLKS * NA_QROWS


def _na_pair_plan():
    nblk = GRID_ROWS // NA_QROWS
    plan = np.full((4, NA_QROWS, NA_BAND_ROWS // 2), NA_PAIR_MASKED, np.int32)
    for kind, g in ((0, 0), (1, 1), (2, nblk - 1)):
        band0 = NA_QROWS * int(np.clip(g - 1, 0, nblk - NA_WIN_BLKS))
        for i in range(NA_QROWS):
            rq = NA_QROWS * g + i
            r0 = int(np.clip(rq - WIN_H // 2, 0, GRID_ROWS - WIN_H))
            rel = [rk - rq + WIN_H - 1 if r0 <= rk < r0 + WIN_H else None
                   for rk in range(band0, band0 + NA_BAND_ROWS)]
            for p in range(NA_BAND_ROWS // 2):
                lo, hi = rel[2 * p], rel[2 * p + 1]
                if lo is not None and hi is not None:
                    plan[kind, i, p] = lo
                elif hi is not None:
                    assert hi == NA_A_LO
                    plan[kind, i, p] = NA_PAIR_LEFT_MASKED
                elif lo is not None:
                    assert lo == NA_A_HI
                    plan[kind, i, p] = NA_PAIR_RIGHT_MASKED
    return plan


def na_pair_table(rpb):
    cq = np.arange(GRID_W)[:, None]
    ck = np.arange(GRID_W)[None, :]
    c0 = np.clip(cq - WIN_W // 2, 0, GRID_W - WIN_W)
    ok = (ck >= c0) & (ck < c0 + WIN_W)
    rp = jnp.pad(rpb.astype(F32), ((0, 0), (0, 0), (GRID_W, GRID_W)))
    t = jnp.stack([rp[:, :, GRID_W + WIN_W - 1 - c:2 * GRID_W + WIN_W - 1 - c]
                   for c in range(GRID_W)], axis=2)
    t = jnp.where(jnp.asarray(ok), t * LOG2E, NEG)
    masked = jnp.full((NA_HEADS, 1, GRID_W, GRID_W), NEG, F32)
    left = jnp.concatenate([t[:, :-1], masked, t[:, NA_A_HI:NA_A_HI + 1], masked], axis=1)
    right = jnp.concatenate([t[:, 1:], t[:, NA_A_LO:NA_A_LO + 1], masked, masked], axis=1)
    return jnp.concatenate([left, right], axis=-1)


def _na_kernel(q_ref, kc_ref, k0_ref, k1_ref, k2_ref, vc_ref, v0_ref, v1_ref, v2_ref,
               pair_ref, o_ref, bias_ref, *, heads, q0, nq):
    qb = pl.program_id(1) + q0
    plan = _na_pair_plan()
    pw = 2 * GRID_W

    for kind, first in ((3, 0), (0, 1), (1, 2), (2, nq - 1)):
        @pl.when(qb == first)
        def _(kind=kind):
            for h in range(heads):
                for i in range(NA_QROWS):
                    for p in range(NA_BAND_ROWS // 2):
                        bias_ref[h, i * GRID_W:(i + 1) * GRID_W, p * pw:(p + 1) * pw] = (
                            pair_ref[h, int(plan[kind, i, p])])

    for h in range(heads):
        sl = slice(h * NA_HEAD_DIM, (h + 1) * NA_HEAD_DIM)
        q = q_ref[:, sl]
        s = [_dot_nt(q, kc_ref[:, sl])]
        for b, k_ref in enumerate((k0_ref, k1_ref, k2_ref)):
            s.append(_dot_nt(q, k_ref[:, sl]) + bias_ref[h, :, b * NA_BLK:(b + 1) * NA_BLK])
        m = s[0].max(axis=-1, keepdims=True)
        for t in s[1:]:
            m = jnp.maximum(m, t.max(axis=-1, keepdims=True))
        acc = None
        den = None
        for t, v_ref in zip(s, (vc_ref, v0_ref, v1_ref, v2_ref)):
            p = jnp.exp2(t - m)
            l = p.sum(axis=-1, keepdims=True)
            pv = _dot(p.astype(BF16), v_ref[:, sl])
            acc = pv if acc is None else acc + pv
            den = l if den is None else den + l
        o_ref[:, sl] = acc / den


def na_attention(qkv, pair_tab, with_ctx_out):
    hs = NA_HEADS_PER_STEP
    hw = hs * NA_HEAD_DIM
    nh = NA_HEADS // hs
    nq = TOKENS // NA_BLK
    q0 = 0 if with_ctx_out else 1
    nlat = GRID_ROWS // NA_QROWS

    def band(qb):
        return 1 + jnp.clip(qb - 2, 0, nlat - NA_WIN_BLKS)

    def spec(col_base, fn):
        return pl.BlockSpec((NA_BLK, hw), lambda h, g: (fn(g + q0), col_base + h))

    kspecs = [spec(nh, lambda qb: 0 * qb)] + [
        spec(nh, lambda qb, b=b: band(qb) + b) for b in range(NA_WIN_BLKS)]
    vspecs = [spec(2 * nh, lambda qb: 0 * qb)] + [
        spec(2 * nh, lambda qb, b=b: band(qb) + b) for b in range(NA_WIN_BLKS)]
    return pl.pallas_call(
        functools.partial(_na_kernel, heads=hs, q0=q0, nq=nq),
        out_shape=jax.ShapeDtypeStruct((TOKENS, D_MODEL), F32),
        grid=(nh, nq - q0),
        in_specs=[spec(0, lambda qb: qb)] + kspecs + vspecs + [
            pl.BlockSpec((hs, NA_PAIR_ENTRIES, GRID_W, 2 * GRID_W),
                         lambda h, g: (h, 0, 0, 0))],
        out_specs=pl.BlockSpec((NA_BLK, hw), lambda h, g: (g + q0, h)),
        scratch_shapes=[pltpu.VMEM((hs, NA_BLK, NA_WIN_BLKS * NA_BLK), F32)],
        compiler_params=_cparams("arbitrary", "arbitrary"),
    )(*([qkv] * 9), pair_tab)


def _ml_pre_kernel(x_ref, prev_ref, next_ref, cw_ref, cb_ref, wq_ref, wk_ref, wv_ref, wvt_ref,
                   wg_ref, bg_ref, xc_ref, q_ref, k_ref, vt_ref, g_ref, *, tm):
    i = pl.program_id(0)
    j = pl.program_id(1)
    x = x_ref[...]
    rows = i * tm + lax.broadcasted_iota(jnp.int32, (tm, 1), 0)
    local = lax.broadcasted_iota(jnp.int32, (tm, 1), 0)
    x_prev = jnp.where(local == 0, prev_ref[7:8, :], pltpu.roll(x, 1, 0))
    x_prev = jnp.where((rows == 0) | (rows == CTX_LEN), 0.0, x_prev)
    x_next = jnp.where(local == tm - 1, next_ref[0:1, :], pltpu.roll(x, tm - 1, 0))
    x_next = jnp.where((rows == CTX_LEN - 1) | (rows == TOKENS - 1), 0.0, x_next)
    conv = cb_ref[...] + x_prev * cw_ref[0:1, :] + x * cw_ref[1:2, :] + x_next * cw_ref[2:3, :]
    xconv = _silu(conv)
    xc_ref[...] = xconv
    xcb = xconv.astype(BF16)
    xb = x.astype(BF16)
    q = _dot(xcb, wq_ref[0])
    k = _dot(xcb, wk_ref[0])
    v = _dot(xb, wv_ref[0])
    qb, kb, vb = q.astype(BF16), k.astype(BF16), v.astype(BF16)
    q_ref[...] = qb
    k_ref[...] = (k * (ML_HEAD_DIM ** -0.5)).astype(BF16)
    vt_ref[...] = _dot_nt(wvt_ref[0], xb).astype(BF16)
    part = _dot_nt(wg_ref[0], qb) + _dot_nt(wg_ref[1], kb) + _dot_nt(wg_ref[2], vb)

    @pl.when(j == 0)
    def _():
        g_ref[...] = jnp.broadcast_to(bg_ref[...], g_ref.shape)

    g_ref[...] += part


def ml_pre(xz, conv_w, conv_b, wq_bd, wk_bd, wv_bd, wvt_bd, wg_t, b_gate, tm=768):
    e = ML_WIDTH
    c = ML_COLS
    nc = e // c
    last8 = TOKENS // 8 - 1
    tok = lambda i, j: (i, j)
    outs = pl.pallas_call(
        functools.partial(_ml_pre_kernel, tm=tm),
        out_shape=[jax.ShapeDtypeStruct((TOKENS, e), F32),
                   jax.ShapeDtypeStruct((TOKENS, e), BF16),
                   jax.ShapeDtypeStruct((TOKENS, e), BF16),
                   jax.ShapeDtypeStruct((e, TOKENS), BF16),
                   jax.ShapeDtypeStruct((4 * ML_HEADS, TOKENS), F32)],
        grid=(TOKENS // tm, nc),
        in_specs=[pl.BlockSpec((tm, c), tok),
                  pl.BlockSpec((8, c), lambda i, j: (jnp.maximum(i * (tm // 8) - 1, 0), j)),
                  pl.BlockSpec((8, c), lambda i, j: (jnp.minimum((i + 1) * (tm // 8), last8), j)),
                  pl.BlockSpec((3, c), lambda i, j: (0, j)),
                  pl.BlockSpec((1, c), lambda i, j: (0, j)),
                  pl.BlockSpec((1, c, c), lambda i, j: (j, 0, 0)),
                  pl.BlockSpec((1, c, c), lambda i, j: (j, 0, 0)),
                  pl.BlockSpec((1, c, c), lambda i, j: (j, 0, 0)),
                  pl.BlockSpec((1, c, c), lambda i, j: (j, 0, 0)),
                  pl.BlockSpec((3, 4 * ML_HEADS, c), lambda i, j: (0, 0, j)),
                  pl.BlockSpec((4 * ML_HEADS, 1), lambda i, j: (0, 0))],
        out_specs=[pl.BlockSpec((tm, c), tok), pl.BlockSpec((tm, c), tok),
                   pl.BlockSpec((tm, c), tok),
                   pl.BlockSpec((c, tm), lambda i, j: (j, i)),
                   pl.BlockSpec((4 * ML_HEADS, tm), lambda i, j: (0, i))],
        compiler_params=_cparams("arbitrary", "arbitrary"),
    )(xz, xz, xz, conv_w, conv_b.reshape(1, e), wq_bd, wk_bd, wv_bd, wvt_bd, wg_t,
      b_gate.reshape(4 * ML_HEADS, 1))
    return outs


def _split3(x):
    hi = x.astype(BF16).astype(F32)
    r1 = x - hi
    mid = r1.astype(BF16).astype(F32)
    lo = (r1 - mid).astype(BF16).astype(F32)
    return hi, mid, lo


def _ml_step(q_ref, k_ref, vt_ref, i_row, f_row, upto_ref, upto_t_ref, mask_ref,
             c_ref, n_ref, m_ref, h_ref, reverse):
    L = ML_CHUNK
    logf = jax.nn.log_sigmoid(f_row)
    hi, mid, lo = _split3(logf)
    rid = lax.broadcasted_iota(jnp.int32, (16, L), 0)
    r16 = jnp.where(rid == 0, hi, jnp.where(rid == 1, mid, jnp.where(rid == 2, lo, 0.0)))
    r16 = r16.astype(BF16)
    rsum = _dot(r16, upto_ref[...])
    b_row = rsum[0:1] + rsum[1:2] + rsum[2:3]
    csum = _dot_nt(upto_t_ref[...], r16)
    b_col = csum[:, 0:1] + csum[:, 1:2] + csum[:, 2:3]
    m_prev = m_ref[...]
    a_col = b_col + m_prev
    yield
    dmat = (b_col - b_row + i_row) + mask_ref[...]
    m_t = jnp.maximum(a_col, dmat.max(axis=-1, keepdims=True))
    w_inter = jnp.exp(a_col - m_t)
    yield
    q, k, vt = q_ref[...], k_ref[...], vt_ref[...]
    s = _dot_nt(q, k) * jnp.exp(dmat - m_t)
    yield
    num = w_inter * _dot_nt(q, c_ref[...].astype(BF16)) + _dot_nt(s.astype(BF16), vt)
    qn = _dot_nt(q, n_ref[...].astype(BF16))[:, 0:1]
    den = w_inter * qn + s.sum(axis=-1, keepdims=True)
    h_ref[...] = num / jnp.maximum(jnp.abs(den), jnp.exp(-m_t))
    yield
    b_end = b_row[:, 0:1] if reverse else b_row[:, L - 1:L]
    g_row = b_end - b_row + i_row
    m_new = jnp.maximum(b_end + m_prev, g_row.max(axis=-1, keepdims=True))
    decay = jnp.exp(b_end + m_prev - m_new)
    w_row = jnp.exp(g_row - m_new)
    c_ref[...] = decay * c_ref[...] + _dot((vt.astype(F32) * w_row).astype(BF16), k)
    w8 = jnp.broadcast_to(w_row, (8, L)).astype(BF16)
    n_ref[...] = decay * n_ref[...] + _dot(w8, k)
    m_ref[...] = m_new


def _ml_rec_kernel(qf_ref, kf_ref, vtf_ref, gf_ref, qb_ref, kb_ref, vtb_ref, gb_ref,
                   lo_ref, up_ref, mlo_ref, mup_ref,
                   hf_ref, hb_ref, c_ref, n_ref, m_ref):
    @pl.when(pl.program_id(1) == 0)
    def _():
        for r in (c_ref, n_ref, m_ref):
            r[...] = jnp.zeros(r.shape, F32)

    nh = ML_HEADS
    dh = ML_HEAD_DIM
    steps = []
    for hh in range(ML_REC_HEADS):
        h = pl.program_id(0) * ML_REC_HEADS + hh
        cols = pl.ds(hh * dh, dh)
        steps += [
            _ml_step(qf_ref.at[:, cols], kf_ref.at[:, cols], vtf_ref.at[cols, :],
                     gf_ref[pl.ds(h, 1), :], gf_ref[pl.ds(nh + h, 1), :],
                     up_ref, lo_ref, mlo_ref, c_ref.at[2 * hh], n_ref.at[2 * hh],
                     m_ref.at[2 * hh], hf_ref.at[:, cols], reverse=False),
            _ml_step(qb_ref.at[:, cols], kb_ref.at[:, cols], vtb_ref.at[cols, :],
                     gb_ref[pl.ds(2 * nh + h, 1), :], gb_ref[pl.ds(3 * nh + h, 1), :],
                     lo_ref, up_ref, mup_ref, c_ref.at[2 * hh + 1], n_ref.at[2 * hh + 1],
                     m_ref.at[2 * hh + 1], hb_ref.at[:, cols], reverse=True)]
    for _ in itertools.zip_longest(*steps):
        pass


def ml_recurrence(q, k, vt, gates_t):
    L = ML_CHUNK
    dh = ML_HEAD_DIM
    nchunk = TOKENS // L
    fwd = lambda j: j
    bwd = lambda j: jnp.where(j == 0, 0, nchunk - j)
    specs = []
    hw = ML_REC_HEADS * dh
    for cm in (fwd, bwd):
        specs += [pl.BlockSpec((L, hw), lambda h, j, cm=cm: (cm(j), h)),
                  pl.BlockSpec((L, hw), lambda h, j, cm=cm: (cm(j), h)),
                  pl.BlockSpec((hw, L), lambda h, j, cm=cm: (h, cm(j))),
                  pl.BlockSpec((4 * ML_HEADS, L), lambda h, j, cm=cm: (0, cm(j)))]
    nstate = 2 * ML_REC_HEADS
    state = [pltpu.VMEM((nstate, dh, dh), F32), pltpu.VMEM((nstate, 8, dh), F32),
             pltpu.VMEM((nstate, 1, 1), F32)]
    row = np.arange(L)[:, None]
    col = np.arange(L)[None, :]
    consts = [jnp.asarray(row >= col, BF16), jnp.asarray(row <= col, BF16),
              jnp.asarray(np.where(row >= col, 0.0, NEG), F32),
              jnp.asarray(np.where(row <= col, 0.0, NEG), F32)]
    specs += [pl.BlockSpec((L, L), lambda h, j: (0, 0))] * len(consts)
    return pl.pallas_call(
        _ml_rec_kernel,
        out_shape=[jax.ShapeDtypeStruct((TOKENS, ML_WIDTH), F32)] * 2,
        grid=(ML_HEADS // ML_REC_HEADS, nchunk),
        in_specs=specs,
        out_specs=[pl.BlockSpec((L, hw), lambda h, j: (fwd(j), h)),
                   pl.BlockSpec((L, hw), lambda h, j: (bwd(j), h))],
        scratch_shapes=state,
        compiler_params=_cparams("arbitrary", "arbitrary"),
    )(q, k, vt, gates_t, q, k, vt, gates_t, *consts)


def _ml_out_kernel(hf_ref, hb_ref, xc_ref, z_ref, ng_ref, skip_ref, w_ref, x_ref, gate_ref,
                   o_ref, *, tm):
    i = pl.program_id(0)
    kk = pl.program_id(1)
    h = hf_ref[...] + hb_ref[...]
    mu = jnp.mean(h, axis=-1, keepdims=True)
    hc = h - mu
    var = jnp.mean(hc * hc, axis=-1, keepdims=True)
    hn = hc * lax.rsqrt(var + EPS) * ng_ref[...]
    lhs = ((hn + skip_ref[...] * xc_ref[...]) * _silu(z_ref[...])).astype(BF16)
    part = _dot(lhs, w_ref[...])

    @pl.when(kk == 0)
    def _():
        o_ref[...] = part

    @pl.when(kk > 0)
    def _():
        o_ref[...] += part

    @pl.when(kk == pl.num_programs(1) - 1)
    def _():
        gate = _row_select(i, tm, gate_ref[1:2, :], gate_ref[0:1, :])
        o_ref[...] = x_ref[...] + gate * o_ref[...]


def ml_out(hf, hb, xconv, xz, norm_g, skip, w_out, xc, mods, tm=768):
    e = ML_WIDTH
    dh = ML_HEAD_DIM
    d = D_MODEL
    nh = ML_HEADS
    hd = lambda i, k: (i, k)
    return pl.pallas_call(
        functools.partial(_ml_out_kernel, tm=tm),
        out_shape=jax.ShapeDtypeStruct((TOKENS, d), F32),
        grid=(TOKENS // tm, nh),
        in_specs=[pl.BlockSpec((tm, dh), hd), pl.BlockSpec((tm, dh), hd),
                  pl.BlockSpec((tm, dh), hd),
                  pl.BlockSpec((tm, dh), lambda i, k: (i, nh + k)),
                  pl.BlockSpec((1, dh), lambda i, k: (0, k)),
                  pl.BlockSpec((1, dh), lambda i, k: (0, k)),
                  pl.BlockSpec((dh, d), lambda i, k: (k, 0)),
                  pl.BlockSpec((tm, d), lambda i, k: (i, 0)),
                  pl.BlockSpec((8, d), lambda i, k: (0, 2))],
        out_specs=pl.BlockSpec((tm, d), lambda i, k: (i, 0)),
        compiler_params=_cparams("arbitrary", "arbitrary"),
    )(hf, hb, xconv, xz, norm_g.reshape(1, e), skip.reshape(1, e), w_out, xc, mods)


def _cmul(ar, ai, br, bi):
    return ar * br - ai * bi, ar * bi + ai * br


def s5_operators(lam_re, lam_im, log_dt, b_re, b_im, c_re, c_im):
    hp = lax.Precision.HIGHEST
    G, P, GS, L = S5_GROUPS, S5_STATE, S5_GROUP, S5_L
    bre, bim = b_re.astype(F32).transpose(0, 2, 1), b_im.astype(F32).transpose(0, 2, 1)
    cre, cim = c_re.astype(F32), c_im.astype(F32)
    tau = jnp.arange(L + 1, dtype=F32)[None, :, None]
    seg_pows = jnp.asarray([1, S5_SEG_CHUNKS], F32) * L
    eye = jnp.eye(S5_TILE_GROUPS, dtype=F32)
    dmaps, bcs, ccts, rows = [], [], [], []
    for j in range(2):
        lr, li = lam_re[j].astype(F32), lam_im[j].astype(F32)
        dt = jnp.exp(log_dt[j].astype(F32))[:, None]
        lrd, lid = (lr * dt)[:, None, :], (li * dt)[:, None, :]
        pr = jnp.exp(lrd * tau) * jnp.cos(lid * tau)
        pi = jnp.exp(lrd * tau) * jnp.sin(lid * tau)
        ar, ai = pr[:, 1], pi[:, 1]
        den = lr * lr + li * li
        kr = (((ar - 1.0) * lr + ai * li) / den)[:, None, :]
        ki = ((ai * lr - (ar - 1.0) * li) / den)[:, None, :]
        bbr = kr * bre - ki * bim
        bbi = kr * bim + ki * bre
        car = cre[:, None] * pr[:, :L, None, :] - cim[:, None] * pi[:, :L, None, :]
        cai = cre[:, None] * pi[:, :L, None, :] + cim[:, None] * pr[:, :L, None, :]
        kk = jnp.sum(car[:, :, None] * bbr[:, None, :, None] - cai[:, :, None] * bbi[:, None, :, None],
                     axis=-1)
        kk = kk.reshape(S5_TILES, S5_TILE_GROUPS, L, GS, GS)
        dmaps.append((kk.transpose(0, 2, 1, 3, 4)[:, :, :, :, None, :]
                      * eye[None, None, :, None, :, None]).reshape(S5_TILES, L, LANES, LANES))
        pw = np.arange(L - 1, -1, -1) if j == 0 else np.arange(L)
        er, ei = _cmul(pr[:, pw, None], pi[:, pw, None], bbr[:, None], bbi[:, None])
        bcs.append(jnp.concatenate([er, ei], -1).reshape(G, L * GS, 2 * P))
        pw = np.arange(1, L + 1) if j == 0 else np.arange(L, 0, -1)
        fr, fi = _cmul(cre[:, None], cim[:, None], pr[:, pw, None], pi[:, pw, None])
        ccts.append(jnp.concatenate([fr, -fi], -1).reshape(G, L * GS, 2 * P))
        sp = seg_pows[:, None, None]
        sr = jnp.exp(lr * dt * sp) * jnp.cos(li * dt * sp)
        si = jnp.exp(lr * dt * sp) * jnp.sin(li * dt * sp)
        a1 = jnp.concatenate([sr, sr], -1).reshape(2, 1, G * 2 * P)
        a2 = jnp.concatenate([-si, si], -1).reshape(2, 1, G * 2 * P)
        rows.append(jnp.stack([a1, a2], axis=1))
    df, db = dmaps
    dsum = jnp.concatenate([jnp.flip(db[:, 1:], axis=1), df[:, :1] + db[:, :1], df[:, 1:]], axis=1)
    return (dsum.astype(BF16), jnp.stack(bcs).astype(BF16), jnp.stack(ccts).astype(BF16),
            jnp.stack(rows))


def _s5_chunk_cat(u_ref, dtype):
    n = u_ref.shape[0] // S5_L
    return jnp.concatenate([u_ref[pl.ds(s, n, stride=S5_L), :].astype(dtype)
                            for s in range(S5_L)], axis=-1)


def _s5_place_group_rows(dst_ref, src_ref, col0):
    for gl in range(S5_TILE_GROUPS):
        for s in range(S5_L):
            r = s * LANES + gl * S5_GROUP
            dst_ref[r:r + S5_GROUP, col0 + gl * LANES:col0 + (gl + 1) * LANES] = (
                src_ref[gl, s * S5_GROUP:(s + 1) * S5_GROUP, :])


def _s5_drive_kernel(u_ref, bcf_ref, bcb_ref, ef_ref, eb_ref, op_ref):
    first = (pl.program_id(0) == 0) & (pl.program_id(1) == 0)

    @pl.when(first)
    def _():
        op_ref[...] = jnp.zeros(op_ref.shape, BF16)

    @pl.when(pl.program_id(1) == 0)
    def _():
        _s5_place_group_rows(op_ref, bcf_ref, 0)
        _s5_place_group_rows(op_ref, bcb_ref, S5_TILE_STATE)

    e = _dot(_s5_chunk_cat(u_ref, BF16), op_ref[...])
    ef_ref[...] = e[:, :S5_TILE_STATE]
    eb_ref[...] = e[:, S5_TILE_STATE:]


def s5_drive(uz, bc):
    rows = TOKENS // S5_ROW_SPLIT
    out = jax.ShapeDtypeStruct((S5_CHUNKS, S5_GROUPS * 2 * S5_STATE), F32)
    ospec = pl.BlockSpec((rows // S5_L, S5_TILE_STATE), lambda q, i: (i, q))

    def bspec(d):
        return pl.BlockSpec((None, S5_TILE_GROUPS, S5_L * S5_GROUP, 2 * S5_STATE),
                            lambda q, i: (d, q, 0, 0))
    return pl.pallas_call(
        _s5_drive_kernel,
        out_shape=[out, out],
        grid=(S5_TILES, S5_ROW_SPLIT),
        in_specs=[pl.BlockSpec((rows, LANES), lambda q, i: (i, q)), bspec(0), bspec(1)],
        out_specs=[ospec, ospec],
        scratch_shapes=[pltpu.VMEM((S5_CAT, 2 * S5_TILE_STATE), BF16)],
        compiler_params=_cparams("arbitrary", "arbitrary"),
    )(uz, bc, bc)


def _s5_scan_kernel(*refs, reverse):
    ng = S5_TILE_GROUPS
    nj = S5_SEG_CHUNKS
    e_refs, a_ref, x_refs = refs[:ng], refs[ng], refs[ng + 1:2 * ng + 1]
    in_ref, es_ref = refs[-2:]
    lanes = [slice(g * LANES, (g + 1) * LANES) for g in range(ng)]
    swap = lambda z: pltpu.roll(z, S5_STATE, 1)
    for g in range(ng):
        es_ref[g] = swap(e_refs[g][...])
    a1 = [a_ref[0, 0][:, sl] for sl in lanes]
    a2 = [a_ref[0, 1][:, sl] for sl in lanes]
    seg0 = lax.broadcasted_iota(jnp.int32, (S5_SEG, LANES), 0) == 0

    def tile(j):
        jj = (nj - 1 - j) if reverse else j
        return jj, pl.ds(jj, S5_SEG, stride=nj)

    def step(j, carry, emit):
        jj, rows = tile(j)
        out = []
        for g in range(ng):
            z, zs = carry[2 * g], carry[2 * g + 1]
            if reverse:
                reset = seg0 & (jj == S5_CTX_CHUNKS - 1)
                z, zs = jnp.where(reset, 0.0, z), jnp.where(reset, 0.0, zs)
            if emit:
                x_refs[g][rows, :] = z
            out += [a1[g] * z + a2[g] * zs + e_refs[g][rows, :],
                    a1[g] * zs - a2[g] * z + es_ref[g, rows, :]]
        return tuple(out)

    zero = tuple(jnp.zeros((S5_SEG, LANES), F32) for _ in range(2 * ng))
    end = lax.fori_loop(0, nj, functools.partial(step, emit=False), zero, unroll=2)[0::2]
    order = range(S5_SEG - 1, -1, -1) if reverse else range(S5_SEG)
    for g, sl in enumerate(lanes):
        cur = end[g][0:1, :] if reverse else jnp.zeros((1, LANES), F32)
        for s in order:
            in_ref[s:s + 1, sl] = cur
            cur = a_ref[1, 0][:, sl] * cur + a_ref[1, 1][:, sl] * swap(cur) + end[g][s:s + 1, :]
    start = []
    for sl in lanes:
        start += [in_ref[:, sl], swap(in_ref[:, sl])]
    lax.fori_loop(0, nj, functools.partial(step, emit=True), tuple(start), unroll=2)


def s5_scan(e, rows, direction):
    ng = S5_TILE_GROUPS
    cols = S5_TILE_STATE
    blks = [pl.BlockSpec((S5_CHUNKS, LANES), lambda i, g=g: (0, ng * i + g)) for g in range(ng)]
    return pl.pallas_call(
        functools.partial(_s5_scan_kernel, reverse=direction == 1),
        out_shape=[jax.ShapeDtypeStruct((S5_CHUNKS, S5_TILES * LANES), F32)] * ng,
        grid=(S5_TILES,),
        in_specs=blks + [pl.BlockSpec((None, 2, 2, 1, cols),
                                      lambda i: (direction, 0, 0, 0, i))],
        out_specs=[pl.BlockSpec((S5_CHUNKS, LANES), lambda i: (0, i))] * ng,
        scratch_shapes=[pltpu.VMEM((S5_SEG, cols), F32),
                        pltpu.VMEM((ng, S5_CHUNKS, LANES), F32)],
        compiler_params=_cparams("arbitrary"),
    )(*([e] * ng), rows)


def _gelu_tanh(x):
    return 0.5 * x * (1.0 + jnp.tanh(math.sqrt(2.0 / math.pi) * (x + 0.044715 * (x * x * x))))


def _s5_read_kernel(u_ref, dsum_ref, cctf_ref, cctb_ref, d_ref, *refs):
    x_refs, (o_ref, m_ref, c_ref) = refs[:2 * S5_TILE_GROUPS], refs[2 * S5_TILE_GROUPS:]
    first = (pl.program_id(0) == 0) & (pl.program_id(1) == 0)

    @pl.when(first)
    def _():
        c_ref[...] = jnp.zeros(c_ref.shape, BF16)

    @pl.when(pl.program_id(1) == 0)
    def _():
        for s in range(S5_L):
            for t in range(S5_L):
                m_ref[s * LANES:(s + 1) * LANES, t * LANES:(t + 1) * LANES] = (
                    dsum_ref[t - s + S5_L - 1])
        _s5_place_group_rows(c_ref, cctf_ref, 0)
        _s5_place_group_rows(c_ref, cctb_ref, S5_TILE_STATE)

    n = u_ref.shape[0] // S5_L
    xs = jnp.concatenate([x[...].astype(BF16) for x in x_refs], axis=-1)
    y = _dot(_s5_chunk_cat(u_ref, BF16), m_ref[...]) + _dot_nt(xs, c_ref[...])
    for t in range(S5_L):
        rows = pl.ds(t, n, stride=S5_L)
        o_ref[rows, :] = _gelu_tanh(y[:, t * LANES:(t + 1) * LANES] + d_ref[...] * u_ref[rows, :])


def s5_read(uz, dsum, cct, xf, xb, d_skip):
    rows = TOKENS // S5_ROW_SPLIT
    ublk = pl.BlockSpec((rows, LANES), lambda q, i: (i, q))
    xblk = pl.BlockSpec((rows // S5_L, LANES), lambda q, i: (i, q))

    def cspec(d):
        return pl.BlockSpec((None, S5_TILE_GROUPS, S5_L * S5_GROUP, 2 * S5_STATE),
                            lambda q, i: (d, q, 0, 0))
    return pl.pallas_call(
        _s5_read_kernel,
        out_shape=jax.ShapeDtypeStruct((TOKENS, S5_WIDTH), F32),
        grid=(S5_TILES, S5_ROW_SPLIT),
        in_specs=[ublk,
                  pl.BlockSpec((None, 2 * S5_L - 1, LANES, LANES), lambda q, i: (q, 0, 0, 0)),
                  cspec(0), cspec(1),
                  pl.BlockSpec((1, LANES), lambda q, i: (0, q))] + [xblk] * (len(xf) + len(xb)),
        out_specs=ublk,
        scratch_shapes=[pltpu.VMEM((S5_CAT, S5_CAT), BF16),
                        pltpu.VMEM((S5_CAT, 2 * S5_TILE_STATE), BF16)],
        compiler_params=_cparams("arbitrary", "arbitrary"),
    )(uz, dsum, cct, cct, d_skip.astype(F32).reshape(1, S5_WIDTH), *xf, *xb)


def _glu_kernel(s_ref, w_ref, b_ref, o_ref, lhs_ref, *, tn):
    j = pl.program_id(1)

    @pl.when(j == 0)
    def _():
        lhs_ref[...] = s_ref[...].astype(BF16)

    v = _dot(lhs_ref[...], w_ref[...]) + b_ref[...]
    sj = s_ref[:, pl.ds(pl.multiple_of(j * tn, tn), tn)]
    o_ref[...] = sj * jax.nn.sigmoid(v)


def s5_glu(s, w_glu, b_glu, tm=768, tn=1024):
    d = S5_WIDTH
    return pl.pallas_call(
        functools.partial(_glu_kernel, tn=tn),
        out_shape=jax.ShapeDtypeStruct((TOKENS, d), F32),
        grid=(TOKENS // tm, d // tn),
        in_specs=[pl.BlockSpec((tm, d), lambda i, j: (i, 0)),
                  pl.BlockSpec((d, tn), lambda i, j: (0, j)),
                  pl.BlockSpec((1, tn), lambda i, j: (0, j))],
        out_specs=pl.BlockSpec((tm, tn), lambda i, j: (i, j)),
        scratch_shapes=[pltpu.VMEM((tm, d), BF16)],
        compiler_params=_cparams("arbitrary", "arbitrary"),
    )(s, w_glu, b_glu.reshape(1, d))


def _final_norm_kernel(x_ref, g_ref, o_ref):
    x = x_ref[...]
    o_ref[...] = x * lax.rsqrt(jnp.mean(x * x, axis=-1, keepdims=True) + EPS) * g_ref[...]


def final_norm(xc, g, tm=256):
    off = CTX_LEN // tm
    return pl.pallas_call(
        _final_norm_kernel,
        out_shape=jax.ShapeDtypeStruct((SEQ, D_MODEL), F32),
        grid=(SEQ // tm,),
        in_specs=[pl.BlockSpec((tm, D_MODEL), lambda i: (i + off, 0)),
                  pl.BlockSpec((1, D_MODEL), lambda i: (0, 0))],
        out_specs=pl.BlockSpec((tm, D_MODEL), lambda i: (i, 0)),
        compiler_params=_cparams("arbitrary"),
    )(xc, g.reshape(1, D_MODEL))


def _block_diag(w, size):
    nb = size // ML_BLOCK
    w = w.reshape(-1, nb, ML_BLOCK, ML_BLOCK)
    eye = jnp.eye(nb, dtype=w.dtype)
    return jnp.einsum('jncd,nm->jncmd', w, eye).reshape(-1, size, size)


def na_layer(xc, g, mods, w_in, rpb, w_out, with_ctx_out):
    e = D_MODEL
    h = norm_mod(xc, g, mods)
    qkv = in_proj(h, w_in, 0, 3 * e, BF16, scaled_cols=e, scale=NA_Q_SCALE)
    z = in_proj(h, w_in, 3 * e, e, F32)
    o = na_attention(qkv, na_pair_table(rpb), with_ctx_out)
    return gated_out(o, z, 0, w_out.astype(BF16), xc, mods)


def mlstm_layer(xc, g, mods, w_in, conv_w, conv_b, wq, wk, wv, w_gate, b_gate, skip, norm_g,
                w_out):
    e = ML_WIDTH
    xz = in_proj(norm_mod(xc, g, mods), w_in, 0, 2 * e, F32)
    wq_bd = _block_diag(wq, ML_COLS).astype(BF16)
    wk_bd = _block_diag(wk, ML_COLS).astype(BF16)
    wv_bd = _block_diag(wv, ML_COLS).astype(BF16)
    wvt_bd = wv_bd.transpose(0, 2, 1)
    wg_t = w_gate.reshape(3, e, 4 * ML_HEADS).transpose(0, 2, 1).astype(BF16)
    xconv, q, k, vt, gates_t = ml_pre(xz, conv_w, conv_b, wq_bd, wk_bd, wv_bd, wvt_bd, wg_t,
                                      b_gate)
    hf, hb = ml_recurrence(q, k, vt, gates_t)
    return ml_out(hf, hb, xconv, xz, norm_g, skip, w_out.astype(BF16), xc, mods)


def s5_layer(xc, g, mods, w_in, lam_re, lam_im, log_dt, b_re, b_im, c_re, c_im, d_skip,
             w_glu, b_glu, w_out):
    e = S5_WIDTH
    uz = in_proj(norm_mod(xc, g, mods), w_in, 0, 2 * e, F32)
    dsum, bc, cct, rows = s5_operators(lam_re, lam_im, log_dt, b_re, b_im, c_re, c_im)
    ef, eb = s5_drive(uz, bc)
    xf = s5_scan(ef, rows, 0)
    xb = s5_scan(eb, rows, 1)
    s = s5_read(uz, dsum, cct, xf, xb, d_skip)
    s2 = s5_glu(s, w_glu.astype(BF16), b_glu)
    return gated_out(s2, uz, e, w_out.astype(BF16), xc, mods)


def kernel(x, c, ctx, c_ctx, norm_g, ada_w, ada_b, na_w_in, na_rpb, na_w_out, ml_w_in, ml_conv_w, ml_conv_b, ml_wq, ml_wk, ml_wv, ml_w_gate, ml_b_gate, ml_skip, ml_norm_g, ml_w_out, s5_w_in, s5_lam_re, s5_lam_im, s5_log_dt, s5_b_re, s5_b_im, s5_c_re, s5_c_im, s5_d, s5_w_glu, s5_b_glu, s5_w_out, final_g):
    xc = jnp.concatenate([ctx[0], x[0]], axis=0)
    c8 = jnp.concatenate([c, c_ctx[None, :], jnp.zeros((6, D_MODEL), F32)], axis=0)
    mods = ada_mods(c8, ada_w, ada_b)
    ia = ib = ic = 0
    for layer in range(DEPTH):
        with_ctx_out = True
        kind = layer % 3
        if kind == 0:
            xc = na_layer(xc, norm_g[layer], mods[layer], na_w_in[ia], na_rpb[ia], na_w_out[ia],
                          with_ctx_out)
            ia += 1
        elif kind == 1:
            xc = mlstm_layer(xc, norm_g[layer], mods[layer], ml_w_in[ib], ml_conv_w[ib],
                             ml_conv_b[ib], ml_wq[ib], ml_wk[ib], ml_wv[ib], ml_w_gate[ib],
                             ml_b_gate[ib], ml_skip[ib], ml_norm_g[ib], ml_w_out[ib])
            ib += 1
        else:
            xc = s5_layer(xc, norm_g[layer], mods[layer], s5_w_in[ic], s5_lam_re[ic],
                          s5_lam_im[ic], s5_log_dt[ic], s5_b_re[ic], s5_b_im[ic], s5_c_re[ic],
                          s5_c_im[ic], s5_d[ic], s5_w_glu[ic], s5_b_glu[ic], s5_w_out[ic])
            ic += 1
    return final_norm(xc, final_g)[None]
```

```python
import functools
import itertools
import math

import numpy as np
import jax
import jax.numpy as jnp
from jax import lax
from jax.experimental import pallas as pl
from jax.experimental.pallas import tpu as pltpu

F32 = jnp.float32
BF16 = jnp.bfloat16

D_MODEL = 2048
SEQ = 8192
CTX_LEN = 256
TOKENS = CTX_LEN + SEQ
DEPTH = 4
EPS = 1e-6
NEG = -1e30

GRID_W = 64
GRID_ROWS = SEQ // GRID_W
NA_HEADS = 16
NA_HEAD_DIM = 128
WIN_H = 8
WIN_W = 16
NA_QROWS = 4
NA_BLK = NA_QROWS * GRID_W
NA_WIN_BLKS = 3
NA_HEADS_PER_STEP = 8
LOG2E = math.log2(math.e)
NA_Q_SCALE = NA_HEAD_DIM ** -0.5 * LOG2E

ML_WIDTH = 2 * D_MODEL
ML_HEADS = 8
ML_HEAD_DIM = ML_WIDTH // ML_HEADS
ML_BLOCK = 4
ML_CHUNK = 256
ML_REC_HEADS = 2
ML_COLS = 512

S5_WIDTH = D_MODEL
S5_GROUP = 16
S5_GROUPS = S5_WIDTH // S5_GROUP
S5_STATE = 64
S5_L = 16
S5_SEG = 8
S5_CHUNKS = TOKENS // S5_L
S5_CTX_CHUNKS = CTX_LEN // S5_L
S5_SEG_CHUNKS = S5_CHUNKS // S5_SEG
LANES = 128
S5_TILE_GROUPS = LANES // S5_GROUP
S5_TILES = S5_WIDTH // LANES
S5_CAT = S5_L * LANES
S5_TILE_STATE = S5_TILE_GROUPS * 2 * S5_STATE
S5_ROW_SPLIT = 2

VMEM_LIMIT = 56 * 1024 * 1024


def _cparams(*sem):
    return pltpu.CompilerParams(dimension_semantics=sem, vmem_limit_bytes=VMEM_LIMIT)


def _dot(a, b):
    return jnp.dot(a, b, preferred_element_type=F32)


def _dot_nt(a, b):
    return lax.dot_general(a, b, (((1,), (1,)), ((), ())), preferred_element_type=F32)


def _silu(x):
    return x * jax.nn.sigmoid(x)


def _row_select(i, tm, ctx_row, x_row):
    rows = i * tm + lax.broadcasted_iota(jnp.int32, (tm, 1), 0)
    return jnp.where(rows < CTX_LEN, ctx_row, x_row)


def _ada_kernel(c_ref, w_ref, b_ref, o_ref):
    cs = _silu(c_ref[...]).astype(BF16)
    o_ref[0] = _dot(cs, w_ref[0].astype(BF16)) + b_ref[0]


def ada_mods(c8, ada_w, ada_b):
    tn = 512
    n = 3 * D_MODEL
    return pl.pallas_call(
        _ada_kernel,
        out_shape=jax.ShapeDtypeStruct((DEPTH, 8, n), F32),
        grid=(DEPTH, n // tn),
        in_specs=[pl.BlockSpec((8, D_MODEL), lambda l, j: (0, 0)),
                  pl.BlockSpec((1, D_MODEL, tn), lambda l, j: (l, 0, j)),
                  pl.BlockSpec((1, 1, tn), lambda l, j: (l, 0, j))],
        out_specs=pl.BlockSpec((1, 8, tn), lambda l, j: (l, 0, j)),
        compiler_params=_cparams("arbitrary", "arbitrary"),
    )(c8, ada_w, ada_b.reshape(DEPTH, 1, n))


def _norm_mod_kernel(x_ref, g_ref, sh_ref, sc_ref, o_ref, *, tm):
    i = pl.program_id(0)
    x = x_ref[...]
    y = x * lax.rsqrt(jnp.mean(x * x, axis=-1, keepdims=True) + EPS) * g_ref[...]
    sc = _row_select(i, tm, sc_ref[1:2, :], sc_ref[0:1, :])
    sh = _row_select(i, tm, sh_ref[1:2, :], sh_ref[0:1, :])
    o_ref[...] = (y * (1.0 + sc) + sh).astype(BF16)


def norm_mod(xc, g, mods, tm=384):
    d = D_MODEL
    return pl.pallas_call(
        functools.partial(_norm_mod_kernel, tm=tm),
        out_shape=jax.ShapeDtypeStruct((TOKENS, d), BF16),
        grid=(TOKENS // tm,),
        in_specs=[pl.BlockSpec((tm, d), lambda i: (i, 0)),
                  pl.BlockSpec((1, d), lambda i: (0, 0)),
                  pl.BlockSpec((8, d), lambda i: (0, 0)),
                  pl.BlockSpec((8, d), lambda i: (0, 1))],
        out_specs=pl.BlockSpec((tm, d), lambda i: (i, 0)),
        compiler_params=_cparams("arbitrary"),
    )(xc, g.reshape(1, d), mods, mods)


def _in_proj_kernel(h_ref, w_ref, o_ref, wb_ref, *, scaled_blocks, scale):
    j = pl.program_id(0)

    @pl.when(pl.program_id(1) == 0)
    def _():
        wb_ref[...] = w_ref[...].astype(BF16)

    acc = _dot(h_ref[...], wb_ref[...])
    if scaled_blocks:
        acc = acc * jnp.where(j < scaled_blocks, scale, 1.0)
    o_ref[...] = acc.astype(o_ref.dtype)


def in_proj(h, w, layer, col0, ncols, out_dtype, scaled_cols=0, scale=1.0, tm=768, tn=1024):
    d = D_MODEL
    cb = col0 // tn
    return pl.pallas_call(
        functools.partial(_in_proj_kernel, scaled_blocks=scaled_cols // tn, scale=scale),
        out_shape=jax.ShapeDtypeStruct((TOKENS, ncols), out_dtype),
        grid=(ncols // tn, TOKENS // tm),
        in_specs=[pl.BlockSpec((tm, d), lambda j, i: (i, 0)),
                  pl.BlockSpec((None, d, tn), lambda j, i: (layer, 0, cb + j))],
        out_specs=pl.BlockSpec((tm, tn), lambda j, i: (i, j)),
        scratch_shapes=[pltpu.VMEM((d, tn), BF16)],
        compiler_params=_cparams("arbitrary", "arbitrary"),
    )(h, w)


def _gated_out_kernel(a_ref, z_ref, w_ref, x_ref, gate_ref, o_ref, *, tm):
    i = pl.program_id(0)
    lhs = (a_ref[...].astype(F32) * _silu(z_ref[...])).astype(BF16)
    gate = _row_select(i, tm, gate_ref[1:2, :], gate_ref[0:1, :])
    o_ref[...] = x_ref[...] + gate * _dot(lhs, w_ref[...])


def gated_out(a, zsrc, zcol0, w, xc, mods, tm=384):
    k = a.shape[1]
    d = D_MODEL
    zb = zcol0 // k
    return pl.pallas_call(
        functools.partial(_gated_out_kernel, tm=tm),
        out_shape=jax.ShapeDtypeStruct((TOKENS, d), F32),
        grid=(TOKENS // tm,),
        in_specs=[pl.BlockSpec((tm, k), lambda i: (i, 0)),
                  pl.BlockSpec((tm, k), lambda i: (i, zb)),
                  pl.BlockSpec((k, d), lambda i: (0, 0)),
                  pl.BlockSpec((tm, d), lambda i: (i, 0)),
                  pl.BlockSpec((8, d), lambda i: (0, 2))],
        out_specs=pl.BlockSpec((tm, d), lambda i: (i, 0)),
        compiler_params=_cparams("arbitrary"),
    )(a, zsrc, w, xc, mods)


NA_REL_ROWS = 2 * WIN_H - 1
NA_A_LO = WIN_H - 1 - WIN_H // 2
NA_A_HI = NA_A_LO + WIN_H - 1
NA_PAIR_BOTH = NA_REL_ROWS - 1
NA_PAIR_LEFT_MASKED = NA_PAIR_BOTH
NA_PAIR_RIGHT_MASKED = NA_PAIR_BOTH + 1
NA_PAIR_MASKED = NA_PAIR_BOTH + 2
NA_PAIR_ENTRIES = NA_PAIR_BOTH + 3
NA_BAND_ROWS = NA_WIN_BLKS * NA_QROWS


def _na_pair_plan():
    nblk = GRID_ROWS // NA_QROWS
    plan = np.full((4, NA_QROWS, NA_BAND_ROWS // 2), NA_PAIR_MASKED, np.int32)
    for kind, g in ((0, 0), (1, 1), (2, nblk - 1)):
        band0 = NA_QROWS * int(np.clip(g - 1, 0, nblk - NA_WIN_BLKS))
        for i in range(NA_QROWS):
            rq = NA_QROWS * g + i
            r0 = int(np.clip(rq - WIN_H // 2, 0, GRID_ROWS - WIN_H))
            rel = [rk - rq + WIN_H - 1 if r0 <= rk < r0 + WIN_H else None
                   for rk in range(band0, band0 + NA_BAND_ROWS)]
            for p in range(NA_BAND_ROWS // 2):
                lo, hi = rel[2 * p], rel[2 * p + 1]
                if lo is not None and hi is not None:
                    plan[kind, i, p] = lo
                elif hi is not None:
                    assert hi == NA_A_LO
                    plan[kind, i, p] = NA_PAIR_LEFT_MASKED
                elif lo is not None:
                    assert lo == NA_A_HI
                    plan[kind, i, p] = NA_PAIR_RIGHT_MASKED
    return plan


def na_pair_table(rpb):
    cq = np.arange(GRID_W)[:, None]
    ck = np.arange(GRID_W)[None, :]
    c0 = np.clip(cq - WIN_W // 2, 0, GRID_W - WIN_W)
    ok = (ck >= c0) & (ck < c0 + WIN_W)
    rp = jnp.pad(rpb.astype(F32), ((0, 0), (0, 0), (GRID_W, GRID_W)))
    t = jnp.stack([rp[:, :, GRID_W + WIN_W - 1 - c:2 * GRID_W + WIN_W - 1 - c]
                   for c in range(GRID_W)], axis=2)
    t = jnp.where(jnp.asarray(ok), t * LOG2E, NEG)
    masked = jnp.full((NA_HEADS, 1, GRID_W, GRID_W), NEG, F32)
    left = jnp.concatenate([t[:, :-1], masked, t[:, NA_A_HI:NA_A_HI + 1], masked], axis=1)
    right = jnp.concatenate([t[:, 1:], t[:, NA_A_LO:NA_A_LO + 1], masked, masked], axis=1)
    return jnp.concatenate([left, right], axis=-1)


def _na_kernel(q_ref, kc_ref, k0_ref, k1_ref, k2_ref, vc_ref, v0_ref, v1_ref, v2_ref,
               pair_ref, o_ref, bias_ref, *, heads, q0, nq):
    qb = pl.program_id(1) + q0
    plan = _na_pair_plan()
    pw = 2 * GRID_W

    for kind, first in ((3, 0), (0, 1), (1, 2), (2, nq - 1)):
        @pl.when(qb == first)
        def _(kind=kind):
            for h in range(heads):
                for i in range(NA_QROWS):
                    for p in range(NA_BAND_ROWS // 2):
                        bias_ref[h, i * GRID_W:(i + 1) * GRID_W, p * pw:(p + 1) * pw] = (
                            pair_ref[h, int(plan[kind, i, p])])

    for h in range(heads):
        sl = slice(h * NA_HEAD_DIM, (h + 1) * NA_HEAD_DIM)
        q = q_ref[:, sl]
        s = [_dot_nt(q, kc_ref[:, sl])]
        for b, k_ref in enumerate((k0_ref, k1_ref, k2_ref)):
            s.append(_dot_nt(q, k_ref[:, sl]) + bias_ref[h, :, b * NA_BLK:(b + 1) * NA_BLK])
        m = s[0].max(axis=-1, keepdims=True)
        for t in s[1:]:
            m = jnp.maximum(m, t.max(axis=-1, keepdims=True))
        acc = None
        den = None
        for t, v_ref in zip(s, (vc_ref, v0_ref, v1_ref, v2_ref)):
            p = jnp.exp2(t - m)
            l = p.sum(axis=-1, keepdims=True)
            pv = _dot(p.astype(BF16), v_ref[:, sl])
            acc = pv if acc is None else acc + pv
            den = l if den is None else den + l
        o_ref[:, sl] = acc / den


def na_attention(qkv, pair_tab, with_ctx_out):
    hs = NA_HEADS_PER_STEP
    hw = hs * NA_HEAD_DIM
    nh = NA_HEADS // hs
    nq = TOKENS // NA_BLK
    q0 = 0 if with_ctx_out else 1
    nlat = GRID_ROWS // NA_QROWS

    def band(qb):
        return 1 + jnp.clip(qb - 2, 0, nlat - NA_WIN_BLKS)

    def spec(col_base, fn):
        return pl.BlockSpec((NA_BLK, hw), lambda h, g: (fn(g + q0), col_base + h))

    kspecs = [spec(nh, lambda qb: 0 * qb)] + [
        spec(nh, lambda qb, b=b: band(qb) + b) for b in range(NA_WIN_BLKS)]
    vspecs = [spec(2 * nh, lambda qb: 0 * qb)] + [
        spec(2 * nh, lambda qb, b=b: band(qb) + b) for b in range(NA_WIN_BLKS)]
    return pl.pallas_call(
        functools.partial(_na_kernel, heads=hs, q0=q0, nq=nq),
        out_shape=jax.ShapeDtypeStruct((TOKENS, D_MODEL), F32),
        grid=(nh, nq - q0),
        in_specs=[spec(0, lambda qb: qb)] + kspecs + vspecs + [
            pl.BlockSpec((hs, NA_PAIR_ENTRIES, GRID_W, 2 * GRID_W),
                         lambda h, g: (h, 0, 0, 0))],
        out_specs=pl.BlockSpec((NA_BLK, hw), lambda h, g: (g + q0, h)),
        scratch_shapes=[pltpu.VMEM((hs, NA_BLK, NA_WIN_BLKS * NA_BLK), F32)],
        compiler_params=_cparams("arbitrary", "arbitrary"),
    )(*([qkv] * 9), pair_tab)


def _ml_pre_kernel(x_ref, prev_ref, next_ref, cw_ref, cb_ref, wq_ref, wk_ref, wv_ref, wvt_ref,
                   wg_ref, bg_ref, xc_ref, q_ref, k_ref, vt_ref, g_ref, *, tm):
    i = pl.program_id(0)
    j = pl.program_id(1)
    x = x_ref[...]
    rows = i * tm + lax.broadcasted_iota(jnp.int32, (tm, 1), 0)
    local = lax.broadcasted_iota(jnp.int32, (tm, 1), 0)
    x_prev = jnp.where(local == 0, prev_ref[7:8, :], pltpu.roll(x, 1, 0))
    x_prev = jnp.where((rows == 0) | (rows == CTX_LEN), 0.0, x_prev)
    x_next = jnp.where(local == tm - 1, next_ref[0:1, :], pltpu.roll(x, tm - 1, 0))
    x_next = jnp.where((rows == CTX_LEN - 1) | (rows == TOKENS - 1), 0.0, x_next)
    conv = cb_ref[...] + x_prev * cw_ref[0:1, :] + x * cw_ref[1:2, :] + x_next * cw_ref[2:3, :]
    xconv = _silu(conv)
    xcb = xconv.astype(BF16)
    xc_ref[...] = xcb
    xb = x.astype(BF16)
    q = _dot(xcb, wq_ref[0])
    k = _dot(xcb, wk_ref[0])
    v = _dot(xb, wv_ref[0])
    qb, kb, vb = q.astype(BF16), k.astype(BF16), v.astype(BF16)
    q_ref[...] = qb
    k_ref[...] = (k * (ML_HEAD_DIM ** -0.5)).astype(BF16)
    vt_ref[...] = _dot_nt(wvt_ref[0], xb).astype(BF16)
    part = _dot_nt(wg_ref[0], qb) + _dot_nt(wg_ref[1], kb) + _dot_nt(wg_ref[2], vb)

    @pl.when(j == 0)
    def _():
        g_ref[...] = jnp.broadcast_to(bg_ref[...], g_ref.shape)

    g_ref[...] += part


def ml_pre(xz, conv_w, conv_b, wq_bd, wk_bd, wv_bd, wvt_bd, wg_t, b_gate, tm=768):
    e = ML_WIDTH
    c = ML_COLS
    nc = e // c
    last8 = TOKENS // 8 - 1
    tok = lambda i, j: (i, j)
    outs = pl.pallas_call(
        functools.partial(_ml_pre_kernel, tm=tm),
        out_shape=[jax.ShapeDtypeStruct((TOKENS, e), BF16),
                   jax.ShapeDtypeStruct((TOKENS, e), BF16),
                   jax.ShapeDtypeStruct((TOKENS, e), BF16),
                   jax.ShapeDtypeStruct((e, TOKENS), BF16),
                   jax.ShapeDtypeStruct((4 * ML_HEADS, TOKENS), F32)],
        grid=(TOKENS // tm, nc),
        in_specs=[pl.BlockSpec((tm, c), tok),
                  pl.BlockSpec((8, c), lambda i, j: (jnp.maximum(i * (tm // 8) - 1, 0), j)),
                  pl.BlockSpec((8, c), lambda i, j: (jnp.minimum((i + 1) * (tm // 8), last8), j)),
                  pl.BlockSpec((3, c), lambda i, j: (0, j)),
                  pl.BlockSpec((1, c), lambda i, j: (0, j)),
                  pl.BlockSpec((1, c, c), lambda i, j: (j, 0, 0)),
                  pl.BlockSpec((1, c, c), lambda i, j: (j, 0, 0)),
                  pl.BlockSpec((1, c, c), lambda i, j: (j, 0, 0)),
                  pl.BlockSpec((1, c, c), lambda i, j: (j, 0, 0)),
                  pl.BlockSpec((3, 4 * ML_HEADS, c), lambda i, j: (0, 0, j)),
                  pl.BlockSpec((4 * ML_HEADS, 1), lambda i, j: (0, 0))],
        out_specs=[pl.BlockSpec((tm, c), tok), pl.BlockSpec((tm, c), tok),
                   pl.BlockSpec((tm, c), tok),
                   pl.BlockSpec((c, tm), lambda i, j: (j, i)),
                   pl.BlockSpec((4 * ML_HEADS, tm), lambda i, j: (0, i))],
        compiler_params=_cparams("arbitrary", "arbitrary"),
    )(xz, xz, xz, conv_w, conv_b.reshape(1, e), wq_bd, wk_bd, wv_bd, wvt_bd, wg_t,
      b_gate.reshape(4 * ML_HEADS, 1))
    return outs


def _split3(x):
    hi = x.astype(BF16).astype(F32)
    r1 = x - hi
    mid = r1.astype(BF16).astype(F32)
    lo = (r1 - mid).astype(BF16).astype(F32)
    return hi, mid, lo


def _ml_step(q_ref, k_ref, vt_ref, i_row, f_row, upto_ref, upto_t_ref, mask_ref,
             c_ref, n_ref, m_ref, h_ref, reverse):
    L = ML_CHUNK
    logf = jax.nn.log_sigmoid(f_row)
    hi, mid, lo = _split3(logf)
    rid = lax.broadcasted_iota(jnp.int32, (16, L), 0)
    r16 = jnp.where(rid == 0, hi, jnp.where(rid == 1, mid, jnp.where(rid == 2, lo, 0.0)))
    r16 = r16.astype(BF16)
    rsum = _dot(r16, upto_ref[...])
    b_row = rsum[0:1] + rsum[1:2] + rsum[2:3]
    csum = _dot_nt(upto_t_ref[...], r16)
    b_col = csum[:, 0:1] + csum[:, 1:2] + csum[:, 2:3]
    m_prev = m_ref[...]
    a_col = b_col + m_prev
    yield
    dmat = (b_col - b_row + i_row) + mask_ref[...]
    m_t = jnp.maximum(a_col, dmat.max(axis=-1, keepdims=True))
    w_inter = jnp.exp(a_col - m_t)
    yield
    q, k, vt = q_ref[...], k_ref[...], vt_ref[...]
    s = _dot_nt(q, k) * jnp.exp(dmat - m_t)
    yield
    num = w_inter * _dot_nt(q, c_ref[...].astype(BF16)) + _dot_nt(s.astype(BF16), vt)
    qn = _dot_nt(q, n_ref[...].astype(BF16))[:, 0:1]
    den = w_inter * qn + s.sum(axis=-1, keepdims=True)
    h_ref[...] = (num / jnp.maximum(jnp.abs(den), jnp.exp(-m_t))).astype(h_ref.dtype)
    yield
    b_end = b_row[:, 0:1] if reverse else b_row[:, L - 1:L]
    g_row = b_end - b_row + i_row
    m_new = jnp.maximum(b_end + m_prev, g_row.max(axis=-1, keepdims=True))
    decay = jnp.exp(b_end + m_prev - m_new)
    w_row = jnp.exp(g_row - m_new)
    c_ref[...] = decay * c_ref[...] + _dot((vt.astype(F32) * w_row).astype(BF16), k)
    w8 = jnp.broadcast_to(w_row, (8, L)).astype(BF16)
    n_ref[...] = decay * n_ref[...] + _dot(w8, k)
    m_ref[...] = m_new


def _ml_rec_kernel(qf_ref, kf_ref, vtf_ref, gf_ref, qb_ref, kb_ref, vtb_ref, gb_ref,
                   lo_ref, up_ref, mlo_ref, mup_ref,
                   hf_ref, hb_ref, c_ref, n_ref, m_ref):
    @pl.when(pl.program_id(1) == 0)
    def _():
        for r in (c_ref, n_ref, m_ref):
            r[...] = jnp.zeros(r.shape, F32)

    nh = ML_HEADS
    dh = ML_HEAD_DIM
    steps = []
    for hh in range(ML_REC_HEADS):
        h = pl.program_id(0) * ML_REC_HEADS + hh
        cols = pl.ds(hh * dh, dh)
        steps += [
            _ml_step(qf_ref.at[:, cols], kf_ref.at[:, cols], vtf_ref.at[cols, :],
                     gf_ref[pl.ds(h, 1), :], gf_ref[pl.ds(nh + h, 1), :],
                     up_ref, lo_ref, mlo_ref, c_ref.at[2 * hh], n_ref.at[2 * hh],
                     m_ref.at[2 * hh], hf_ref.at[:, cols], reverse=False),
            _ml_step(qb_ref.at[:, cols], kb_ref.at[:, cols], vtb_ref.at[cols, :],
                     gb_ref[pl.ds(2 * nh + h, 1), :], gb_ref[pl.ds(3 * nh + h, 1), :],
                     lo_ref, up_ref, mup_ref, c_ref.at[2 * hh + 1], n_ref.at[2 * hh + 1],
                     m_ref.at[2 * hh + 1], hb_ref.at[:, cols], reverse=True)]
    for _ in itertools.zip_longest(*steps):
        pass


def ml_recurrence(q, k, vt, gates_t):
    L = ML_CHUNK
    dh = ML_HEAD_DIM
    nchunk = TOKENS // L
    fwd = lambda j: j
    bwd = lambda j: jnp.where(j == 0, 0, nchunk - j)
    specs = []
    hw = ML_REC_HEADS * dh
    for cm in (fwd, bwd):
        specs += [pl.BlockSpec((L, hw), lambda h, j, cm=cm: (cm(j), h)),
                  pl.BlockSpec((L, hw), lambda h, j, cm=cm: (cm(j), h)),
                  pl.BlockSpec((hw, L), lambda h, j, cm=cm: (h, cm(j))),
                  pl.BlockSpec((4 * ML_HEADS, L), lambda h, j, cm=cm: (0, cm(j)))]
    nstate = 2 * ML_REC_HEADS
    state = [pltpu.VMEM((nstate, dh, dh), F32), pltpu.VMEM((nstate, 8, dh), F32),
             pltpu.VMEM((nstate, 1, 1), F32)]
    row = np.arange(L)[:, None]
    col = np.arange(L)[None, :]
    consts = [jnp.asarray(row >= col, BF16), jnp.asarray(row <= col, BF16),
              jnp.asarray(np.where(row >= col, 0.0, NEG), F32),
              jnp.asarray(np.where(row <= col, 0.0, NEG), F32)]
    specs += [pl.BlockSpec((L, L), lambda h, j: (0, 0))] * len(consts)
    return pl.pallas_call(
        _ml_rec_kernel,
        out_shape=[jax.ShapeDtypeStruct((TOKENS, ML_WIDTH), BF16)] * 2,
        grid=(ML_HEADS // ML_REC_HEADS, nchunk),
        in_specs=specs,
        out_specs=[pl.BlockSpec((L, hw), lambda h, j: (fwd(j), h)),
                   pl.BlockSpec((L, hw), lambda h, j: (bwd(j), h))],
        scratch_shapes=state,
        compiler_params=_cparams("arbitrary", "arbitrary"),
    )(q, k, vt, gates_t, q, k, vt, gates_t, *consts)


def _ml_out_kernel(hf_ref, hb_ref, xc_ref, z_ref, ng_ref, skip_ref, w_ref, x_ref, gate_ref,
                   o_ref, *, tm):
    i = pl.program_id(0)
    kk = pl.program_id(1)
    h = hf_ref[...].astype(F32) + hb_ref[...].astype(F32)
    mu = jnp.mean(h, axis=-1, keepdims=True)
    hc = h - mu
    var = jnp.mean(hc * hc, axis=-1, keepdims=True)
    hn = hc * lax.rsqrt(var + EPS) * ng_ref[...]
    lhs = ((hn + skip_ref[...] * xc_ref[...].astype(F32)) * _silu(z_ref[...])).astype(BF16)
    part = _dot(lhs, w_ref[...])

    @pl.when(kk == 0)
    def _():
        o_ref[...] = part

    @pl.when(kk > 0)
    def _():
        o_ref[...] += part

    @pl.when(kk == pl.num_programs(1) - 1)
    def _():
        gate = _row_select(i, tm, gate_ref[1:2, :], gate_ref[0:1, :])
        o_ref[...] = x_ref[...] + gate * o_ref[...]


def ml_out(hf, hb, xconv, xz, norm_g, skip, w_out, xc, mods, tm=768):
    e = ML_WIDTH
    dh = ML_HEAD_DIM
    d = D_MODEL
    nh = ML_HEADS
    hd = lambda i, k: (i, k)
    return pl.pallas_call(
        functools.partial(_ml_out_kernel, tm=tm),
        out_shape=jax.ShapeDtypeStruct((TOKENS, d), F32),
        grid=(TOKENS // tm, nh),
        in_specs=[pl.BlockSpec((tm, dh), hd), pl.BlockSpec((tm, dh), hd),
                  pl.BlockSpec((tm, dh), hd),
                  pl.BlockSpec((tm, dh), lambda i, k: (i, nh + k)),
                  pl.BlockSpec((1, dh), lambda i, k: (0, k)),
                  pl.BlockSpec((1, dh), lambda i, k: (0, k)),
                  pl.BlockSpec((dh, d), lambda i, k: (k, 0)),
                  pl.BlockSpec((tm, d), lambda i, k: (i, 0)),
                  pl.BlockSpec((8, d), lambda i, k: (0, 2))],
        out_specs=pl.BlockSpec((tm, d), lambda i, k: (i, 0)),
        compiler_params=_cparams("arbitrary", "arbitrary"),
    )(hf, hb, xconv, xz, norm_g.reshape(1, e), skip.reshape(1, e), w_out, xc, mods)


def _cmul(ar, ai, br, bi):
    return ar * br - ai * bi, ar * bi + ai * br


def s5_operators(lam_re, lam_im, log_dt, b_re, b_im, c_re, c_im):
    G, P, GS, L = S5_GROUPS, S5_STATE, S5_GROUP, S5_L
    Q, GL = S5_TILES, S5_TILE_GROUPS
    bre, bim = b_re.astype(F32).transpose(0, 2, 1), b_im.astype(F32).transpose(0, 2, 1)
    cre, cim = c_re.astype(F32), c_im.astype(F32)
    tau = jnp.arange(L + 1, dtype=F32)[None, :, None]
    seg_pows = jnp.asarray([1, S5_SEG_CHUNKS], F32) * L
    dmaps, bcs, ccts, rows = [], [], [], []
    for j in range(2):
        lr, li = lam_re[j].astype(F32), lam_im[j].astype(F32)
        dt = jnp.exp(log_dt[j].astype(F32))[:, None]
        lrd, lid = (lr * dt)[:, None, :], (li * dt)[:, None, :]
        pr = jnp.exp(lrd * tau) * jnp.cos(lid * tau)
        pi = jnp.exp(lrd * tau) * jnp.sin(lid * tau)
        ar, ai = pr[:, 1], pi[:, 1]
        den = lr * lr + li * li
        kr = (((ar - 1.0) * lr + ai * li) / den)[:, None, :]
        ki = ((ai * lr - (ar - 1.0) * li) / den)[:, None, :]
        bbr = kr * bre - ki * bim
        bbi = kr * bim + ki * bre
        tq = tau[:, :L, None, :]
        lrq, liq = (lr * dt).reshape(Q, 1, GL, P), (li * dt).reshape(Q, 1, GL, P)
        prq = (jnp.exp(lrq * tq) * jnp.cos(liq * tq))[:, :, :, None, :]
        piq = (jnp.exp(lrq * tq) * jnp.sin(liq * tq))[:, :, :, None, :]
        creq, cimq = cre.reshape(Q, 1, GL, GS, P), cim.reshape(Q, 1, GL, GS, P)
        car = (creq * prq - cimq * piq)[:, :, :, None]
        cai = (creq * piq + cimq * prq)[:, :, :, None]
        bbrq = bbr.reshape(Q, 1, GL, GS, 1, P)
        bbiq = bbi.reshape(Q, 1, GL, GS, 1, P)
        dmaps.append(jnp.sum(car * bbrq - cai * bbiq, axis=-1))
        pw = np.arange(L - 1, -1, -1) if j == 0 else np.arange(L)
        er, ei = _cmul(pr[:, pw, None], pi[:, pw, None], bbr[:, None], bbi[:, None])
        bcs.append(jnp.concatenate([er, ei], -1).reshape(G, L * GS, 2 * P))
        pw = np.arange(1, L + 1) if j == 0 else np.arange(L, 0, -1)
        fr, fi = _cmul(cre[:, None], cim[:, None], pr[:, pw, None], pi[:, pw, None])
        ccts.append(jnp.concatenate([fr, -fi], -1).reshape(G, L * GS, 2 * P))
        sp = seg_pows[:, None, None]
        sr = jnp.exp(lr * dt * sp) * jnp.cos(li * dt * sp)
        si = jnp.exp(lr * dt * sp) * jnp.sin(li * dt * sp)
        a1 = jnp.concatenate([sr, sr], -1).reshape(2, 1, G * 2 * P)
        a2 = jnp.concatenate([-si, si], -1).reshape(2, 1, G * 2 * P)
        rows.append(jnp.stack([a1, a2], axis=1))
    df, db = dmaps
    ksum = jnp.concatenate([jnp.flip(db[:, 1:], axis=1), df[:, :1] + db[:, :1], df[:, 1:]], axis=1)
    lane = np.arange(LANES)
    spread = jnp.asarray(lane[None, :] % GS == np.arange(GS)[:, None], BF16)
    tiled = jnp.dot(ksum.astype(BF16).reshape(-1, GS), spread, preferred_element_type=BF16)
    same_group = jnp.asarray(lane[:, None] // GS == lane[None, :] // GS)
    dsum = jnp.where(same_group, tiled.reshape(Q, 2 * L - 1, LANES, LANES), jnp.zeros((), BF16))
    return dsum, jnp.stack(bcs).astype(BF16), jnp.stack(ccts).astype(BF16), jnp.stack(rows)


def _s5_chunk_cat(u_ref, dtype):
    n = u_ref.shape[0] // S5_L
    return jnp.concatenate([u_ref[pl.ds(s, n, stride=S5_L), :].astype(dtype)
                            for s in range(S5_L)], axis=-1)


def _s5_place_group_rows(dst_ref, src_ref, col0):
    for gl in range(S5_TILE_GROUPS):
        for s in range(S5_L):
            r = s * LANES + gl * S5_GROUP
            dst_ref[r:r + S5_GROUP, col0 + gl * LANES:col0 + (gl + 1) * LANES] = (
                src_ref[gl, s * S5_GROUP:(s + 1) * S5_GROUP, :])


def _s5_drive_kernel(u_ref, bcf_ref, bcb_ref, ef_ref, eb_ref, op_ref):
    first = (pl.program_id(0) == 0) & (pl.program_id(1) == 0)

    @pl.when(first)
    def _():
        op_ref[...] = jnp.zeros(op_ref.shape, BF16)

    @pl.when(pl.program_id(1) == 0)
    def _():
        _s5_place_group_rows(op_ref, bcf_ref, 0)
        _s5_place_group_rows(op_ref, bcb_ref, S5_TILE_STATE)

    e = _dot(_s5_chunk_cat(u_ref, BF16), op_ref[...])
    ef_ref[...] = e[:, :S5_TILE_STATE]
    eb_ref[...] = e[:, S5_TILE_STATE:]


def s5_drive(uz, bc):
    rows = TOKENS // S5_ROW_SPLIT
    out = jax.ShapeDtypeStruct((S5_CHUNKS, S5_GROUPS * 2 * S5_STATE), F32)
    ospec = pl.BlockSpec((rows // S5_L, S5_TILE_STATE), lambda q, i: (i, q))

    def bspec(d):
        return pl.BlockSpec((None, S5_TILE_GROUPS, S5_L * S5_GROUP, 2 * S5_STATE),
                            lambda q, i: (d, q, 0, 0))
    return pl.pallas_call(
        _s5_drive_kernel,
        out_shape=[out, out],
        grid=(S5_TILES, S5_ROW_SPLIT),
        in_specs=[pl.BlockSpec((rows, LANES), lambda q, i: (i, q)), bspec(0), bspec(1)],
        out_specs=[ospec, ospec],
        scratch_shapes=[pltpu.VMEM((S5_CAT, 2 * S5_TILE_STATE), BF16)],
        compiler_params=_cparams("arbitrary", "arbitrary"),
    )(uz, bc, bc)


def _s5_scan_kernel(*refs, reverse):
    ng = S5_TILE_GROUPS
    nj = S5_SEG_CHUNKS
    e_refs, a_ref, x_refs = refs[:ng], refs[ng], refs[ng + 1:2 * ng + 1]
    in_ref, es_ref = refs[-2:]
    lanes = [slice(g * LANES, (g + 1) * LANES) for g in range(ng)]
    swap = lambda z: pltpu.roll(z, S5_STATE, 1)
    for g in range(ng):
        es_ref[g] = swap(e_refs[g][...])
    a1 = [a_ref[0, 0][:, sl] for sl in lanes]
    a2 = [a_ref[0, 1][:, sl] for sl in lanes]
    seg0 = lax.broadcasted_iota(jnp.int32, (S5_SEG, LANES), 0) == 0

    def tile(j):
        jj = (nj - 1 - j) if reverse else j
        return jj, pl.ds(jj, S5_SEG, stride=nj)

    def step(j, carry, emit):
        jj, rows = tile(j)
        out = []
        for g in range(ng):
            z, zs = carry[2 * g], carry[2 * g + 1]
            if reverse:
                reset = seg0 & (jj == S5_CTX_CHUNKS - 1)
                z, zs = jnp.where(reset, 0.0, z), jnp.where(reset, 0.0, zs)
            if emit:
                x_refs[g][rows, :] = z
            out += [a1[g] * z + a2[g] * zs + e_refs[g][rows, :],
                    a1[g] * zs - a2[g] * z + es_ref[g, rows, :]]
        return tuple(out)

    zero = tuple(jnp.zeros((S5_SEG, LANES), F32) for _ in range(2 * ng))
    end = lax.fori_loop(0, nj, functools.partial(step, emit=False), zero, unroll=2)[0::2]
    order = range(S5_SEG - 1, -1, -1) if reverse else range(S5_SEG)
    for g, sl in enumerate(lanes):
        cur = end[g][0:1, :] if reverse else jnp.zeros((1, LANES), F32)
        for s in order:
            in_ref[s:s + 1, sl] = cur
            cur = a_ref[1, 0][:, sl] * cur + a_ref[1, 1][:, sl] * swap(cur) + end[g][s:s + 1, :]
    start = []
    for sl in lanes:
        start += [in_ref[:, sl], swap(in_ref[:, sl])]
    lax.fori_loop(0, nj, functools.partial(step, emit=True), tuple(start), unroll=2)


def s5_scan(e, rows, direction):
    ng = S5_TILE_GROUPS
    cols = S5_TILE_STATE
    blks = [pl.BlockSpec((S5_CHUNKS, LANES), lambda i, g=g: (0, ng * i + g)) for g in range(ng)]
    return pl.pallas_call(
        functools.partial(_s5_scan_kernel, reverse=direction == 1),
        out_shape=[jax.ShapeDtypeStruct((S5_CHUNKS, S5_TILES * LANES), F32)] * ng,
        grid=(S5_TILES,),
        in_specs=blks + [pl.BlockSpec((None, 2, 2, 1, cols),
                                      lambda i: (direction, 0, 0, 0, i))],
        out_specs=[pl.BlockSpec((S5_CHUNKS, LANES), lambda i: (0, i))] * ng,
        scratch_shapes=[pltpu.VMEM((S5_SEG, cols), F32),
                        pltpu.VMEM((ng, S5_CHUNKS, LANES), F32)],
        compiler_params=_cparams("arbitrary"),
    )(*([e] * ng), rows)


def _gelu_tanh(x):
    return 0.5 * x * (1.0 + jnp.tanh(math.sqrt(2.0 / math.pi) * (x + 0.044715 * (x * x * x))))


def _s5_read_kernel(u_ref, dsum_ref, cctf_ref, cctb_ref, d_ref, *refs):
    x_refs, (o_ref, m_ref, c_ref) = refs[:2 * S5_TILE_GROUPS], refs[2 * S5_TILE_GROUPS:]
    first = (pl.program_id(0) == 0) & (pl.program_id(1) == 0)

    @pl.when(first)
    def _():
        c_ref[...] = jnp.zeros(c_ref.shape, BF16)

    @pl.when(pl.program_id(1) == 0)
    def _():
        for s in range(S5_L):
            for t in range(S5_L):
                m_ref[s * LANES:(s + 1) * LANES, t * LANES:(t + 1) * LANES] = (
                    dsum_ref[t - s + S5_L - 1])
        _s5_place_group_rows(c_ref, cctf_ref, 0)
        _s5_place_group_rows(c_ref, cctb_ref, S5_TILE_STATE)

    n = u_ref.shape[0] // S5_L
    xs = jnp.concatenate([x[...].astype(BF16) for x in x_refs], axis=-1)
    y = _dot(_s5_chunk_cat(u_ref, BF16), m_ref[...]) + _dot_nt(xs, c_ref[...])
    for t in range(S5_L):
        rows = pl.ds(t, n, stride=S5_L)
        o_ref[rows, :] = _gelu_tanh(y[:, t * LANES:(t + 1) * LANES] + d_ref[...] * u_ref[rows, :])


def s5_read(uz, dsum, cct, xf, xb, d_skip):
    rows = TOKENS // S5_ROW_SPLIT
    ublk = pl.BlockSpec((rows, LANES), lambda q, i: (i, q))
    xblk = pl.BlockSpec((rows // S5_L, LANES), lambda q, i: (i, q))

    def cspec(d):
        return pl.BlockSpec((None, S5_TILE_GROUPS, S5_L * S5_GROUP, 2 * S5_STATE),
                            lambda q, i: (d, q, 0, 0))
    return pl.pallas_call(
        _s5_read_kernel,
        out_shape=jax.ShapeDtypeStruct((TOKENS, S5_WIDTH), F32),
        grid=(S5_TILES, S5_ROW_SPLIT),
        in_specs=[ublk,
                  pl.BlockSpec((None, 2 * S5_L - 1, LANES, LANES), lambda q, i: (q, 0, 0, 0)),
                  cspec(0), cspec(1),
                  pl.BlockSpec((1, LANES), lambda q, i: (0, q))] + [xblk] * (len(xf) + len(xb)),
        out_specs=ublk,
        scratch_shapes=[pltpu.VMEM((S5_CAT, S5_CAT), BF16),
                        pltpu.VMEM((S5_CAT, 2 * S5_TILE_STATE), BF16)],
        compiler_params=_cparams("arbitrary", "arbitrary"),
    )(uz, dsum, cct, cct, d_skip.astype(F32).reshape(1, S5_WIDTH), *xf, *xb)


def _glu_kernel(s_ref, w_ref, b_ref, o_ref, lhs_ref, *, tn):
    j = pl.program_id(1)

    @pl.when(j == 0)
    def _():
        lhs_ref[...] = s_ref[...].astype(BF16)

    v = _dot(lhs_ref[...], w_ref[...]) + b_ref[...]
    sj = s_ref[:, pl.ds(pl.multiple_of(j * tn, tn), tn)]
    o_ref[...] = sj * jax.nn.sigmoid(v)


def s5_glu(s, w_glu, b_glu, tm=768, tn=1024):
    d = S5_WIDTH
    return pl.pallas_call(
        functools.partial(_glu_kernel, tn=tn),
        out_shape=jax.ShapeDtypeStruct((TOKENS, d), F32),
        grid=(TOKENS // tm, d // tn),
        in_specs=[pl.BlockSpec((tm, d), lambda i, j: (i, 0)),
                  pl.BlockSpec((d, tn), lambda i, j: (0, j)),
                  pl.BlockSpec((1, tn), lambda i, j: (0, j))],
        out_specs=pl.BlockSpec((tm, tn), lambda i, j: (i, j)),
        scratch_shapes=[pltpu.VMEM((tm, d), BF16)],
        compiler_params=_cparams("arbitrary", "arbitrary"),
    )(s, w_glu, b_glu.reshape(1, d))


def _final_norm_kernel(x_ref, g_ref, o_ref):
    x = x_ref[...]
    o_ref[...] = x * lax.rsqrt(jnp.mean(x * x, axis=-1, keepdims=True) + EPS) * g_ref[...]


def final_norm(xc, g, tm=256):
    off = CTX_LEN // tm
    return pl.pallas_call(
        _final_norm_kernel,
        out_shape=jax.ShapeDtypeStruct((SEQ, D_MODEL), F32),
        grid=(SEQ // tm,),
        in_specs=[pl.BlockSpec((tm, D_MODEL), lambda i: (i + off, 0)),
                  pl.BlockSpec((1, D_MODEL), lambda i: (0, 0))],
        out_specs=pl.BlockSpec((tm, D_MODEL), lambda i: (i, 0)),
        compiler_params=_cparams("arbitrary"),
    )(xc, g.reshape(1, D_MODEL))


def _block_diag(w, size):
    lane = np.arange(size)
    spread = jnp.asarray(lane[None, :] % ML_BLOCK == np.arange(ML_BLOCK)[:, None], BF16)
    tiled = jnp.dot(w.astype(BF16).reshape(-1, ML_BLOCK), spread, preferred_element_type=BF16)
    same_block = jnp.asarray(lane[:, None] // ML_BLOCK == lane[None, :] // ML_BLOCK)
    return jnp.where(same_block, tiled.reshape(-1, size, size), jnp.zeros((), BF16))


def na_layer(xc, g, mods, w_in, li, rpb, w_out, with_ctx_out):
    e = D_MODEL
    h = norm_mod(xc, g, mods)
    qkv = in_proj(h, w_in, li, 0, 3 * e, BF16, scaled_cols=e, scale=NA_Q_SCALE)
    z = in_proj(h, w_in, li, 3 * e, e, F32)
    o = na_attention(qkv, na_pair_table(rpb), with_ctx_out)
    return gated_out(o, z, 0, w_out.astype(BF16), xc, mods)


def mlstm_layer(xc, g, mods, w_in, li, conv_w, conv_b, wq, wk, wv, w_gate, b_gate, skip,
                norm_g, w_out):
    e = ML_WIDTH
    xz = in_proj(norm_mod(xc, g, mods), w_in, li, 0, 2 * e, F32)
    wq_bd = _block_diag(wq, ML_COLS)
    wk_bd = _block_diag(wk, ML_COLS)
    wv_bd = _block_diag(wv, ML_COLS)
    wvt_bd = _block_diag(wv.transpose(0, 2, 1), ML_COLS)
    wg_t = w_gate.reshape(3, e, 4 * ML_HEADS).transpose(0, 2, 1).astype(BF16)
    xconv, q, k, vt, gates_t = ml_pre(xz, conv_w, conv_b, wq_bd, wk_bd, wv_bd, wvt_bd, wg_t,
                                      b_gate)
    hf, hb = ml_recurrence(q, k, vt, gates_t)
    return ml_out(hf, hb, xconv, xz, norm_g, skip, w_out.astype(BF16), xc, mods)


def s5_layer(xc, g, mods, w_in, li, lam_re, lam_im, log_dt, b_re, b_im, c_re, c_im, d_skip,
             w_glu, b_glu, w_out):
    e = S5_WIDTH
    uz = in_proj(norm_mod(xc, g, mods), w_in, li, 0, 2 * e, F32)
    dsum, bc, cct, rows = s5_operators(lam_re, lam_im, log_dt, b_re, b_im, c_re, c_im)
    ef, eb = s5_drive(uz, bc)
    xf = s5_scan(ef, rows, 0)
    xb = s5_scan(eb, rows, 1)
    s = s5_read(uz, dsum, cct, xf, xb, d_skip)
    s2 = s5_glu(s, w_glu.astype(BF16), b_glu)
    return gated_out(s2, uz, e, w_out.astype(BF16), xc, mods)


def kernel(x, c, ctx, c_ctx, norm_g, ada_w, ada_b, na_w_in, na_rpb, na_w_out, ml_w_in, ml_conv_w, ml_conv_b, ml_wq, ml_wk, ml_wv, ml_w_gate, ml_b_gate, ml_skip, ml_norm_g, ml_w_out, s5_w_in, s5_lam_re, s5_lam_im, s5_log_dt, s5_b_re, s5_b_im, s5_c_re, s5_c_im, s5_d, s5_w_glu, s5_b_glu, s5_w_out, final_g):
    xc = jnp.concatenate([ctx[0], x[0]], axis=0)
    c8 = jnp.concatenate([c, c_ctx[None, :], jnp.zeros((6, D_MODEL), F32)], axis=0)
    mods = ada_mods(c8, ada_w, ada_b)
    ia = ib = ic = 0
    for layer in range(DEPTH):
        with_ctx_out = True
        kind = layer % 3
        if kind == 0:
            xc = na_layer(xc, norm_g[layer], mods[layer], na_w_in, ia, na_rpb[ia], na_w_out[ia],
                          with_ctx_out)
            ia += 1
        elif kind == 1:
            xc = mlstm_layer(xc, norm_g[layer], mods[layer], ml_w_in, ib, ml_conv_w[ib],
                             ml_conv_b[ib], ml_wq[ib], ml_wk[ib], ml_wv[ib], ml_w_gate[ib],
                             ml_b_gate[ib], ml_skip[ib], ml_norm_g[ib], ml_w_out[ib])
            ib += 1
        else:
            xc = s5_layer(xc, norm_g[layer], mods[layer], s5_w_in, ic, s5_lam_re[ic],
                          s5_lam_im[ic], s5_log_dt[ic], s5_b_re[ic], s5_b_im[ic], s5_c_re[ic],
                          s5_c_im[ic], s5_d[ic], s5_w_glu[ic], s5_b_glu[ic], s5_w_out[ic])
            ic += 1
    return final_norm(xc, final_g)[None]
```

```python
import functools
import itertools
import math

import numpy as np
import jax
import jax.numpy as jnp
from jax import lax
from jax.experimental import pallas as pl
from jax.experimental.pallas import tpu as pltpu

F32 = jnp.float32
BF16 = jnp.bfloat16

D_MODEL = 2048
SEQ = 8192
CTX_LEN = 256
TOKENS = CTX_LEN + SEQ
DEPTH = 4
EPS = 1e-6
NEG = -1e30

GRID_W = 64
GRID_ROWS = SEQ // GRID_W
NA_HEADS = 16
NA_HEAD_DIM = 128
WIN_H = 8
WIN_W = 16
NA_QROWS = 4
NA_BLK = NA_QROWS * GRID_W
NA_WIN_BLKS = 3
NA_HEADS_PER_STEP = 8
LOG2E = math.log2(math.e)
NA_Q_SCALE = NA_HEAD_DIM ** -0.5 * LOG2E

ML_WIDTH = 2 * D_MODEL
ML_HEADS = 8
ML_HEAD_DIM = ML_WIDTH // ML_HEADS
ML_BLOCK = 4
ML_CHUNK = 256
ML_REC_HEADS = 2
ML_COLS = 512

S5_WIDTH = D_MODEL
S5_GROUP = 16
S5_GROUPS = S5_WIDTH // S5_GROUP
S5_STATE = 64
S5_L = 16
S5_SEG = 8
S5_CHUNKS = TOKENS // S5_L
S5_CTX_CHUNKS = CTX_LEN // S5_L
S5_SEG_CHUNKS = S5_CHUNKS // S5_SEG
LANES = 128
S5_TILE_GROUPS = LANES // S5_GROUP
S5_TILES = S5_WIDTH // LANES
S5_CAT = S5_L * LANES
S5_TILE_STATE = S5_TILE_GROUPS * 2 * S5_STATE
S5_ROW_SPLIT = 2

VMEM_LIMIT = 56 * 1024 * 1024


def _cparams(*sem):
    return pltpu.CompilerParams(dimension_semantics=sem, vmem_limit_bytes=VMEM_LIMIT)


def _dot(a, b):
    return jnp.dot(a, b, preferred_element_type=F32)


def _dot_nt(a, b):
    return lax.dot_general(a, b, (((1,), (1,)), ((), ())), preferred_element_type=F32)


def _silu(x):
    return x * jax.nn.sigmoid(x)


def _row_select(row0, n, ctx_row, x_row):
    rows = row0 + lax.broadcasted_iota(jnp.int32, (n, 1), 0)
    return jnp.where(rows < CTX_LEN, ctx_row, x_row)


def _ada_kernel(c_ref, w_ref, b_ref, o_ref):
    cs = _silu(c_ref[...]).astype(BF16)
    o_ref[0] = _dot(cs, w_ref[0].astype(BF16)) + b_ref[0]


def ada_mods(c8, ada_w, ada_b):
    tn = 512
    n = 3 * D_MODEL
    return pl.pallas_call(
        _ada_kernel,
        out_shape=jax.ShapeDtypeStruct((DEPTH, 8, n), F32),
        grid=(DEPTH, n // tn),
        in_specs=[pl.BlockSpec((8, D_MODEL), lambda l, j: (0, 0)),
                  pl.BlockSpec((1, D_MODEL, tn), lambda l, j: (l, 0, j)),
                  pl.BlockSpec((1, 1, tn), lambda l, j: (l, 0, j))],
        out_specs=pl.BlockSpec((1, 8, tn), lambda l, j: (l, 0, j)),
        compiler_params=_cparams("arbitrary", "arbitrary"),
    )(c8, ada_w, ada_b.reshape(DEPTH, 1, n))


def _rmsnorm(x, g):
    return x * lax.rsqrt(jnp.mean(x * x, axis=-1, keepdims=True) + EPS) * g


def _norm_mod_rows(x, g_ref, sh_ref, sc_ref, row0):
    sc = _row_select(row0, x.shape[0], sc_ref[1:2, :], sc_ref[0:1, :])
    sh = _row_select(row0, x.shape[0], sh_ref[1:2, :], sh_ref[0:1, :])
    return (_rmsnorm(x, g_ref[...]) * (1.0 + sc) + sh).astype(BF16)


def _norm_specs(g, mods):
    d = D_MODEL
    return ([g.reshape(1, d), mods, mods],
            [pl.BlockSpec((1, d), lambda *a: (0, 0)), pl.BlockSpec((8, d), lambda *a: (0, 0)),
             pl.BlockSpec((8, d), lambda *a: (0, 1))])


def _norm_mod_kernel(x_ref, g_ref, sh_ref, sc_ref, o_ref, *, tm):
    o_ref[...] = _norm_mod_rows(x_ref[...], g_ref, sh_ref, sc_ref, pl.program_id(0) * tm)


def norm_mod(xc, g, mods, tm=384):
    d = D_MODEL
    nargs, nspecs = _norm_specs(g, mods)
    return pl.pallas_call(
        functools.partial(_norm_mod_kernel, tm=tm),
        out_shape=jax.ShapeDtypeStruct((TOKENS, d), BF16),
        grid=(TOKENS // tm,),
        in_specs=[pl.BlockSpec((tm, d), lambda i: (i, 0))] + nspecs,
        out_specs=pl.BlockSpec((tm, d), lambda i: (i, 0)),
        compiler_params=_cparams("arbitrary"),
    )(xc, *nargs)


def _in_proj_kernel(h_ref, w_ref, o_ref, wb_ref, *, scaled_blocks, scale):
    j = pl.program_id(0)

    @pl.when(pl.program_id(1) == 0)
    def _():
        wb_ref[...] = w_ref[...].astype(BF16)

    acc = _dot(h_ref[...], wb_ref[...])
    if scaled_blocks:
        acc = acc * jnp.where(j < scaled_blocks, scale, 1.0)
    o_ref[...] = acc.astype(o_ref.dtype)


def in_proj(h, w, layer, col0, ncols, out_dtype, scaled_cols=0, scale=1.0, tm=768, tn=1024):
    d = D_MODEL
    cb = col0 // tn
    return pl.pallas_call(
        functools.partial(_in_proj_kernel, scaled_blocks=scaled_cols // tn, scale=scale),
        out_shape=jax.ShapeDtypeStruct((TOKENS, ncols), out_dtype),
        grid=(ncols // tn, TOKENS // tm),
        in_specs=[pl.BlockSpec((tm, d), lambda j, i: (i, 0)),
                  pl.BlockSpec((None, d, tn), lambda j, i: (layer, 0, cb + j))],
        out_specs=pl.BlockSpec((tm, tn), lambda j, i: (i, j)),
        scratch_shapes=[pltpu.VMEM((d, tn), BF16)],
        compiler_params=_cparams("arbitrary", "arbitrary"),
    )(h, w)


def _gated_out_kernel(a_ref, z_ref, w_ref, x_ref, gate_ref, g_ref, sh_ref, sc_ref,
                      o_ref, h_ref, *, tm):
    i = pl.program_id(0)
    lhs = (a_ref[...].astype(F32) * _silu(z_ref[...])).astype(BF16)
    gate = _row_select(i * tm, tm, gate_ref[1:2, :], gate_ref[0:1, :])
    x = x_ref[...] + gate * _dot(lhs, w_ref[...])
    o_ref[...] = x
    h_ref[...] = _norm_mod_rows(x, g_ref, sh_ref, sc_ref, i * tm)


def gated_out(a, zsrc, zcol0, w, xc, mods, next_g, next_mods, tm=384):
    k = a.shape[1]
    d = D_MODEL
    zb = zcol0 // k
    nargs, nspecs = _norm_specs(next_g, next_mods)
    row = pl.BlockSpec((tm, d), lambda i: (i, 0))
    return pl.pallas_call(
        functools.partial(_gated_out_kernel, tm=tm),
        out_shape=[jax.ShapeDtypeStruct((TOKENS, d), F32),
                   jax.ShapeDtypeStruct((TOKENS, d), BF16)],
        grid=(TOKENS // tm,),
        in_specs=[pl.BlockSpec((tm, k), lambda i: (i, 0)),
                  pl.BlockSpec((tm, k), lambda i: (i, zb)),
                  pl.BlockSpec((k, d), lambda i: (0, 0)),
                  row,
                  pl.BlockSpec((8, d), lambda i: (0, 2))] + nspecs,
        out_specs=[row, row],
        compiler_params=_cparams("arbitrary"),
    )(a, zsrc, w, xc, mods, *nargs)


def _gated_out_final_kernel(a_ref, z_ref, w_ref, x_ref, gate_ref, g_ref, o_ref):
    lhs = (a_ref[...].astype(F32) * _silu(z_ref[...])).astype(BF16)
    x = x_ref[...] + gate_ref[0:1, :] * _dot(lhs, w_ref[...])
    o_ref[...] = _rmsnorm(x, g_ref[...])


def gated_out_final(a, zsrc, zcol0, w, xc, mods, final_g, tm=256):
    k = a.shape[1]
    d = D_MODEL
    zb = zcol0 // k
    off = CTX_LEN // tm
    return pl.pallas_call(
        _gated_out_final_kernel,
        out_shape=jax.ShapeDtypeStruct((SEQ, d), F32),
        grid=(SEQ // tm,),
        in_specs=[pl.BlockSpec((tm, k), lambda i: (i + off, 0)),
                  pl.BlockSpec((tm, k), lambda i: (i + off, zb)),
                  pl.BlockSpec((k, d), lambda i: (0, 0)),
                  pl.BlockSpec((tm, d), lambda i: (i + off, 0)),
                  pl.BlockSpec((8, d), lambda i: (0, 2)),
                  pl.BlockSpec((1, d), lambda i: (0, 0))],
        out_specs=pl.BlockSpec((tm, d), lambda i: (i, 0)),
        compiler_params=_cparams("arbitrary"),
    )(a, zsrc, w, xc, mods, final_g.reshape(1, d))


NA_REL_ROWS = 2 * WIN_H - 1
NA_A_LO = WIN_H - 1 - WIN_H // 2
NA_A_HI = NA_A_LO + WIN_H - 1
NA_PAIR_BOTH = NA_REL_ROWS - 1
NA_PAIR_LEFT_MASKED = NA_PAIR_BOTH
NA_PAIR_RIGHT_MASKED = NA_PAIR_BOTH + 1
NA_PAIR_MASKED = NA_PAIR_BOTH + 2
NA_PAIR_ENTRIES = NA_PAIR_BOTH + 3
NA_BAND_ROWS = NA_WIN_BLKS * NA_QROWS


def _na_pair_plan():
    nblk = GRID_ROWS // NA_QROWS
    plan = np.full((4, NA_QROWS, NA_BAND_ROWS // 2), NA_PAIR_MASKED, np.int32)
    for kind, g in ((0, 0), (1, 1), (2, nblk - 1)):
        band0 = NA_QROWS * int(np.clip(g - 1, 0, nblk - NA_WIN_BLKS))
        for i in range(NA_QROWS):
            rq = NA_QROWS * g + i
            r0 = int(np.clip(rq - WIN_H // 2, 0, GRID_ROWS - WIN_H))
            rel = [rk - rq + WIN_H - 1 if r0 <= rk < r0 + WIN_H else None
                   for rk in range(band0, band0 + NA_BAND_ROWS)]
            for p in range(NA_BAND_ROWS // 2):
                lo, hi = rel[2 * p], rel[2 * p + 1]
                if lo is not None and hi is not None:
                    plan[kind, i, p] = lo
                elif hi is not None:
                    assert hi == NA_A_LO
                    plan[kind, i, p] = NA_PAIR_LEFT_MASKED
                elif lo is not None:
                    assert lo == NA_A_HI
                    plan[kind, i, p] = NA_PAIR_RIGHT_MASKED
    return plan


def na_pair_table(rpb):
    cq = np.arange(GRID_W)[:, None]
    ck = np.arange(GRID_W)[None, :]
    c0 = np.clip(cq - WIN_W // 2, 0, GRID_W - WIN_W)
    ok = (ck >= c0) & (ck < c0 + WIN_W)
    rp = jnp.pad(rpb.astype(F32), ((0, 0), (0, 0), (GRID_W, GRID_W)))
    t = jnp.stack([rp[:, :, GRID_W + WIN_W - 1 - c:2 * GRID_W + WIN_W - 1 - c]
                   for c in range(GRID_W)], axis=2)
    t = jnp.where(jnp.asarray(ok), t * LOG2E, NEG)
    masked = jnp.full((NA_HEADS, 1, GRID_W, GRID_W), NEG, F32)
    left = jnp.concatenate([t[:, :-1], masked, t[:, NA_A_HI:NA_A_HI + 1], masked], axis=1)
    right = jnp.concatenate([t[:, 1:], t[:, NA_A_LO:NA_A_LO + 1], masked, masked], axis=1)
    return jnp.concatenate([left, right], axis=-1)


def _na_kernel(q_ref, kc_ref, k0_ref, k1_ref, k2_ref, vc_ref, v0_ref, v1_ref, v2_ref,
               pair_ref, o_ref, bias_ref, *, heads, q0, nq):
    qb = pl.program_id(1) + q0
    plan = _na_pair_plan()
    pw = 2 * GRID_W

    for kind, first in ((3, 0), (0, 1), (1, 2), (2, nq - 1)):
        @pl.when(qb == first)
        def _(kind=kind):
            for h in range(heads):
                for i in range(NA_QROWS):
                    for p in range(NA_BAND_ROWS // 2):
                        bias_ref[h, i * GRID_W:(i + 1) * GRID_W, p * pw:(p + 1) * pw] = (
                            pair_ref[h, int(plan[kind, i, p])])

    for h in range(heads):
        sl = slice(h * NA_HEAD_DIM, (h + 1) * NA_HEAD_DIM)
        q = q_ref[:, sl]
        s = [_dot_nt(q, kc_ref[:, sl])]
        for b, k_ref in enumerate((k0_ref, k1_ref, k2_ref)):
            s.append(_dot_nt(q, k_ref[:, sl]) + bias_ref[h, :, b * NA_BLK:(b + 1) * NA_BLK])
        m = s[0].max(axis=-1, keepdims=True)
        for t in s[1:]:
            m = jnp.maximum(m, t.max(axis=-1, keepdims=True))
        acc = None
        den = None
        for t, v_ref in zip(s, (vc_ref, v0_ref, v1_ref, v2_ref)):
            p = jnp.exp2(t - m)
            l = p.sum(axis=-1, keepdims=True)
            pv = _dot(p.astype(BF16), v_ref[:, sl])
            acc = pv if acc is None else acc + pv
            den = l if den is None else den + l
        o_ref[:, sl] = acc / den


def na_attention(qkv, pair_tab, with_ctx_out):
    hs = NA_HEADS_PER_STEP
    hw = hs * NA_HEAD_DIM
    nh = NA_HEADS // hs
    nq = TOKENS // NA_BLK
    q0 = 0 if with_ctx_out else 1
    nlat = GRID_ROWS // NA_QROWS

    def band(qb):
        return 1 + jnp.clip(qb - 2, 0, nlat - NA_WIN_BLKS)

    def spec(col_base, fn):
        return pl.BlockSpec((NA_BLK, hw), lambda h, g: (fn(g + q0), col_base + h))

    kspecs = [spec(nh, lambda qb: 0 * qb)] + [
        spec(nh, lambda qb, b=b: band(qb) + b) for b in range(NA_WIN_BLKS)]
    vspecs = [spec(2 * nh, lambda qb: 0 * qb)] + [
        spec(2 * nh, lambda qb, b=b: band(qb) + b) for b in range(NA_WIN_BLKS)]
    return pl.pallas_call(
        functools.partial(_na_kernel, heads=hs, q0=q0, nq=nq),
        out_shape=jax.ShapeDtypeStruct((TOKENS, D_MODEL), F32),
        grid=(nh, nq - q0),
        in_specs=[spec(0, lambda qb: qb)] + kspecs + vspecs + [
            pl.BlockSpec((hs, NA_PAIR_ENTRIES, GRID_W, 2 * GRID_W),
                         lambda h, g: (h, 0, 0, 0))],
        out_specs=pl.BlockSpec((NA_BLK, hw), lambda h, g: (g + q0, h)),
        scratch_shapes=[pltpu.VMEM((hs, NA_BLK, NA_WIN_BLKS * NA_BLK), F32)],
        compiler_params=_cparams("arbitrary", "arbitrary"),
    )(*([qkv] * 9), pair_tab)


def _ml_pre_kernel(x_ref, prev_ref, next_ref, cw_ref, cb_ref, wq_ref, wk_ref, wv_ref, wvt_ref,
                   wg_ref, bg_ref, xc_ref, q_ref, k_ref, vt_ref, g_ref, *, tm):
    i = pl.program_id(0)
    j = pl.program_id(1)
    x = x_ref[...]
    rows = i * tm + lax.broadcasted_iota(jnp.int32, (tm, 1), 0)
    local = lax.broadcasted_iota(jnp.int32, (tm, 1), 0)
    x_prev = jnp.where(local == 0, prev_ref[7:8, :], pltpu.roll(x, 1, 0))
    x_prev = jnp.where((rows == 0) | (rows == CTX_LEN), 0.0, x_prev)
    x_next = jnp.where(local == tm - 1, next_ref[0:1, :], pltpu.roll(x, tm - 1, 0))
    x_next = jnp.where((rows == CTX_LEN - 1) | (rows == TOKENS - 1), 0.0, x_next)
    conv = cb_ref[...] + x_prev * cw_ref[0:1, :] + x * cw_ref[1:2, :] + x_next * cw_ref[2:3, :]
    xconv = _silu(conv)
    xcb = xconv.astype(BF16)
    xc_ref[...] = xcb
    xb = x.astype(BF16)
    q = _dot(xcb, wq_ref[0])
    k = _dot(xcb, wk_ref[0])
    v = _dot(xb, wv_ref[0])
    qb, kb, vb = q.astype(BF16), k.astype(BF16), v.astype(BF16)
    q_ref[...] = qb
    k_ref[...] = (k * (ML_HEAD_DIM ** -0.5)).astype(BF16)
    vt_ref[...] = _dot_nt(wvt_ref[0], xb).astype(BF16)
    part = _dot_nt(wg_ref[0], qb) + _dot_nt(wg_ref[1], kb) + _dot_nt(wg_ref[2], vb)

    @pl.when(j == 0)
    def _():
        g_ref[...] = jnp.broadcast_to(bg_ref[...], g_ref.shape)

    g_ref[...] += part


def ml_pre(xz, conv_w, conv_b, wq_bd, wk_bd, wv_bd, wvt_bd, wg_t, b_gate, tm=768):
    e = ML_WIDTH
    c = ML_COLS
    nc = e // c
    last8 = TOKENS // 8 - 1
    tok = lambda i, j: (i, j)
    outs = pl.pallas_call(
        functools.partial(_ml_pre_kernel, tm=tm),
        out_shape=[jax.ShapeDtypeStruct((TOKENS, e), BF16),
                   jax.ShapeDtypeStruct((TOKENS, e), BF16),
                   jax.ShapeDtypeStruct((TOKENS, e), BF16),
                   jax.ShapeDtypeStruct((e, TOKENS), BF16),
                   jax.ShapeDtypeStruct((4 * ML_HEADS, TOKENS), F32)],
        grid=(TOKENS // tm, nc),
        in_specs=[pl.BlockSpec((tm, c), tok),
                  pl.BlockSpec((8, c), lambda i, j: (jnp.maximum(i * (tm // 8) - 1, 0), j)),
                  pl.BlockSpec((8, c), lambda i, j: (jnp.minimum((i + 1) * (tm // 8), last8), j)),
                  pl.BlockSpec((3, c), lambda i, j: (0, j)),
                  pl.BlockSpec((1, c), lambda i, j: (0, j)),
                  pl.BlockSpec((1, c, c), lambda i, j: (j, 0, 0)),
                  pl.BlockSpec((1, c, c), lambda i, j: (j, 0, 0)),
                  pl.BlockSpec((1, c, c), lambda i, j: (j, 0, 0)),
                  pl.BlockSpec((1, c, c), lambda i, j: (j, 0, 0)),
                  pl.BlockSpec((3, 4 * ML_HEADS, c), lambda i, j: (0, 0, j)),
                  pl.BlockSpec((4 * ML_HEADS, 1), lambda i, j: (0, 0))],
        out_specs=[pl.BlockSpec((tm, c), tok), pl.BlockSpec((tm, c), tok),
                   pl.BlockSpec((tm, c), tok),
                   pl.BlockSpec((c, tm), lambda i, j: (j, i)),
                   pl.BlockSpec((4 * ML_HEADS, tm), lambda i, j: (0, i))],
        compiler_params=_cparams("arbitrary", "arbitrary"),
    )(xz, xz, xz, conv_w, conv_b.reshape(1, e), wq_bd, wk_bd, wv_bd, wvt_bd, wg_t,
      b_gate.reshape(4 * ML_HEADS, 1))
    return outs


def _split3(x):
    hi = x.astype(BF16).astype(F32)
    r1 = x - hi
    mid = r1.astype(BF16).astype(F32)
    lo = (r1 - mid).astype(BF16).astype(F32)
    return hi, mid, lo


def _ml_step(q_ref, k_ref, vt_ref, i_row, f_row, upto_ref, upto_t_ref, mask_ref,
             c_ref, n_ref, m_ref, h_ref, reverse):
    L = ML_CHUNK
    logf = jax.nn.log_sigmoid(f_row)
    hi, mid, lo = _split3(logf)
    rid = lax.broadcasted_iota(jnp.int32, (16, L), 0)
    r16 = jnp.where(rid == 0, hi, jnp.where(rid == 1, mid, jnp.where(rid == 2, lo, 0.0)))
    r16 = r16.astype(BF16)
    rsum = _dot(r16, upto_ref[...])
    b_row = rsum[0:1] + rsum[1:2] + rsum[2:3]
    csum = _dot_nt(upto_t_ref[...], r16)
    b_col = csum[:, 0:1] + csum[:, 1:2] + csum[:, 2:3]
    m_prev = m_ref[...]
    a_col = b_col + m_prev
    yield
    dmat = (b_col - b_row + i_row) + mask_ref[...]
    m_t = jnp.maximum(a_col, dmat.max(axis=-1, keepdims=True))
    w_inter = jnp.exp(a_col - m_t)
    yield
    q, k, vt = q_ref[...], k_ref[...], vt_ref[...]
    s = _dot_nt(q, k) * jnp.exp(dmat - m_t)
    yield
    num = w_inter * _dot_nt(q, c_ref[...].astype(BF16)) + _dot_nt(s.astype(BF16), vt)
    qn = _dot_nt(q, n_ref[...].astype(BF16))[:, 0:1]
    den = w_inter * qn + s.sum(axis=-1, keepdims=True)
    h_ref[...] = (num / jnp.maximum(jnp.abs(den), jnp.exp(-m_t))).astype(h_ref.dtype)
    yield
    b_end = b_row[:, 0:1] if reverse else b_row[:, L - 1:L]
    g_row = b_end - b_row + i_row
    m_new = jnp.maximum(b_end + m_prev, g_row.max(axis=-1, keepdims=True))
    decay = jnp.exp(b_end + m_prev - m_new)
    w_row = jnp.exp(g_row - m_new)
    c_ref[...] = decay * c_ref[...] + _dot((vt.astype(F32) * w_row).astype(BF16), k)
    w8 = jnp.broadcast_to(w_row, (8, L)).astype(BF16)
    n_ref[...] = decay * n_ref[...] + _dot(w8, k)
    m_ref[...] = m_new


def _ml_rec_kernel(qf_ref, kf_ref, vtf_ref, gf_ref, qb_ref, kb_ref, vtb_ref, gb_ref,
                   lo_ref, up_ref, mlo_ref, mup_ref,
                   hf_ref, hb_ref, c_ref, n_ref, m_ref):
    @pl.when(pl.program_id(1) == 0)
    def _():
        for r in (c_ref, n_ref, m_ref):
            r[...] = jnp.zeros(r.shape, F32)

    nh = ML_HEADS
    dh = ML_HEAD_DIM
    steps = []
    for hh in range(ML_REC_HEADS):
        h = pl.program_id(0) * ML_REC_HEADS + hh
        cols = pl.ds(hh * dh, dh)
        steps += [
            _ml_step(qf_ref.at[:, cols], kf_ref.at[:, cols], vtf_ref.at[cols, :],
                     gf_ref[pl.ds(h, 1), :], gf_ref[pl.ds(nh + h, 1), :],
                     up_ref, lo_ref, mlo_ref, c_ref.at[2 * hh], n_ref.at[2 * hh],
                     m_ref.at[2 * hh], hf_ref.at[:, cols], reverse=False),
            _ml_step(qb_ref.at[:, cols], kb_ref.at[:, cols], vtb_ref.at[cols, :],
                     gb_ref[pl.ds(2 * nh + h, 1), :], gb_ref[pl.ds(3 * nh + h, 1), :],
                     lo_ref, up_ref, mup_ref, c_ref.at[2 * hh + 1], n_ref.at[2 * hh + 1],
                     m_ref.at[2 * hh + 1], hb_ref.at[:, cols], reverse=True)]
    for _ in itertools.zip_longest(*steps):
        pass


def ml_recurrence(q, k, vt, gates_t):
    L = ML_CHUNK
    dh = ML_HEAD_DIM
    nchunk = TOKENS // L
    fwd = lambda j: j
    bwd = lambda j: jnp.where(j == 0, 0, nchunk - j)
    specs = []
    hw = ML_REC_HEADS * dh
    for cm in (fwd, bwd):
        specs += [pl.BlockSpec((L, hw), lambda h, j, cm=cm: (cm(j), h)),
                  pl.BlockSpec((L, hw), lambda h, j, cm=cm: (cm(j), h)),
                  pl.BlockSpec((hw, L), lambda h, j, cm=cm: (h, cm(j))),
                  pl.BlockSpec((4 * ML_HEADS, L), lambda h, j, cm=cm: (0, cm(j)))]
    nstate = 2 * ML_REC_HEADS
    state = [pltpu.VMEM((nstate, dh, dh), F32), pltpu.VMEM((nstate, 8, dh), F32),
             pltpu.VMEM((nstate, 1, 1), F32)]
    row = np.arange(L)[:, None]
    col = np.arange(L)[None, :]
    consts = [jnp.asarray(row >= col, BF16), jnp.asarray(row <= col, BF16),
              jnp.asarray(np.where(row >= col, 0.0, NEG), F32),
              jnp.asarray(np.where(row <= col, 0.0, NEG), F32)]
    specs += [pl.BlockSpec((L, L), lambda h, j: (0, 0))] * len(consts)
    return pl.pallas_call(
        _ml_rec_kernel,
        out_shape=[jax.ShapeDtypeStruct((TOKENS, ML_WIDTH), BF16)] * 2,
        grid=(ML_HEADS // ML_REC_HEADS, nchunk),
        in_specs=specs,
        out_specs=[pl.BlockSpec((L, hw), lambda h, j: (fwd(j), h)),
                   pl.BlockSpec((L, hw), lambda h, j: (bwd(j), h))],
        scratch_shapes=state,
        compiler_params=_cparams("arbitrary", "arbitrary"),
    )(q, k, vt, gates_t, q, k, vt, gates_t, *consts)


def _ml_out_kernel(hf_ref, hb_ref, xc_ref, z_ref, ng_ref, skip_ref, w_ref, x_ref, gate_ref,
                   g2_ref, sh2_ref, sc2_ref, o_ref, h2_ref, *, tm):
    i = pl.program_id(0)
    kk = pl.program_id(1)
    h = hf_ref[...].astype(F32) + hb_ref[...].astype(F32)
    mu = jnp.mean(h, axis=-1, keepdims=True)
    hc = h - mu
    var = jnp.mean(hc * hc, axis=-1, keepdims=True)
    hn = hc * lax.rsqrt(var + EPS) * ng_ref[...]
    lhs = ((hn + skip_ref[...] * xc_ref[...].astype(F32)) * _silu(z_ref[...])).astype(BF16)
    part = _dot(lhs, w_ref[...])

    @pl.when(kk == 0)
    def _():
        o_ref[...] = part

    @pl.when(kk > 0)
    def _():
        o_ref[...] += part

    @pl.when(kk == pl.num_programs(1) - 1)
    def _():
        strip = 128

        def finish(r, carry):
            rows = pl.ds(pl.multiple_of(r * strip, strip), strip)
            row0 = i * tm + r * strip
            gate = _row_select(row0, strip, gate_ref[1:2, :], gate_ref[0:1, :])
            x = x_ref[rows, :] + gate * o_ref[rows, :]
            o_ref[rows, :] = x
            h2_ref[rows, :] = _norm_mod_rows(x, g2_ref, sh2_ref, sc2_ref, row0)
            return carry

        lax.fori_loop(0, tm // strip, finish, 0)


def ml_out(hf, hb, xconv, xz, norm_g, skip, w_out, xc, mods, next_g, next_mods, tm=768):
    e = ML_WIDTH
    dh = ML_HEAD_DIM
    d = D_MODEL
    nh = ML_HEADS
    hd = lambda i, k: (i, k)
    nargs, nspecs = _norm_specs(next_g, next_mods)
    row = pl.BlockSpec((tm, d), lambda i, k: (i, 0))
    return pl.pallas_call(
        functools.partial(_ml_out_kernel, tm=tm),
        out_shape=[jax.ShapeDtypeStruct((TOKENS, d), F32),
                   jax.ShapeDtypeStruct((TOKENS, d), BF16)],
        grid=(TOKENS // tm, nh),
        in_specs=[pl.BlockSpec((tm, dh), hd), pl.BlockSpec((tm, dh), hd),
                  pl.BlockSpec((tm, dh), hd),
                  pl.BlockSpec((tm, dh), lambda i, k: (i, nh + k)),
                  pl.BlockSpec((1, dh), lambda i, k: (0, k)),
                  pl.BlockSpec((1, dh), lambda i, k: (0, k)),
                  pl.BlockSpec((dh, d), lambda i, k: (k, 0)),
                  row,
                  pl.BlockSpec((8, d), lambda i, k: (0, 2))] + nspecs,
        out_specs=[row, row],
        compiler_params=_cparams("arbitrary", "arbitrary"),
    )(hf, hb, xconv, xz, norm_g.reshape(1, e), skip.reshape(1, e), w_out, xc, mods, *nargs)


def _cmul(ar, ai, br, bi):
    return ar * br - ai * bi, ar * bi + ai * br


def s5_operators(lam_re, lam_im, log_dt, b_re, b_im, c_re, c_im):
    G, P, GS, L = S5_GROUPS, S5_STATE, S5_GROUP, S5_L
    Q, GL = S5_TILES, S5_TILE_GROUPS
    bre, bim = b_re.astype(F32).transpose(0, 2, 1), b_im.astype(F32).transpose(0, 2, 1)
    cre, cim = c_re.astype(F32), c_im.astype(F32)
    tau = jnp.arange(L + 1, dtype=F32)[None, :, None]
    seg_pows = jnp.asarray([1, S5_SEG_CHUNKS], F32) * L
    dmaps, bcs, ccts, rows = [], [], [], []
    for j in range(2):
        lr, li = lam_re[j].astype(F32), lam_im[j].astype(F32)
        dt = jnp.exp(log_dt[j].astype(F32))[:, None]
        lrd, lid = (lr * dt)[:, None, :], (li * dt)[:, None, :]
        pr = jnp.exp(lrd * tau) * jnp.cos(lid * tau)
        pi = jnp.exp(lrd * tau) * jnp.sin(lid * tau)
        ar, ai = pr[:, 1], pi[:, 1]
        den = lr * lr + li * li
        kr = (((ar - 1.0) * lr + ai * li) / den)[:, None, :]
        ki = ((ai * lr - (ar - 1.0) * li) / den)[:, None, :]
        bbr = kr * bre - ki * bim
        bbi = kr * bim + ki * bre
        tq = tau[:, :L, None, :]
        lrq, liq = (lr * dt).reshape(Q, 1, GL, P), (li * dt).reshape(Q, 1, GL, P)
        prq = (jnp.exp(lrq * tq) * jnp.cos(liq * tq))[:, :, :, None, :]
        piq = (jnp.exp(lrq * tq) * jnp.sin(liq * tq))[:, :, :, None, :]
        creq, cimq = cre.reshape(Q, 1, GL, GS, P), cim.reshape(Q, 1, GL, GS, P)
        car = (creq * prq - cimq * piq)[:, :, :, None]
        cai = (creq * piq + cimq * prq)[:, :, :, None]
        bbrq = bbr.reshape(Q, 1, GL, GS, 1, P)
        bbiq = bbi.reshape(Q, 1, GL, GS, 1, P)
        dmaps.append(jnp.sum(car * bbrq - cai * bbiq, axis=-1))
        pw = np.arange(L - 1, -1, -1) if j == 0 else np.arange(L)
        er, ei = _cmul(pr[:, pw, None], pi[:, pw, None], bbr[:, None], bbi[:, None])
        bcs.append(jnp.concatenate([er, ei], -1).reshape(G, L * GS, 2 * P))
        pw = np.arange(1, L + 1) if j == 0 else np.arange(L, 0, -1)
        fr, fi = _cmul(cre[:, None], cim[:, None], pr[:, pw, None], pi[:, pw, None])
        ccts.append(jnp.concatenate([fr, -fi], -1).reshape(G, L * GS, 2 * P))
        sp = seg_pows[:, None, None]
        sr = jnp.exp(lr * dt * sp) * jnp.cos(li * dt * sp)
        si = jnp.exp(lr * dt * sp) * jnp.sin(li * dt * sp)
        a1 = jnp.concatenate([sr, sr], -1).reshape(2, 1, G * 2 * P)
        a2 = jnp.concatenate([-si, si], -1).reshape(2, 1, G * 2 * P)
        rows.append(jnp.stack([a1, a2], axis=1))
    df, db = dmaps
    ksum = jnp.concatenate([jnp.flip(db[:, 1:], axis=1), df[:, :1] + db[:, :1], df[:, 1:]], axis=1)
    lane = np.arange(LANES)
    spread = jnp.asarray(lane[None, :] % GS == np.arange(GS)[:, None], BF16)
    tiled = jnp.dot(ksum.astype(BF16).reshape(-1, GS), spread, preferred_element_type=BF16)
    same_group = jnp.asarray(lane[:, None] // GS == lane[None, :] // GS)
    dsum = jnp.where(same_group, tiled.reshape(Q, 2 * L - 1, LANES, LANES), jnp.zeros((), BF16))
    return dsum, jnp.stack(bcs).astype(BF16), jnp.stack(ccts).astype(BF16), jnp.stack(rows)


def _s5_chunk_cat(u_ref, dtype):
    n = u_ref.shape[0] // S5_L
    return jnp.concatenate([u_ref[pl.ds(s, n, stride=S5_L), :].astype(dtype)
                            for s in range(S5_L)], axis=-1)


def _s5_place_group_rows(dst_ref, src_ref, col0):
    for gl in range(S5_TILE_GROUPS):
        for s in range(S5_L):
            r = s * LANES + gl * S5_GROUP
            dst_ref[r:r + S5_GROUP, col0 + gl * LANES:col0 + (gl + 1) * LANES] = (
                src_ref[gl, s * S5_GROUP:(s + 1) * S5_GROUP, :])


def _s5_drive_kernel(u_ref, bcf_ref, bcb_ref, ef_ref, eb_ref, op_ref):
    first = (pl.program_id(0) == 0) & (pl.program_id(1) == 0)

    @pl.when(first)
    def _():
        op_ref[...] = jnp.zeros(op_ref.shape, BF16)

    @pl.when(pl.program_id(1) == 0)
    def _():
        _s5_place_group_rows(op_ref, bcf_ref, 0)
        _s5_place_group_rows(op_ref, bcb_ref, S5_TILE_STATE)

    e = _dot(_s5_chunk_cat(u_ref, BF16), op_ref[...])
    ef_ref[...] = e[:, :S5_TILE_STATE]
    eb_ref[...] = e[:, S5_TILE_STATE:]


def s5_drive(uz, bc):
    rows = TOKENS // S5_ROW_SPLIT
    out = jax.ShapeDtypeStruct((S5_CHUNKS, S5_GROUPS * 2 * S5_STATE), F32)
    ospec = pl.BlockSpec((rows // S5_L, S5_TILE_STATE), lambda q, i: (i, q))

    def bspec(d):
        return pl.BlockSpec((None, S5_TILE_GROUPS, S5_L * S5_GROUP, 2 * S5_STATE),
                            lambda q, i: (d, q, 0, 0))
    return pl.pallas_call(
        _s5_drive_kernel,
        out_shape=[out, out],
        grid=(S5_TILES, S5_ROW_SPLIT),
        in_specs=[pl.BlockSpec((rows, LANES), lambda q, i: (i, q)), bspec(0), bspec(1)],
        out_specs=[ospec, ospec],
        scratch_shapes=[pltpu.VMEM((S5_CAT, 2 * S5_TILE_STATE), BF16)],
        compiler_params=_cparams("arbitrary", "arbitrary"),
    )(uz, bc, bc)


def _s5_scan_kernel(*refs, reverse):
    ng = S5_TILE_GROUPS
    nj = S5_SEG_CHUNKS
    e_refs, a_ref, x_refs = refs[:ng], refs[ng], refs[ng + 1:2 * ng + 1]
    in_ref, es_ref = refs[-2:]
    lanes = [slice(g * LANES, (g + 1) * LANES) for g in range(ng)]
    swap = lambda z: pltpu.roll(z, S5_STATE, 1)
    for g in range(ng):
        es_ref[g] = swap(e_refs[g][...])
    a1 = [a_ref[0, 0][:, sl] for sl in lanes]
    a2 = [a_ref[0, 1][:, sl] for sl in lanes]
    seg0 = lax.broadcasted_iota(jnp.int32, (S5_SEG, LANES), 0) == 0

    def tile(j):
        jj = (nj - 1 - j) if reverse else j
        return jj, pl.ds(jj, S5_SEG, stride=nj)

    def step(j, carry, emit):
        jj, rows = tile(j)
        out = []
        for g in range(ng):
            z, zs = carry[2 * g], carry[2 * g + 1]
            if reverse:
                reset = seg0 & (jj == S5_CTX_CHUNKS - 1)
                z, zs = jnp.where(reset, 0.0, z), jnp.where(reset, 0.0, zs)
            if emit:
                x_refs[g][rows, :] = z
            out += [a1[g] * z + a2[g] * zs + e_refs[g][rows, :],
                    a1[g] * zs - a2[g] * z + es_ref[g, rows, :]]
        return tuple(out)

    zero = tuple(jnp.zeros((S5_SEG, LANES), F32) for _ in range(2 * ng))
    end = lax.fori_loop(0, nj, functools.partial(step, emit=False), zero, unroll=2)[0::2]
    order = range(S5_SEG - 1, -1, -1) if reverse else range(S5_SEG)
    for g, sl in enumerate(lanes):
        cur = end[g][0:1, :] if reverse else jnp.zeros((1, LANES), F32)
        for s in order:
            in_ref[s:s + 1, sl] = cur
            cur = a_ref[1, 0][:, sl] * cur + a_ref[1, 1][:, sl] * swap(cur) + end[g][s:s + 1, :]
    start = []
    for sl in lanes:
        start += [in_ref[:, sl], swap(in_ref[:, sl])]
    lax.fori_loop(0, nj, functools.partial(step, emit=True), tuple(start), unroll=2)


def s5_scan(e, rows, direction):
    ng = S5_TILE_GROUPS
    cols = S5_TILE_STATE
    blks = [pl.BlockSpec((S5_CHUNKS, LANES), lambda i, g=g: (0, ng * i + g)) for g in range(ng)]
    return pl.pallas_call(
        functools.partial(_s5_scan_kernel, reverse=direction == 1),
        out_shape=[jax.ShapeDtypeStruct((S5_CHUNKS, S5_TILES * LANES), F32)] * ng,
        grid=(S5_TILES,),
        in_specs=blks + [pl.BlockSpec((None, 2, 2, 1, cols),
                                      lambda i: (direction, 0, 0, 0, i))],
        out_specs=[pl.BlockSpec((S5_CHUNKS, LANES), lambda i: (0, i))] * ng,
        scratch_shapes=[pltpu.VMEM((S5_SEG, cols), F32),
                        pltpu.VMEM((ng, S5_CHUNKS, LANES), F32)],
        compiler_params=_cparams("arbitrary"),
    )(*([e] * ng), rows)


def _gelu_tanh(x):
    return 0.5 * x * (1.0 + jnp.tanh(math.sqrt(2.0 / math.pi) * (x + 0.044715 * (x * x * x))))


def _s5_read_kernel(u_ref, dsum_ref, cctf_ref, cctb_ref, d_ref, *refs):
    x_refs, (o_ref, m_ref, c_ref) = refs[:2 * S5_TILE_GROUPS], refs[2 * S5_TILE_GROUPS:]
    first = (pl.program_id(0) == 0) & (pl.program_id(1) == 0)

    @pl.when(first)
    def _():
        c_ref[...] = jnp.zeros(c_ref.shape, BF16)

    @pl.when(pl.program_id(1) == 0)
    def _():
        for s in range(S5_L):
            for t in range(S5_L):
                m_ref[s * LANES:(s + 1) * LANES, t * LANES:(t + 1) * LANES] = (
                    dsum_ref[t - s + S5_L - 1])
        _s5_place_group_rows(c_ref, cctf_ref, 0)
        _s5_place_group_rows(c_ref, cctb_ref, S5_TILE_STATE)

    n = u_ref.shape[0] // S5_L
    xs = jnp.concatenate([x[...].astype(BF16) for x in x_refs], axis=-1)
    y = _dot(_s5_chunk_cat(u_ref, BF16), m_ref[...]) + _dot_nt(xs, c_ref[...])
    for t in range(S5_L):
        rows = pl.ds(t, n, stride=S5_L)
        o_ref[rows, :] = _gelu_tanh(y[:, t * LANES:(t + 1) * LANES] + d_ref[...] * u_ref[rows, :])


def s5_read(uz, dsum, cct, xf, xb, d_skip):
    rows = TOKENS // S5_ROW_SPLIT
    ublk = pl.BlockSpec((rows, LANES), lambda q, i: (i, q))
    xblk = pl.BlockSpec((rows // S5_L, LANES), lambda q, i: (i, q))

    def cspec(d):
        return pl.BlockSpec((None, S5_TILE_GROUPS, S5_L * S5_GROUP, 2 * S5_STATE),
                            lambda q, i: (d, q, 0, 0))
    return pl.pallas_call(
        _s5_read_kernel,
        out_shape=jax.ShapeDtypeStruct((TOKENS, S5_WIDTH), F32),
        grid=(S5_TILES, S5_ROW_SPLIT),
        in_specs=[ublk,
                  pl.BlockSpec((None, 2 * S5_L - 1, LANES, LANES), lambda q, i: (q, 0, 0, 0)),
                  cspec(0), cspec(1),
                  pl.BlockSpec((1, LANES), lambda q, i: (0, q))] + [xblk] * (len(xf) + len(xb)),
        out_specs=ublk,
        scratch_shapes=[pltpu.VMEM((S5_CAT, S5_CAT), BF16),
                        pltpu.VMEM((S5_CAT, 2 * S5_TILE_STATE), BF16)],
        compiler_params=_cparams("arbitrary", "arbitrary"),
    )(uz, dsum, cct, cct, d_skip.astype(F32).reshape(1, S5_WIDTH), *xf, *xb)


def _glu_kernel(s_ref, w_ref, b_ref, o_ref, lhs_ref, *, tn):
    j = pl.program_id(1)

    @pl.when(j == 0)
    def _():
        lhs_ref[...] = s_ref[...].astype(BF16)

    v = _dot(lhs_ref[...], w_ref[...]) + b_ref[...]
    sj = s_ref[:, pl.ds(pl.multiple_of(j * tn, tn), tn)]
    o_ref[...] = sj * jax.nn.sigmoid(v)


def s5_glu(s, w_glu, b_glu, tm=768, tn=1024):
    d = S5_WIDTH
    return pl.pallas_call(
        functools.partial(_glu_kernel, tn=tn),
        out_shape=jax.ShapeDtypeStruct((TOKENS, d), F32),
        grid=(TOKENS // tm, d // tn),
        in_specs=[pl.BlockSpec((tm, d), lambda i, j: (i, 0)),
                  pl.BlockSpec((d, tn), lambda i, j: (0, j)),
                  pl.BlockSpec((1, tn), lambda i, j: (0, j))],
        out_specs=pl.BlockSpec((tm, tn), lambda i, j: (i, j)),
        scratch_shapes=[pltpu.VMEM((tm, d), BF16)],
        compiler_params=_cparams("arbitrary", "arbitrary"),
    )(s, w_glu, b_glu.reshape(1, d))


def _block_diag(w, size):
    lane = np.arange(size)
    spread = jnp.asarray(lane[None, :] % ML_BLOCK == np.arange(ML_BLOCK)[:, None], BF16)
    tiled = jnp.dot(w.astype(BF16).reshape(-1, ML_BLOCK), spread, preferred_element_type=BF16)
    same_block = jnp.asarray(lane[:, None] // ML_BLOCK == lane[None, :] // ML_BLOCK)
    return jnp.where(same_block, tiled.reshape(-1, size, size), jnp.zeros((), BF16))


def na_layer(xc, h, mods, tail, w_in, li, rpb, w_out):
    e = D_MODEL
    last = isinstance(tail[0], str)
    qkv = in_proj(h, w_in, li, 0, 3 * e, BF16, scaled_cols=e, scale=NA_Q_SCALE)
    z = in_proj(h, w_in, li, 3 * e, e, F32)
    o = na_attention(qkv, na_pair_table(rpb), with_ctx_out=not last)
    if last:
        return gated_out_final(o, z, 0, w_out.astype(BF16), xc, mods, tail[1])
    return gated_out(o, z, 0, w_out.astype(BF16), xc, mods, *tail)


def mlstm_layer(xc, h, mods, tail, w_in, li, conv_w, conv_b, wq, wk, wv, w_gate, b_gate, skip,
                norm_g, w_out):
    e = ML_WIDTH
    xz = in_proj(h, w_in, li, 0, 2 * e, F32)
    wq_bd = _block_diag(wq, ML_COLS)
    wk_bd = _block_diag(wk, ML_COLS)
    wv_bd = _block_diag(wv, ML_COLS)
    wvt_bd = _block_diag(wv.transpose(0, 2, 1), ML_COLS)
    wg_t = w_gate.reshape(3, e, 4 * ML_HEADS).transpose(0, 2, 1).astype(BF16)
    xconv, q, k, vt, gates_t = ml_pre(xz, conv_w, conv_b, wq_bd, wk_bd, wv_bd, wvt_bd, wg_t,
                                      b_gate)
    hf, hb = ml_recurrence(q, k, vt, gates_t)
    return ml_out(hf, hb, xconv, xz, norm_g, skip, w_out.astype(BF16), xc, mods, *tail)


def s5_layer(xc, h, mods, tail, w_in, li, lam_re, lam_im, log_dt, b_re, b_im, c_re, c_im,
             d_skip, w_glu, b_glu, w_out):
    e = S5_WIDTH
    uz = in_proj(h, w_in, li, 0, 2 * e, F32)
    dsum, bc, cct, rows = s5_operators(lam_re, lam_im, log_dt, b_re, b_im, c_re, c_im)
    ef, eb = s5_drive(uz, bc)
    xf = s5_scan(ef, rows, 0)
    xb = s5_scan(eb, rows, 1)
    s = s5_read(uz, dsum, cct, xf, xb, d_skip)
    s2 = s5_glu(s, w_glu.astype(BF16), b_glu)
    return gated_out(s2, uz, e, w_out.astype(BF16), xc, mods, *tail)


def kernel(x, c, ctx, c_ctx, norm_g, ada_w, ada_b, na_w_in, na_rpb, na_w_out, ml_w_in, ml_conv_w, ml_conv_b, ml_wq, ml_wk, ml_wv, ml_w_gate, ml_b_gate, ml_skip, ml_norm_g, ml_w_out, s5_w_in, s5_lam_re, s5_lam_im, s5_log_dt, s5_b_re, s5_b_im, s5_c_re, s5_c_im, s5_d, s5_w_glu, s5_b_glu, s5_w_out, final_g):
    xc = jnp.concatenate([ctx[0], x[0]], axis=0)
    c8 = jnp.concatenate([c, c_ctx[None, :], jnp.zeros((6, D_MODEL), F32)], axis=0)
    mods = ada_mods(c8, ada_w, ada_b)
    h = norm_mod(xc, norm_g[0], mods[0])
    ia = ib = ic = 0
    assert (DEPTH - 1) % 3 == 0, "the last layer must be the attention layer that ends the trunk"
    for layer in range(DEPTH):
        tail = ((norm_g[layer + 1], mods[layer + 1]) if layer + 1 < DEPTH
                else ('final', final_g))
        kind = layer % 3
        if kind == 0:
            out = na_layer(xc, h, mods[layer], tail, na_w_in, ia, na_rpb[ia], na_w_out[ia])
            ia += 1
        elif kind == 1:
            out = mlstm_layer(xc, h, mods[layer], tail, ml_w_in, ib, ml_conv_w[ib],
                              ml_conv_b[ib], ml_wq[ib], ml_wk[ib], ml_wv[ib], ml_w_gate[ib],
                              ml_b_gate[ib], ml_skip[ib], ml_norm_g[ib], ml_w_out[ib])
            ib += 1
        else:
            out = s5_layer(xc, h, mods[layer], tail, s5_w_in, ic, s5_lam_re[ic],
                           s5_lam_im[ic], s5_log_dt[ic], s5_b_re[ic], s5_b_im[ic], s5_c_re[ic],
                           s5_c_im[ic], s5_d[ic], s5_w_glu[ic], s5_b_glu[ic], s5_w_out[ic])
            ic += 1
        if layer + 1 < DEPTH:
            xc, h = out
    return out[None]
```

```python
import functools
import itertools
import math

import numpy as np
import jax
import jax.numpy as jnp
from jax import lax
from jax.experimental import pallas as pl
from jax.experimental.pallas import tpu as pltpu

F32 = jnp.float32
BF16 = jnp.bfloat16

D_MODEL = 2048
SEQ = 8192
CTX_LEN = 256
TOKENS = CTX_LEN + SEQ
DEPTH = 4
EPS = 1e-6
NEG = -1e30

GRID_W = 64
GRID_ROWS = SEQ // GRID_W
NA_HEADS = 16
NA_HEAD_DIM = 128
WIN_H = 8
WIN_W = 16
NA_QROWS = 4
NA_BLK = NA_QROWS * GRID_W
NA_WIN_BLKS = 3
NA_HEADS_PER_STEP = 8
LOG2E = math.log2(math.e)
NA_Q_SCALE = NA_HEAD_DIM ** -0.5 * LOG2E

ML_WIDTH = 2 * D_MODEL
ML_HEADS = 8
ML_HEAD_DIM = ML_WIDTH // ML_HEADS
ML_BLOCK = 4
ML_CHUNK = 256
ML_REC_HEADS = 2
ML_COLS = 512
ML_DIAG = 256
ML_STRIP = 256

S5_WIDTH = D_MODEL
S5_GROUP = 16
S5_GROUPS = S5_WIDTH // S5_GROUP
S5_STATE = 64
S5_L = 16
S5_SEG = 8
S5_CHUNKS = TOKENS // S5_L
S5_CTX_CHUNKS = CTX_LEN // S5_L
S5_SEG_CHUNKS = S5_CHUNKS // S5_SEG
LANES = 128
S5_TILE_GROUPS = LANES // S5_GROUP
S5_TILES = S5_WIDTH // LANES
S5_CAT = S5_L * LANES
S5_TILE_STATE = S5_TILE_GROUPS * 2 * S5_STATE
S5_ROW_SPLIT = 2

VMEM_LIMIT = 56 * 1024 * 1024


def _cparams(*sem):
    return pltpu.CompilerParams(dimension_semantics=sem, vmem_limit_bytes=VMEM_LIMIT)


def _dot(a, b):
    return jnp.dot(a, b, preferred_element_type=F32)


def _dot_nt(a, b):
    return lax.dot_general(a, b, (((1,), (1,)), ((), ())), preferred_element_type=F32)


def _silu(x):
    return x * jax.nn.sigmoid(x)


def _row_select(row0, n, ctx_row, x_row):
    rows = row0 + lax.broadcasted_iota(jnp.int32, (n, 1), 0)
    return jnp.where(rows < CTX_LEN, ctx_row, x_row)


def _ada_kernel(c_ref, w_ref, b_ref, o_ref):
    cs = _silu(c_ref[...]).astype(BF16)
    o_ref[0] = _dot(cs, w_ref[0].astype(BF16)) + b_ref[0]


def ada_mods(c8, ada_w, ada_b):
    tn = 512
    n = 3 * D_MODEL
    return pl.pallas_call(
        _ada_kernel,
        out_shape=jax.ShapeDtypeStruct((DEPTH, 8, n), F32),
        grid=(DEPTH, n // tn),
        in_specs=[pl.BlockSpec((8, D_MODEL), lambda l, j: (0, 0)),
                  pl.BlockSpec((1, D_MODEL, tn), lambda l, j: (l, 0, j)),
                  pl.BlockSpec((1, 1, tn), lambda l, j: (l, 0, j))],
        out_specs=pl.BlockSpec((1, 8, tn), lambda l, j: (l, 0, j)),
        compiler_params=_cparams("arbitrary", "arbitrary"),
    )(c8, ada_w, ada_b.reshape(DEPTH, 1, n))


def _rmsnorm(x, g):
    return x * lax.rsqrt(jnp.mean(x * x, axis=-1, keepdims=True) + EPS) * g


def _norm_mod_rows(x, g_ref, sh_ref, sc_ref, row0):
    sc = _row_select(row0, x.shape[0], sc_ref[1:2, :], sc_ref[0:1, :])
    sh = _row_select(row0, x.shape[0], sh_ref[1:2, :], sh_ref[0:1, :])
    return (_rmsnorm(x, g_ref[...]) * (1.0 + sc) + sh).astype(BF16)


def _norm_specs(g, mods):
    d = D_MODEL
    return ([g.reshape(1, d), mods, mods],
            [pl.BlockSpec((1, d), lambda *a: (0, 0)), pl.BlockSpec((8, d), lambda *a: (0, 0)),
             pl.BlockSpec((8, d), lambda *a: (0, 1))])


def _norm_mod_kernel(x_ref, g_ref, sh_ref, sc_ref, o_ref, *, tm):
    o_ref[...] = _norm_mod_rows(x_ref[...], g_ref, sh_ref, sc_ref, pl.program_id(0) * tm)


def norm_mod(xc, g, mods, tm=384):
    d = D_MODEL
    nargs, nspecs = _norm_specs(g, mods)
    return pl.pallas_call(
        functools.partial(_norm_mod_kernel, tm=tm),
        out_shape=jax.ShapeDtypeStruct((TOKENS, d), BF16),
        grid=(TOKENS // tm,),
        in_specs=[pl.BlockSpec((tm, d), lambda i: (i, 0))] + nspecs,
        out_specs=pl.BlockSpec((tm, d), lambda i: (i, 0)),
        compiler_params=_cparams("arbitrary"),
    )(xc, *nargs)


def _in_proj_kernel(h_ref, w_ref, o_ref, wb_ref, *, scaled_blocks, scale):
    j = pl.program_id(0)

    @pl.when(pl.program_id(1) == 0)
    def _():
        wb_ref[...] = w_ref[...].astype(BF16)

    acc = _dot(h_ref[...], wb_ref[...])
    if scaled_blocks:
        acc = acc * jnp.where(j < scaled_blocks, scale, 1.0)
    o_ref[...] = acc.astype(o_ref.dtype)


def in_proj(h, w, layer, col0, ncols, out_dtype, scaled_cols=0, scale=1.0, tm=768, tn=1024):
    d = D_MODEL
    cb = col0 // tn
    return pl.pallas_call(
        functools.partial(_in_proj_kernel, scaled_blocks=scaled_cols // tn, scale=scale),
        out_shape=jax.ShapeDtypeStruct((TOKENS, ncols), out_dtype),
        grid=(ncols // tn, TOKENS // tm),
        in_specs=[pl.BlockSpec((tm, d), lambda j, i: (i, 0)),
                  pl.BlockSpec((None, d, tn), lambda j, i: (layer, 0, cb + j))],
        out_specs=pl.BlockSpec((tm, tn), lambda j, i: (i, j)),
        scratch_shapes=[pltpu.VMEM((d, tn), BF16)],
        compiler_params=_cparams("arbitrary", "arbitrary"),
    )(h, w)


def _gated_out_kernel(a_ref, z_ref, w_ref, x_ref, gate_ref, g_ref, sh_ref, sc_ref,
                      o_ref, h_ref, *, tm):
    i = pl.program_id(0)
    lhs = (a_ref[...].astype(F32) * _silu(z_ref[...])).astype(BF16)
    gate = _row_select(i * tm, tm, gate_ref[1:2, :], gate_ref[0:1, :])
    x = x_ref[...] + gate * _dot(lhs, w_ref[...])
    o_ref[...] = x
    h_ref[...] = _norm_mod_rows(x, g_ref, sh_ref, sc_ref, i * tm)


def gated_out(a, zsrc, zcol0, w, xc, mods, next_g, next_mods, tm=384):
    k = a.shape[1]
    d = D_MODEL
    zb = zcol0 // k
    nargs, nspecs = _norm_specs(next_g, next_mods)
    row = pl.BlockSpec((tm, d), lambda i: (i, 0))
    return pl.pallas_call(
        functools.partial(_gated_out_kernel, tm=tm),
        out_shape=[jax.ShapeDtypeStruct((TOKENS, d), F32),
                   jax.ShapeDtypeStruct((TOKENS, d), BF16)],
        grid=(TOKENS // tm,),
        in_specs=[pl.BlockSpec((tm, k), lambda i: (i, 0)),
                  pl.BlockSpec((tm, k), lambda i: (i, zb)),
                  pl.BlockSpec((k, d), lambda i: (0, 0)),
                  row,
                  pl.BlockSpec((8, d), lambda i: (0, 2))] + nspecs,
        out_specs=[row, row],
        compiler_params=_cparams("arbitrary"),
    )(a, zsrc, w, xc, mods, *nargs)


def _gated_out_final_kernel(a_ref, z_ref, w_ref, x_ref, gate_ref, g_ref, o_ref):
    lhs = (a_ref[...].astype(F32) * _silu(z_ref[...])).astype(BF16)
    x = x_ref[...] + gate_ref[0:1, :] * _dot(lhs, w_ref[...])
    o_ref[...] = _rmsnorm(x, g_ref[...])


def gated_out_final(a, zsrc, zcol0, w, xc, mods, final_g, tm=256):
    k = a.shape[1]
    d = D_MODEL
    zb = zcol0 // k
    off = CTX_LEN // tm
    return pl.pallas_call(
        _gated_out_final_kernel,
        out_shape=jax.ShapeDtypeStruct((SEQ, d), F32),
        grid=(SEQ // tm,),
        in_specs=[pl.BlockSpec((tm, k), lambda i: (i + off, 0)),
                  pl.BlockSpec((tm, k), lambda i: (i + off, zb)),
                  pl.BlockSpec((k, d), lambda i: (0, 0)),
                  pl.BlockSpec((tm, d), lambda i: (i + off, 0)),
                  pl.BlockSpec((8, d), lambda i: (0, 2)),
                  pl.BlockSpec((1, d), lambda i: (0, 0))],
        out_specs=pl.BlockSpec((tm, d), lambda i: (i, 0)),
        compiler_params=_cparams("arbitrary"),
    )(a, zsrc, w, xc, mods, final_g.reshape(1, d))


NA_REL_ROWS = 2 * WIN_H - 1
NA_A_LO = WIN_H - 1 - WIN_H // 2
NA_A_HI = NA_A_LO + WIN_H - 1
NA_PAIR_BOTH = NA_REL_ROWS - 1
NA_PAIR_LEFT_MASKED = NA_PAIR_BOTH
NA_PAIR_RIGHT_MASKED = NA_PAIR_BOTH + 1
NA_PAIR_MASKED = NA_PAIR_BOTH + 2
NA_PAIR_ENTRIES = NA_PAIR_BOTH + 3
NA_BAND_ROWS = NA_WIN_BLKS * NA_QROWS


def _na_pair_plan():
    nblk = GRID_ROWS // NA_QROWS
    plan = np.full((4, NA_QROWS, NA_BAND_ROWS // 2), NA_PAIR_MASKED, np.int32)
    for kind, g in ((0, 0), (1, 1), (2, nblk - 1)):
        band0 = NA_QROWS * int(np.clip(g - 1, 0, nblk - NA_WIN_BLKS))
        for i in range(NA_QROWS):
            rq = NA_QROWS * g + i
            r0 = int(np.clip(rq - WIN_H // 2, 0, GRID_ROWS - WIN_H))
            rel = [rk - rq + WIN_H - 1 if r0 <= rk < r0 + WIN_H else None
                   for rk in range(band0, band0 + NA_BAND_ROWS)]
            for p in range(NA_BAND_ROWS // 2):
                lo, hi = rel[2 * p], rel[2 * p + 1]
                if lo is not None and hi is not None:
                    plan[kind, i, p] = lo
                elif hi is not None:
                    assert hi == NA_A_LO
                    plan[kind, i, p] = NA_PAIR_LEFT_MASKED
                elif lo is not None:
                    assert lo == NA_A_HI
                    plan[kind, i, p] = NA_PAIR_RIGHT_MASKED
    return plan


def na_pair_table(rpb):
    cq = np.arange(GRID_W)[:, None]
    ck = np.arange(GRID_W)[None, :]
    c0 = np.clip(cq - WIN_W // 2, 0, GRID_W - WIN_W)
    ok = (ck >= c0) & (ck < c0 + WIN_W)
    rp = jnp.pad(rpb.astype(F32), ((0, 0), (0, 0), (GRID_W, GRID_W)))
    t = jnp.stack([rp[:, :, GRID_W + WIN_W - 1 - c:2 * GRID_W + WIN_W - 1 - c]
                   for c in range(GRID_W)], axis=2)
    t = jnp.where(jnp.asarray(ok), t * LOG2E, NEG)
    masked = jnp.full((NA_HEADS, 1, GRID_W, GRID_W), NEG, F32)
    left = jnp.concatenate([t[:, :-1], masked, t[:, NA_A_HI:NA_A_HI + 1], masked], axis=1)
    right = jnp.concatenate([t[:, 1:], t[:, NA_A_LO:NA_A_LO + 1], masked, masked], axis=1)
    return jnp.concatenate([left, right], axis=-1)


def _na_kernel(q_ref, kc_ref, k0_ref, k1_ref, k2_ref, vc_ref, v0_ref, v1_ref, v2_ref,
               pair_ref, o_ref, bias_ref, *, heads, q0, nq):
    qb = pl.program_id(1) + q0
    plan = _na_pair_plan()
    pw = 2 * GRID_W

    for kind, first in ((3, 0), (0, 1), (1, 2), (2, nq - 1)):
        @pl.when(qb == first)
        def _(kind=kind):
            for h in range(heads):
                for i in range(NA_QROWS):
                    for p in range(NA_BAND_ROWS // 2):
                        bias_ref[h, i * GRID_W:(i + 1) * GRID_W, p * pw:(p + 1) * pw] = (
                            pair_ref[h, int(plan[kind, i, p])])

    for h in range(heads):
        sl = slice(h * NA_HEAD_DIM, (h + 1) * NA_HEAD_DIM)
        q = q_ref[:, sl]
        s = [_dot_nt(q, kc_ref[:, sl])]
        for b, k_ref in enumerate((k0_ref, k1_ref, k2_ref)):
            s.append(_dot_nt(q, k_ref[:, sl]) + bias_ref[h, :, b * NA_BLK:(b + 1) * NA_BLK])
        m = s[0].max(axis=-1, keepdims=True)
        for t in s[1:]:
            m = jnp.maximum(m, t.max(axis=-1, keepdims=True))
        acc = None
        den = None
        for t, v_ref in zip(s, (vc_ref, v0_ref, v1_ref, v2_ref)):
            p = jnp.exp2(t - m)
            l = p.sum(axis=-1, keepdims=True)
            pv = _dot(p.astype(BF16), v_ref[:, sl])
            acc = pv if acc is None else acc + pv
            den = l if den is None else den + l
        o_ref[:, sl] = acc / den


def na_attention(qkv, pair_tab, with_ctx_out):
    hs = NA_HEADS_PER_STEP
    hw = hs * NA_HEAD_DIM
    nh = NA_HEADS // hs
    nq = TOKENS // NA_BLK
    q0 = 0 if with_ctx_out else 1
    nlat = GRID_ROWS // NA_QROWS

    def band(qb):
        return 1 + jnp.clip(qb - 2, 0, nlat - NA_WIN_BLKS)

    def spec(col_base, fn):
        return pl.BlockSpec((NA_BLK, hw), lambda h, g: (fn(g + q0), col_base + h))

    kspecs = [spec(nh, lambda qb: 0 * qb)] + [
        spec(nh, lambda qb, b=b: band(qb) + b) for b in range(NA_WIN_BLKS)]
    vspecs = [spec(2 * nh, lambda qb: 0 * qb)] + [
        spec(2 * nh, lambda qb, b=b: band(qb) + b) for b in range(NA_WIN_BLKS)]
    return pl.pallas_call(
        functools.partial(_na_kernel, heads=hs, q0=q0, nq=nq),
        out_shape=jax.ShapeDtypeStruct((TOKENS, D_MODEL), F32),
        grid=(nh, nq - q0),
        in_specs=[spec(0, lambda qb: qb)] + kspecs + vspecs + [
            pl.BlockSpec((hs, NA_PAIR_ENTRIES, GRID_W, 2 * GRID_W),
                         lambda h, g: (h, 0, 0, 0))],
        out_specs=pl.BlockSpec((NA_BLK, hw), lambda h, g: (g + q0, h)),
        scratch_shapes=[pltpu.VMEM((hs, NA_BLK, NA_WIN_BLKS * NA_BLK), F32)],
        compiler_params=_cparams("arbitrary", "arbitrary"),
    )(*([qkv] * 9), pair_tab)


def _ml_pre_kernel(x_ref, prev_ref, next_ref, cw_ref, cb_ref, wq_ref, wk_ref, wv_ref, wvt_ref,
                   wg_ref, bg_ref, xc_ref, q_ref, k_ref, vt_ref, g_ref, *, tm):
    i = pl.program_id(0)
    j = pl.program_id(1)
    x = x_ref[...]
    rows = i * tm + lax.broadcasted_iota(jnp.int32, (tm, 1), 0)
    local = lax.broadcasted_iota(jnp.int32, (tm, 1), 0)
    x_prev = jnp.where(local == 0, prev_ref[7:8, :], pltpu.roll(x, 1, 0))
    x_prev = jnp.where((rows == 0) | (rows == CTX_LEN), 0.0, x_prev)
    x_next = jnp.where(local == tm - 1, next_ref[0:1, :], pltpu.roll(x, tm - 1, 0))
    x_next = jnp.where((rows == CTX_LEN - 1) | (rows == TOKENS - 1), 0.0, x_next)
    @pl.when(j == 0)
    def _():
        g_ref[...] = jnp.broadcast_to(bg_ref[...], g_ref.shape)

    def activate(r):
        rs = slice(r * ML_STRIP, (r + 1) * ML_STRIP)
        conv = (cb_ref[...] + x_prev[rs] * cw_ref[0:1, :] + x[rs] * cw_ref[1:2, :]
                + x_next[rs] * cw_ref[2:3, :])
        xcb = _silu(conv).astype(BF16)
        xc_ref[rs, :] = xcb
        return xcb, x[rs].astype(BF16)

    def project(r, xcb, xb):
        rs = slice(r * ML_STRIP, (r + 1) * ML_STRIP)
        part = None
        for t in range(ML_COLS // ML_DIAG):
            cs = slice(t * ML_DIAG, (t + 1) * ML_DIAG)
            q = _dot(xcb[:, cs], wq_ref[t])
            k = _dot(xcb[:, cs], wk_ref[t])
            v = _dot(xb[:, cs], wv_ref[t])
            qb, kb, vb = q.astype(BF16), k.astype(BF16), v.astype(BF16)
            q_ref[rs, cs] = qb
            k_ref[rs, cs] = (k * (ML_HEAD_DIM ** -0.5)).astype(BF16)
            vt_ref[cs, rs] = _dot_nt(wvt_ref[t], xb[:, cs]).astype(BF16)
            gt = (_dot_nt(wg_ref[0, :, cs], qb) + _dot_nt(wg_ref[1, :, cs], kb)
                  + _dot_nt(wg_ref[2, :, cs], vb))
            part = gt if part is None else part + gt
        g_ref[:, rs] += part

    pending = activate(0)
    for r in range(tm // ML_STRIP):
        nxt = activate(r + 1) if (r + 1) * ML_STRIP < tm else None
        project(r, *pending)
        pending = nxt


def ml_pre(xz, conv_w, conv_b, wq_bd, wk_bd, wv_bd, wvt_bd, wg_t, b_gate, tm=768):
    e = ML_WIDTH
    c = ML_COLS
    nc = e // c
    last8 = TOKENS // 8 - 1
    tok = lambda i, j: (i, j)
    outs = pl.pallas_call(
        functools.partial(_ml_pre_kernel, tm=tm),
        out_shape=[jax.ShapeDtypeStruct((TOKENS, e), BF16),
                   jax.ShapeDtypeStruct((TOKENS, e), BF16),
                   jax.ShapeDtypeStruct((TOKENS, e), BF16),
                   jax.ShapeDtypeStruct((e, TOKENS), BF16),
                   jax.ShapeDtypeStruct((4 * ML_HEADS, TOKENS), F32)],
        grid=(TOKENS // tm, nc),
        in_specs=[pl.BlockSpec((tm, c), tok),
                  pl.BlockSpec((8, c), lambda i, j: (jnp.maximum(i * (tm // 8) - 1, 0), j)),
                  pl.BlockSpec((8, c), lambda i, j: (jnp.minimum((i + 1) * (tm // 8), last8), j)),
                  pl.BlockSpec((3, c), lambda i, j: (0, j)),
                  pl.BlockSpec((1, c), lambda i, j: (0, j)),
                  pl.BlockSpec((c // ML_DIAG, ML_DIAG, ML_DIAG), lambda i, j: (j, 0, 0)),
                  pl.BlockSpec((c // ML_DIAG, ML_DIAG, ML_DIAG), lambda i, j: (j, 0, 0)),
                  pl.BlockSpec((c // ML_DIAG, ML_DIAG, ML_DIAG), lambda i, j: (j, 0, 0)),
                  pl.BlockSpec((c // ML_DIAG, ML_DIAG, ML_DIAG), lambda i, j: (j, 0, 0)),
                  pl.BlockSpec((3, 4 * ML_HEADS, c), lambda i, j: (0, 0, j)),
                  pl.BlockSpec((4 * ML_HEADS, 1), lambda i, j: (0, 0))],
        out_specs=[pl.BlockSpec((tm, c), tok), pl.BlockSpec((tm, c), tok),
                   pl.BlockSpec((tm, c), tok),
                   pl.BlockSpec((c, tm), lambda i, j: (j, i)),
                   pl.BlockSpec((4 * ML_HEADS, tm), lambda i, j: (0, i))],
        compiler_params=_cparams("arbitrary", "arbitrary"),
    )(xz, xz, xz, conv_w, conv_b.reshape(1, e), wq_bd, wk_bd, wv_bd, wvt_bd, wg_t,
      b_gate.reshape(4 * ML_HEADS, 1))
    return outs


def _split3(x):
    hi = x.astype(BF16).astype(F32)
    r1 = x - hi
    mid = r1.astype(BF16).astype(F32)
    lo = (r1 - mid).astype(BF16).astype(F32)
    return hi, mid, lo


def _ml_step(q_ref, k_ref, vt_ref, i_row, f_row, upto_ref, upto_t_ref, mask_ref,
             c_ref, n_ref, m_ref, h_ref, reverse):
    L = ML_CHUNK
    logf = jax.nn.log_sigmoid(f_row)
    hi, mid, lo = _split3(logf)
    rid = lax.broadcasted_iota(jnp.int32, (16, L), 0)
    r16 = jnp.where(rid == 0, hi, jnp.where(rid == 1, mid, jnp.where(rid == 2, lo, 0.0)))
    r16 = r16.astype(BF16)
    rsum = _dot(r16, upto_ref[...])
    b_row = rsum[0:1] + rsum[1:2] + rsum[2:3]
    csum = _dot_nt(upto_t_ref[...], r16)
    b_col = csum[:, 0:1] + csum[:, 1:2] + csum[:, 2:3]
    m_prev = m_ref[...]
    a_col = b_col + m_prev
    yield
    dmat = (b_col - b_row + i_row) + mask_ref[...]
    m_t = jnp.maximum(a_col, dmat.max(axis=-1, keepdims=True))
    w_inter = jnp.exp(a_col - m_t)
    yield
    q, k, vt = q_ref[...], k_ref[...], vt_ref[...]
    s = _dot_nt(q, k) * jnp.exp(dmat - m_t)
    yield
    num = w_inter * _dot_nt(q, c_ref[...].astype(BF16)) + _dot_nt(s.astype(BF16), vt)
    qn = _dot_nt(q, n_ref[...].astype(BF16))[:, 0:1]
    den = w_inter * qn + s.sum(axis=-1, keepdims=True)
    h_ref[...] = (num / jnp.maximum(jnp.abs(den), jnp.exp(-m_t))).astype(h_ref.dtype)
    yield
    b_end = b_row[:, 0:1] if reverse else b_row[:, L - 1:L]
    g_row = b_end - b_row + i_row
    m_new = jnp.maximum(b_end + m_prev, g_row.max(axis=-1, keepdims=True))
    decay = jnp.exp(b_end + m_prev - m_new)
    w_row = jnp.exp(g_row - m_new)
    c_ref[...] = decay * c_ref[...] + _dot((vt.astype(F32) * w_row).astype(BF16), k)
    w8 = jnp.broadcast_to(w_row, (8, L)).astype(BF16)
    n_ref[...] = decay * n_ref[...] + _dot(w8, k)
    m_ref[...] = m_new


def _ml_rec_kernel(qf_ref, kf_ref, vtf_ref, gf_ref, qb_ref, kb_ref, vtb_ref, gb_ref,
                   lo_ref, up_ref, mlo_ref, mup_ref,
                   hf_ref, hb_ref, c_ref, n_ref, m_ref):
    @pl.when(pl.program_id(1) == 0)
    def _():
        for r in (c_ref, n_ref, m_ref):
            r[...] = jnp.zeros(r.shape, F32)

    nh = ML_HEADS
    dh = ML_HEAD_DIM
    steps = []
    for hh in range(ML_REC_HEADS):
        h = pl.program_id(0) * ML_REC_HEADS + hh
        cols = pl.ds(hh * dh, dh)
        steps += [
            _ml_step(qf_ref.at[:, cols], kf_ref.at[:, cols], vtf_ref.at[cols, :],
                     gf_ref[pl.ds(h, 1), :], gf_ref[pl.ds(nh + h, 1), :],
                     up_ref, lo_ref, mlo_ref, c_ref.at[2 * hh], n_ref.at[2 * hh],
                     m_ref.at[2 * hh], hf_ref.at[:, cols], reverse=False),
            _ml_step(qb_ref.at[:, cols], kb_ref.at[:, cols], vtb_ref.at[cols, :],
                     gb_ref[pl.ds(2 * nh + h, 1), :], gb_ref[pl.ds(3 * nh + h, 1), :],
                     lo_ref, up_ref, mup_ref, c_ref.at[2 * hh + 1], n_ref.at[2 * hh + 1],
                     m_ref.at[2 * hh + 1], hb_ref.at[:, cols], reverse=True)]
    for _ in itertools.zip_longest(*steps):
        pass


def ml_recurrence(q, k, vt, gates_t):
    L = ML_CHUNK
    dh = ML_HEAD_DIM
    nchunk = TOKENS // L
    fwd = lambda j: j
    bwd = lambda j: jnp.where(j == 0, 0, nchunk - j)
    specs = []
    hw = ML_REC_HEADS * dh
    for cm in (fwd, bwd):
        specs += [pl.BlockSpec((L, hw), lambda h, j, cm=cm: (cm(j), h)),
                  pl.BlockSpec((L, hw), lambda h, j, cm=cm: (cm(j), h)),
                  pl.BlockSpec((hw, L), lambda h, j, cm=cm: (h, cm(j))),
                  pl.BlockSpec((4 * ML_HEADS, L), lambda h, j, cm=cm: (0, cm(j)))]
    nstate = 2 * ML_REC_HEADS
    state = [pltpu.VMEM((nstate, dh, dh), F32), pltpu.VMEM((nstate, 8, dh), F32),
             pltpu.VMEM((nstate, 1, 1), F32)]
    row = np.arange(L)[:, None]
    col = np.arange(L)[None, :]
    consts = [jnp.asarray(row >= col, BF16), jnp.asarray(row <= col, BF16),
              jnp.asarray(np.where(row >= col, 0.0, NEG), F32),
              jnp.asarray(np.where(row <= col, 0.0, NEG), F32)]
    specs += [pl.BlockSpec((L, L), lambda h, j: (0, 0))] * len(consts)
    return pl.pallas_call(
        _ml_rec_kernel,
        out_shape=[jax.ShapeDtypeStruct((TOKENS, ML_WIDTH), BF16)] * 2,
        grid=(ML_HEADS // ML_REC_HEADS, nchunk),
        in_specs=specs,
        out_specs=[pl.BlockSpec((L, hw), lambda h, j: (fwd(j), h)),
                   pl.BlockSpec((L, hw), lambda h, j: (bwd(j), h))],
        scratch_shapes=state,
        compiler_params=_cparams("arbitrary", "arbitrary"),
    )(q, k, vt, gates_t, q, k, vt, gates_t, *consts)


def _ml_out_kernel(hf_ref, hb_ref, xc_ref, z_ref, ng_ref, skip_ref, w_ref, x_ref, gate_ref,
                   g2_ref, sh2_ref, sc2_ref, o_ref, h2_ref, *, tm):
    i = pl.program_id(0)
    kk = pl.program_id(1)
    @pl.when(kk == 0)
    def _():
        o_ref[...] = jnp.zeros(o_ref.shape, F32)

    def readout(r):
        rs = slice(r * ML_STRIP, (r + 1) * ML_STRIP)
        h = hf_ref[rs, :].astype(F32) + hb_ref[rs, :].astype(F32)
        mu = jnp.mean(h, axis=-1, keepdims=True)
        hc = h - mu
        var = jnp.mean(hc * hc, axis=-1, keepdims=True)
        hn = hc * lax.rsqrt(var + EPS) * ng_ref[...]
        return ((hn + skip_ref[...] * xc_ref[rs, :].astype(F32)) * _silu(z_ref[rs, :])).astype(BF16)

    pending = readout(0)
    for r in range(tm // ML_STRIP):
        nxt = readout(r + 1) if (r + 1) * ML_STRIP < tm else None
        o_ref[r * ML_STRIP:(r + 1) * ML_STRIP, :] += _dot(pending, w_ref[...])
        pending = nxt

    @pl.when(kk == pl.num_programs(1) - 1)
    def _():
        strip = 128

        def finish(r, carry):
            rows = pl.ds(pl.multiple_of(r * strip, strip), strip)
            row0 = i * tm + r * strip
            gate = _row_select(row0, strip, gate_ref[1:2, :], gate_ref[0:1, :])
            x = x_ref[rows, :] + gate * o_ref[rows, :]
            o_ref[rows, :] = x
            h2_ref[rows, :] = _norm_mod_rows(x, g2_ref, sh2_ref, sc2_ref, row0)
            return carry

        lax.fori_loop(0, tm // strip, finish, 0)


def ml_out(hf, hb, xconv, xz, norm_g, skip, w_out, xc, mods, next_g, next_mods, tm=768):
    e = ML_WIDTH
    dh = ML_HEAD_DIM
    d = D_MODEL
    nh = ML_HEADS
    hd = lambda i, k: (i, k)
    nargs, nspecs = _norm_specs(next_g, next_mods)
    row = pl.BlockSpec((tm, d), lambda i, k: (i, 0))
    return pl.pallas_call(
        functools.partial(_ml_out_kernel, tm=tm),
        out_shape=[jax.ShapeDtypeStruct((TOKENS, d), F32),
                   jax.ShapeDtypeStruct((TOKENS, d), BF16)],
        grid=(TOKENS // tm, nh),
        in_specs=[pl.BlockSpec((tm, dh), hd), pl.BlockSpec((tm, dh), hd),
                  pl.BlockSpec((tm, dh), hd),
                  pl.BlockSpec((tm, dh), lambda i, k: (i, nh + k)),
                  pl.BlockSpec((1, dh), lambda i, k: (0, k)),
                  pl.BlockSpec((1, dh), lambda i, k: (0, k)),
                  pl.BlockSpec((dh, d), lambda i, k: (k, 0)),
                  row,
                  pl.BlockSpec((8, d), lambda i, k: (0, 2))] + nspecs,
        out_specs=[row, row],
        compiler_params=_cparams("arbitrary", "arbitrary"),
    )(hf, hb, xconv, xz, norm_g.reshape(1, e), skip.reshape(1, e), w_out, xc, mods, *nargs)


def _cmul(ar, ai, br, bi):
    return ar * br - ai * bi, ar * bi + ai * br


def s5_operators(lam_re, lam_im, log_dt, b_re, b_im, c_re, c_im):
    G, P, GS, L = S5_GROUPS, S5_STATE, S5_GROUP, S5_L
    bre, bim = b_re.astype(F32).transpose(0, 2, 1), b_im.astype(F32).transpose(0, 2, 1)
    cre, cim = c_re.astype(F32), c_im.astype(F32)
    tau = jnp.arange(L + 1, dtype=F32)[None, :, None]
    seg_pows = jnp.asarray([1, S5_SEG_CHUNKS], F32) * L
    bcs, ccts, rows = [], [], []
    for j in range(2):
        lr, li = lam_re[j].astype(F32), lam_im[j].astype(F32)
        dt = jnp.exp(log_dt[j].astype(F32))[:, None]
        lrd, lid = (lr * dt)[:, None, :], (li * dt)[:, None, :]
        pr = jnp.exp(lrd * tau) * jnp.cos(lid * tau)
        pi = jnp.exp(lrd * tau) * jnp.sin(lid * tau)
        ar, ai = pr[:, 1], pi[:, 1]
        den = lr * lr + li * li
        kr = (((ar - 1.0) * lr + ai * li) / den)[:, None, :]
        ki = ((ai * lr - (ar - 1.0) * li) / den)[:, None, :]
        bbr = kr * bre - ki * bim
        bbi = kr * bim + ki * bre
        pw = np.arange(L - 1, -1, -1) if j == 0 else np.arange(L)
        er, ei = _cmul(pr[:, pw, None], pi[:, pw, None], bbr[:, None], bbi[:, None])
        bcs.append(jnp.concatenate([er, ei], -1).reshape(G, L * GS, 2 * P))
        pw = np.arange(1, L + 1) if j == 0 else np.arange(L, 0, -1)
        fr, fi = _cmul(cre[:, None], cim[:, None], pr[:, pw, None], pi[:, pw, None])
        ccts.append(jnp.concatenate([fr, -fi], -1).reshape(G, L * GS, 2 * P))
        sp = seg_pows[:, None, None]
        sr = jnp.exp(lr * dt * sp) * jnp.cos(li * dt * sp)
        si = jnp.exp(lr * dt * sp) * jnp.sin(li * dt * sp)
        a1 = jnp.concatenate([sr, sr], -1).reshape(2, 1, G * 2 * P)
        a2 = jnp.concatenate([-si, si], -1).reshape(2, 1, G * 2 * P)
        rows.append(jnp.stack([a1, a2], axis=1))
    c0 = jnp.concatenate([cre, -cim], -1).astype(BF16)
    return jnp.stack(bcs).astype(BF16), c0, jnp.stack(ccts).astype(BF16), jnp.stack(rows)


def _s5_chunk_cat(u_ref, dtype):
    n = u_ref.shape[0] // S5_L
    return jnp.concatenate([u_ref[pl.ds(s, n, stride=S5_L), :].astype(dtype)
                            for s in range(S5_L)], axis=-1)


def _s5_place_group_rows(dst_ref, src_ref, col0):
    for gl in range(S5_TILE_GROUPS):
        for s in range(S5_L):
            r = s * LANES + gl * S5_GROUP
            dst_ref[r:r + S5_GROUP, col0 + gl * LANES:col0 + (gl + 1) * LANES] = (
                src_ref[gl, s * S5_GROUP:(s + 1) * S5_GROUP, :])


def _s5_drive_kernel(u_ref, bcf_ref, bcb_ref, ef_ref, eb_ref, op_ref):
    first = (pl.program_id(0) == 0) & (pl.program_id(1) == 0)

    @pl.when(first)
    def _():
        op_ref[...] = jnp.zeros(op_ref.shape, BF16)

    @pl.when(pl.program_id(1) == 0)
    def _():
        _s5_place_group_rows(op_ref, bcf_ref, 0)
        _s5_place_group_rows(op_ref, bcb_ref, S5_TILE_STATE)

    e = _dot(_s5_chunk_cat(u_ref, BF16), op_ref[...])
    ef_ref[...] = e[:, :S5_TILE_STATE]
    eb_ref[...] = e[:, S5_TILE_STATE:]


def s5_drive(uz, bc):
    rows = TOKENS // S5_ROW_SPLIT
    out = jax.ShapeDtypeStruct((S5_CHUNKS, S5_GROUPS * 2 * S5_STATE), F32)
    ospec = pl.BlockSpec((rows // S5_L, S5_TILE_STATE), lambda q, i: (i, q))

    def bspec(d):
        return pl.BlockSpec((None, S5_TILE_GROUPS, S5_L * S5_GROUP, 2 * S5_STATE),
                            lambda q, i: (d, q, 0, 0))
    return pl.pallas_call(
        _s5_drive_kernel,
        out_shape=[out, out],
        grid=(S5_TILES, S5_ROW_SPLIT),
        in_specs=[pl.BlockSpec((rows, LANES), lambda q, i: (i, q)), bspec(0), bspec(1)],
        out_specs=[ospec, ospec],
        scratch_shapes=[pltpu.VMEM((S5_CAT, 2 * S5_TILE_STATE), BF16)],
        compiler_params=_cparams("arbitrary", "arbitrary"),
    )(uz, bc, bc)


def _s5_scan_kernel(*refs, reverse):
    ng = S5_TILE_GROUPS
    nj = S5_SEG_CHUNKS
    e_refs, a_ref, x_refs = refs[:ng], refs[ng], refs[ng + 1:2 * ng + 1]
    in_ref, es_ref = refs[-2:]
    lanes = [slice(g * LANES, (g + 1) * LANES) for g in range(ng)]
    swap = lambda z: pltpu.roll(z, S5_STATE, 1)
    for g in range(ng):
        es_ref[g] = swap(e_refs[g][...])
    a1 = [a_ref[0, 0][:, sl] for sl in lanes]
    a2 = [a_ref[0, 1][:, sl] for sl in lanes]
    seg0 = lax.broadcasted_iota(jnp.int32, (S5_SEG, LANES), 0) == 0

    def tile(j):
        jj = (nj - 1 - j) if reverse else j
        return jj, pl.ds(jj, S5_SEG, stride=nj)

    def step(j, carry, emit):
        jj, rows = tile(j)
        out = []
        for g in range(ng):
            z, zs = carry[2 * g], carry[2 * g + 1]
            if reverse:
                reset = seg0 & (jj == S5_CTX_CHUNKS - 1)
                z, zs = jnp.where(reset, 0.0, z), jnp.where(reset, 0.0, zs)
            if emit:
                x_refs[g][rows, :] = z
            out += [a1[g] * z + a2[g] * zs + e_refs[g][rows, :],
                    a1[g] * zs - a2[g] * z + es_ref[g, rows, :]]
        return tuple(out)

    zero = tuple(jnp.zeros((S5_SEG, LANES), F32) for _ in range(2 * ng))
    end = lax.fori_loop(0, nj, functools.partial(step, emit=False), zero, unroll=2)[0::2]
    order = range(S5_SEG - 1, -1, -1) if reverse else range(S5_SEG)
    for g, sl in enumerate(lanes):
        cur = end[g][0:1, :] if reverse else jnp.zeros((1, LANES), F32)
        for s in order:
            in_ref[s:s + 1, sl] = cur
            cur = a_ref[1, 0][:, sl] * cur + a_ref[1, 1][:, sl] * swap(cur) + end[g][s:s + 1, :]
    start = []
    for sl in lanes:
        start += [in_ref[:, sl], swap(in_ref[:, sl])]
    lax.fori_loop(0, nj, functools.partial(step, emit=True), tuple(start), unroll=2)


def s5_scan(e, rows, direction):
    ng = S5_TILE_GROUPS
    cols = S5_TILE_STATE
    blks = [pl.BlockSpec((S5_CHUNKS, LANES), lambda i, g=g: (0, ng * i + g)) for g in range(ng)]
    return pl.pallas_call(
        functools.partial(_s5_scan_kernel, reverse=direction == 1),
        out_shape=[jax.ShapeDtypeStruct((S5_CHUNKS, S5_TILES * LANES), F32)] * ng,
        grid=(S5_TILES,),
        in_specs=blks + [pl.BlockSpec((None, 2, 2, 1, cols),
                                      lambda i: (direction, 0, 0, 0, i))],
        out_specs=[pl.BlockSpec((S5_CHUNKS, LANES), lambda i: (0, i))] * ng,
        scratch_shapes=[pltpu.VMEM((S5_SEG, cols), F32),
                        pltpu.VMEM((ng, S5_CHUNKS, LANES), F32)],
        compiler_params=_cparams("arbitrary"),
    )(*([e] * ng), rows)


def _gelu_tanh(x):
    return 0.5 * x * (1.0 + jnp.tanh(math.sqrt(2.0 / math.pi) * (x + 0.044715 * (x * x * x))))


def _s5_read_kernel(u_ref, b0f_ref, b0b_ref, c0_ref, cctf_ref, cctb_ref, d_ref, *refs):
    x_refs = refs[:2 * S5_TILE_GROUPS]
    o_ref, m_ref, c_ref, t_ref, dk_ref = refs[2 * S5_TILE_GROUPS:]
    first = (pl.program_id(0) == 0) & (pl.program_id(1) == 0)
    ts = S5_TILE_STATE

    @pl.when(first)
    def _():
        c_ref[...] = jnp.zeros(c_ref.shape, BF16)
        t_ref[...] = jnp.zeros(t_ref.shape, BF16)

    @pl.when(pl.program_id(1) == 0)
    def _():
        _s5_place_group_rows(c_ref, cctf_ref, 0)
        _s5_place_group_rows(c_ref, cctb_ref, ts)
        for k, src in enumerate((b0f_ref, b0b_ref, c0_ref)):
            for gl in range(S5_TILE_GROUPS):
                t_ref[k, gl * S5_GROUP:(gl + 1) * S5_GROUP, gl * LANES:(gl + 1) * LANES] = src[gl]
        b0f, b0b, c0 = t_ref[0], t_ref[1], t_ref[2]
        dk_ref[S5_L - 1] = (_dot_nt(b0f, c0) + _dot_nt(b0b, c0)).astype(BF16)
        for lag in range(1, S5_L):
            cf = c_ref[(lag - 1) * LANES:lag * LANES, 0:ts]
            dk_ref[S5_L - 1 + lag] = _dot_nt(b0f, cf).astype(BF16)
            cb = c_ref[(S5_L - lag) * LANES:(S5_L - lag + 1) * LANES, ts:2 * ts]
            dk_ref[S5_L - 1 - lag] = _dot_nt(b0b, cb).astype(BF16)
        for s in range(S5_L):
            for t in range(S5_L):
                m_ref[s * LANES:(s + 1) * LANES, t * LANES:(t + 1) * LANES] = (
                    dk_ref[t - s + S5_L - 1])

    n = u_ref.shape[0] // S5_L
    xs = jnp.concatenate([x[...].astype(BF16) for x in x_refs], axis=-1)
    y = _dot(_s5_chunk_cat(u_ref, BF16), m_ref[...]) + _dot_nt(xs, c_ref[...])
    for t in range(S5_L):
        rows = pl.ds(t, n, stride=S5_L)
        o_ref[rows, :] = _gelu_tanh(y[:, t * LANES:(t + 1) * LANES] + d_ref[...] * u_ref[rows, :])


def s5_read(uz, bc, c0, cct, xf, xb, d_skip):
    rows = TOKENS // S5_ROW_SPLIT
    ublk = pl.BlockSpec((rows, LANES), lambda q, i: (i, q))
    xblk = pl.BlockSpec((rows // S5_L, LANES), lambda q, i: (i, q))
    gp = (S5_TILE_GROUPS, S5_GROUP, 2 * S5_STATE)

    def cspec(d):
        return pl.BlockSpec((None, S5_TILE_GROUPS, S5_L * S5_GROUP, 2 * S5_STATE),
                            lambda q, i: (d, q, 0, 0))

    def b0spec(d, s):
        return pl.BlockSpec((None, S5_TILE_GROUPS, None, S5_GROUP, 2 * S5_STATE),
                            lambda q, i: (d, q, s, 0, 0))
    bc5 = bc.reshape(2, S5_GROUPS, S5_L, S5_GROUP, 2 * S5_STATE)
    return pl.pallas_call(
        _s5_read_kernel,
        out_shape=jax.ShapeDtypeStruct((TOKENS, S5_WIDTH), F32),
        grid=(S5_TILES, S5_ROW_SPLIT),
        in_specs=[ublk, b0spec(0, S5_L - 1), b0spec(1, 0),
                  pl.BlockSpec(gp, lambda q, i: (q, 0, 0)),
                  cspec(0), cspec(1),
                  pl.BlockSpec((1, LANES), lambda q, i: (0, q))] + [xblk] * (len(xf) + len(xb)),
        out_specs=ublk,
        scratch_shapes=[pltpu.VMEM((S5_CAT, S5_CAT), BF16),
                        pltpu.VMEM((S5_CAT, 2 * S5_TILE_STATE), BF16),
                        pltpu.VMEM((3, LANES, S5_TILE_STATE), BF16),
                        pltpu.VMEM((2 * S5_L - 1, LANES, LANES), BF16)],
        compiler_params=_cparams("arbitrary", "arbitrary"),
    )(uz, bc5, bc5, c0, cct, cct, d_skip.astype(F32).reshape(1, S5_WIDTH), *xf, *xb)


def _glu_kernel(s_ref, w_ref, b_ref, o_ref, lhs_ref, *, tn):
    j = pl.program_id(1)

    @pl.when(j == 0)
    def _():
        lhs_ref[...] = s_ref[...].astype(BF16)

    v = _dot(lhs_ref[...], w_ref[...]) + b_ref[...]
    sj = s_ref[:, pl.ds(pl.multiple_of(j * tn, tn), tn)]
    o_ref[...] = sj * jax.nn.sigmoid(v)


def s5_glu(s, w_glu, b_glu, tm=768, tn=1024):
    d = S5_WIDTH
    return pl.pallas_call(
        functools.partial(_glu_kernel, tn=tn),
        out_shape=jax.ShapeDtypeStruct((TOKENS, d), F32),
        grid=(TOKENS // tm, d // tn),
        in_specs=[pl.BlockSpec((tm, d), lambda i, j: (i, 0)),
                  pl.BlockSpec((d, tn), lambda i, j: (0, j)),
                  pl.BlockSpec((1, tn), lambda i, j: (0, j))],
        out_specs=pl.BlockSpec((tm, tn), lambda i, j: (i, j)),
        scratch_shapes=[pltpu.VMEM((tm, d), BF16)],
        compiler_params=_cparams("arbitrary", "arbitrary"),
    )(s, w_glu, b_glu.reshape(1, d))


def _block_diag(w, size):
    lane = np.arange(size)
    spread = jnp.asarray(lane[None, :] % ML_BLOCK == np.arange(ML_BLOCK)[:, None], BF16)
    tiled = jnp.dot(w.astype(BF16).reshape(-1, ML_BLOCK), spread, preferred_element_type=BF16)
    same_block = jnp.asarray(lane[:, None] // ML_BLOCK == lane[None, :] // ML_BLOCK)
    return jnp.where(same_block, tiled.reshape(-1, size, size), jnp.zeros((), BF16))


def na_layer(xc, h, mods, tail, w_in, li, rpb, w_out):
    e = D_MODEL
    last = isinstance(tail[0], str)
    qkv = in_proj(h, w_in, li, 0, 3 * e, BF16, scaled_cols=e, scale=NA_Q_SCALE)
    z = in_proj(h, w_in, li, 3 * e, e, F32)
    o = na_attention(qkv, na_pair_table(rpb), with_ctx_out=not last)
    if last:
        return gated_out_final(o, z, 0, w_out.astype(BF16), xc, mods, tail[1])
    return gated_out(o, z, 0, w_out.astype(BF16), xc, mods, *tail)


def mlstm_layer(xc, h, mods, tail, w_in, li, conv_w, conv_b, wq, wk, wv, w_gate, b_gate, skip,
                norm_g, w_out):
    e = ML_WIDTH
    xz = in_proj(h, w_in, li, 0, 2 * e, F32)
    wq_bd = _block_diag(wq, ML_DIAG)
    wk_bd = _block_diag(wk, ML_DIAG)
    wv_bd = _block_diag(wv, ML_DIAG)
    wvt_bd = _block_diag(wv.transpose(0, 2, 1), ML_DIAG)
    wg_t = w_gate.reshape(3, e, 4 * ML_HEADS).transpose(0, 2, 1).astype(BF16)
    xconv, q, k, vt, gates_t = ml_pre(xz, conv_w, conv_b, wq_bd, wk_bd, wv_bd, wvt_bd, wg_t,
                                      b_gate)
    hf, hb = ml_recurrence(q, k, vt, gates_t)
    return ml_out(hf, hb, xconv, xz, norm_g, skip, w_out.astype(BF16), xc, mods, *tail)


def s5_layer(xc, h, mods, tail, w_in, li, lam_re, lam_im, log_dt, b_re, b_im, c_re, c_im,
             d_skip, w_glu, b_glu, w_out):
    e = S5_WIDTH
    uz = in_proj(h, w_in, li, 0, 2 * e, F32)
    bc, c0, cct, rows = s5_operators(lam_re, lam_im, log_dt, b_re, b_im, c_re, c_im)
    ef, eb = s5_drive(uz, bc)
    xf = s5_scan(ef, rows, 0)
    xb = s5_scan(eb, rows, 1)
    s = s5_read(uz, bc, c0, cct, xf, xb, d_skip)
    s2 = s5_glu(s, w_glu.astype(BF16), b_glu)
    return gated_out(s2, uz, e, w_out.astype(BF16), xc, mods, *tail)


def kernel(x, c, ctx, c_ctx, norm_g, ada_w, ada_b, na_w_in, na_rpb, na_w_out, ml_w_in, ml_conv_w, ml_conv_b, ml_wq, ml_wk, ml_wv, ml_w_gate, ml_b_gate, ml_skip, ml_norm_g, ml_w_out, s5_w_in, s5_lam_re, s5_lam_im, s5_log_dt, s5_b_re, s5_b_im, s5_c_re, s5_c_im, s5_d, s5_w_glu, s5_b_glu, s5_w_out, final_g):
    xc = jnp.concatenate([ctx[0], x[0]], axis=0)
    c8 = jnp.concatenate([c, c_ctx[None, :], jnp.zeros((6, D_MODEL), F32)], axis=0)
    mods = ada_mods(c8, ada_w, ada_b)
    h = norm_mod(xc, norm_g[0], mods[0])
    ia = ib = ic = 0
    assert (DEPTH - 1) % 3 == 0, "the last layer must be the attention layer that ends the trunk"
    for layer in range(DEPTH):
        tail = ((norm_g[layer + 1], mods[layer + 1]) if layer + 1 < DEPTH
                else ('final', final_g))
        kind = layer % 3
        if kind == 0:
            out = na_layer(xc, h, mods[layer], tail, na_w_in, ia, na_rpb[ia], na_w_out[ia])
            ia += 1
        elif kind == 1:
            out = mlstm_layer(xc, h, mods[layer], tail, ml_w_in, ib, ml_conv_w[ib],
                              ml_conv_b[ib], ml_wq[ib], ml_wk[ib], ml_wv[ib], ml_w_gate[ib],
                              ml_b_gate[ib], ml_skip[ib], ml_norm_g[ib], ml_w_out[ib])
            ib += 1
        else:
            out = s5_layer(xc, h, mods[layer], tail, s5_w_in, ic, s5_lam_re[ic],
                           s5_lam_im[ic], s5_log_dt[ic], s5_b_re[ic], s5_b_im[ic], s5_c_re[ic],
                           s5_c_im[ic], s5_d[ic], s5_w_glu[ic], s5_b_glu[ic], s5_w_out[ic])
            ic += 1
        if layer + 1 < DEPTH:
            xc, h = out
    return out[None]
```

```python
import functools
import itertools
import math

import numpy as np
import jax
import jax.numpy as jnp
from jax import lax
from jax.experimental import pallas as pl
from jax.experimental.pallas import tpu as pltpu

F32 = jnp.float32
BF16 = jnp.bfloat16

D_MODEL = 2048
SEQ = 8192
CTX_LEN = 256
TOKENS = CTX_LEN + SEQ
DEPTH = 4
EPS = 1e-6
NEG = -1e30

GRID_W = 64
GRID_ROWS = SEQ // GRID_W
NA_HEADS = 16
NA_HEAD_DIM = 128
WIN_H = 8
WIN_W = 16
NA_QROWS = 4
NA_BLK = NA_QROWS * GRID_W
NA_WIN_BLKS = 3
NA_HEADS_PER_STEP = 8
LOG2E = math.log2(math.e)
NA_Q_SCALE = NA_HEAD_DIM ** -0.5 * LOG2E

ML_WIDTH = 2 * D_MODEL
ML_HEADS = 8
ML_HEAD_DIM = ML_WIDTH // ML_HEADS
ML_BLOCK = 4
ML_CHUNK = 256
ML_REC_HEADS = 4
ML_COLS = 512
ML_DIAG = 256
ML_STRIP = 256

S5_WIDTH = D_MODEL
S5_GROUP = 16
S5_GROUPS = S5_WIDTH // S5_GROUP
S5_STATE = 64
S5_L = 16
S5_SEG = 8
S5_CHUNKS = TOKENS // S5_L
S5_CTX_CHUNKS = CTX_LEN // S5_L
S5_SEG_CHUNKS = S5_CHUNKS // S5_SEG
LANES = 128
S5_TILE_GROUPS = LANES // S5_GROUP
S5_TILES = S5_WIDTH // LANES
S5_CAT = S5_L * LANES
S5_TILE_STATE = S5_TILE_GROUPS * 2 * S5_STATE
S5_ROW_SPLIT = 2

VMEM_LIMIT = 56 * 1024 * 1024


def _cparams(*sem):
    return pltpu.CompilerParams(dimension_semantics=sem, vmem_limit_bytes=VMEM_LIMIT)


def _dot(a, b):
    return jnp.dot(a, b, preferred_element_type=F32)


def _dot_nt(a, b):
    return lax.dot_general(a, b, (((1,), (1,)), ((), ())), preferred_element_type=F32)


def _silu(x):
    return x * jax.nn.sigmoid(x)


def _row_select(row0, n, ctx_row, x_row):
    rows = row0 + lax.broadcasted_iota(jnp.int32, (n, 1), 0)
    return jnp.where(rows < CTX_LEN, ctx_row, x_row)


def _ada_kernel(c_ref, w_ref, b_ref, o_ref):
    cs = _silu(c_ref[...]).astype(BF16)
    o_ref[0] = _dot(cs, w_ref[0].astype(BF16)) + b_ref[0]


def ada_mods(c8, ada_w, ada_b):
    tn = 512
    n = 3 * D_MODEL
    return pl.pallas_call(
        _ada_kernel,
        out_shape=jax.ShapeDtypeStruct((DEPTH, 8, n), F32),
        grid=(DEPTH, n // tn),
        in_specs=[pl.BlockSpec((8, D_MODEL), lambda l, j: (0, 0)),
                  pl.BlockSpec((1, D_MODEL, tn), lambda l, j: (l, 0, j)),
                  pl.BlockSpec((1, 1, tn), lambda l, j: (l, 0, j))],
        out_specs=pl.BlockSpec((1, 8, tn), lambda l, j: (l, 0, j)),
        compiler_params=_cparams("arbitrary", "arbitrary"),
    )(c8, ada_w, ada_b.reshape(DEPTH, 1, n))


def _rmsnorm(x, g):
    return x * lax.rsqrt(jnp.mean(x * x, axis=-1, keepdims=True) + EPS) * g


def _norm_mod_rows(x, g_ref, sh_ref, sc_ref, row0):
    sc = _row_select(row0, x.shape[0], sc_ref[1:2, :], sc_ref[0:1, :])
    sh = _row_select(row0, x.shape[0], sh_ref[1:2, :], sh_ref[0:1, :])
    return (_rmsnorm(x, g_ref[...]) * (1.0 + sc) + sh).astype(BF16)


def _norm_specs(g, mods):
    d = D_MODEL
    return ([g.reshape(1, d), mods, mods],
            [pl.BlockSpec((1, d), lambda *a: (0, 0)), pl.BlockSpec((8, d), lambda *a: (0, 0)),
             pl.BlockSpec((8, d), lambda *a: (0, 1))])


def _stream_specs(xc, tm):
    d = D_MODEL
    if not isinstance(xc, tuple):
        return [xc], [pl.BlockSpec((tm, d), lambda i: (i, 0))]
    assert tm == CTX_LEN
    return list(xc), [pl.BlockSpec((tm, d), lambda i: (0, 0)),
                      pl.BlockSpec((tm, d), lambda i: (jnp.maximum(i - 1, 0), 0))]


def _stream_tile(refs):
    if len(refs) == 1:
        return refs[0][...]
    return jnp.where(pl.program_id(0) == 0, refs[0][...], refs[1][...])


def _norm_mod_kernel(*refs, tm):
    g_ref, sh_ref, sc_ref, o_ref = refs[-4:]
    o_ref[...] = _norm_mod_rows(_stream_tile(refs[:-4]), g_ref, sh_ref, sc_ref,
                                pl.program_id(0) * tm)


def norm_mod(xc, g, mods, tm=384):
    d = D_MODEL
    nargs, nspecs = _norm_specs(g, mods)
    sargs, sspecs = _stream_specs(xc, tm)
    return pl.pallas_call(
        functools.partial(_norm_mod_kernel, tm=tm),
        out_shape=jax.ShapeDtypeStruct((TOKENS, d), BF16),
        grid=(TOKENS // tm,),
        in_specs=sspecs + nspecs,
        out_specs=pl.BlockSpec((tm, d), lambda i: (i, 0)),
        compiler_params=_cparams("arbitrary"),
    )(*sargs, *nargs)


def _in_proj_kernel(h_ref, w_ref, o_ref, wb_ref, *, scaled_blocks, scale):
    j = pl.program_id(0)

    @pl.when(pl.program_id(1) == 0)
    def _():
        wb_ref[...] = w_ref[...].astype(BF16)

    acc = _dot(h_ref[...], wb_ref[...])
    if scaled_blocks:
        acc = acc * jnp.where(j < scaled_blocks, scale, 1.0)
    o_ref[...] = acc.astype(o_ref.dtype)


def in_proj(h, w, layer, col0, ncols, out_dtype, scaled_cols=0, scale=1.0, tm=1408, tn=1024):
    d = D_MODEL
    cb = col0 // tn
    return pl.pallas_call(
        functools.partial(_in_proj_kernel, scaled_blocks=scaled_cols // tn, scale=scale),
        out_shape=jax.ShapeDtypeStruct((TOKENS, ncols), out_dtype),
        grid=(ncols // tn, TOKENS // tm),
        in_specs=[pl.BlockSpec((tm, d), lambda j, i: (i, 0)),
                  pl.BlockSpec((None, d, tn), lambda j, i: (layer, 0, cb + j))],
        out_specs=pl.BlockSpec((tm, tn), lambda j, i: (i, j)),
        scratch_shapes=[pltpu.VMEM((d, tn), BF16)],
        compiler_params=_cparams("arbitrary", "arbitrary"),
    )(h, w)


def _gated_out_kernel(a_ref, z_ref, w_ref, gate_ref, g_ref, sh_ref, sc_ref, *refs, tm):
    o_ref, h_ref = refs[-2:]
    i = pl.program_id(0)
    lhs = (a_ref[...].astype(F32) * _silu(z_ref[...])).astype(BF16)
    gate = _row_select(i * tm, tm, gate_ref[1:2, :], gate_ref[0:1, :])
    x = _stream_tile(refs[:-2]) + gate * _dot(lhs, w_ref[...])
    o_ref[...] = x
    h_ref[...] = _norm_mod_rows(x, g_ref, sh_ref, sc_ref, i * tm)


def gated_out(a, zsrc, zcol0, w, xc, mods, next_g, next_mods, tm=384):
    k = a.shape[1]
    d = D_MODEL
    zb = zcol0 // k
    nargs, nspecs = _norm_specs(next_g, next_mods)
    sargs, sspecs = _stream_specs(xc, tm)
    row = pl.BlockSpec((tm, d), lambda i: (i, 0))
    return pl.pallas_call(
        functools.partial(_gated_out_kernel, tm=tm),
        out_shape=[jax.ShapeDtypeStruct((TOKENS, d), F32),
                   jax.ShapeDtypeStruct((TOKENS, d), BF16)],
        grid=(TOKENS // tm,),
        in_specs=[pl.BlockSpec((tm, k), lambda i: (i, 0)),
                  pl.BlockSpec((tm, k), lambda i: (i, zb)),
                  pl.BlockSpec((k, d), lambda i: (0, 0)),
                  pl.BlockSpec((8, d), lambda i: (0, 2))] + nspecs + sspecs,
        out_specs=[row, row],
        compiler_params=_cparams("arbitrary"),
    )(a, zsrc, w, mods, *nargs, *sargs)


def _gated_out_final_kernel(a_ref, z_ref, w_ref, x_ref, gate_ref, g_ref, o_ref):
    lhs = (a_ref[...].astype(F32) * _silu(z_ref[...])).astype(BF16)
    x = x_ref[...] + gate_ref[0:1, :] * _dot(lhs, w_ref[...])
    o_ref[...] = _rmsnorm(x, g_ref[...])


def gated_out_final(a, zsrc, zcol0, w, xc, mods, final_g, tm=256):
    k = a.shape[1]
    d = D_MODEL
    zb = zcol0 // k
    off = CTX_LEN // tm
    return pl.pallas_call(
        _gated_out_final_kernel,
        out_shape=jax.ShapeDtypeStruct((SEQ, d), F32),
        grid=(SEQ // tm,),
        in_specs=[pl.BlockSpec((tm, k), lambda i: (i, 0)),
                  pl.BlockSpec((tm, k), lambda i: (i + off, zb)),
                  pl.BlockSpec((k, d), lambda i: (0, 0)),
                  pl.BlockSpec((tm, d), lambda i: (i + off, 0)),
                  pl.BlockSpec((8, d), lambda i: (0, 2)),
                  pl.BlockSpec((1, d), lambda i: (0, 0))],
        out_specs=pl.BlockSpec((tm, d), lambda i: (i, 0)),
        compiler_params=_cparams("arbitrary"),
    )(a, zsrc, w, xc, mods, final_g.reshape(1, d))


NA_REL_ROWS = 2 * WIN_H - 1
NA_A_LO = WIN_H - 1 - WIN_H // 2
NA_A_HI = NA_A_LO + WIN_H - 1
NA_PAIR_BOTH = NA_REL_ROWS - 1
NA_PAIR_LEFT_MASKED = NA_PAIR_BOTH
NA_PAIR_RIGHT_MASKED = NA_PAIR_BOTH + 1
NA_PAIR_MASKED = NA_PAIR_BOTH + 2
NA_PAIR_ENTRIES = NA_PAIR_BOTH + 3
NA_BAND_ROWS = NA_WIN_BLKS * NA_QROWS


def _na_pair_plan():
    nblk = GRID_ROWS // NA_QROWS
    plan = np.full((4, NA_QROWS, NA_BAND_ROWS // 2), NA_PAIR_MASKED, np.int32)
    for kind, g in ((0, 0), (1, 1), (2, nblk - 1)):
        band0 = NA_QROWS * int(np.clip(g - 1, 0, nblk - NA_WIN_BLKS))
        for i in range(NA_QROWS):
            rq = NA_QROWS * g + i
            r0 = int(np.clip(rq - WIN_H // 2, 0, GRID_ROWS - WIN_H))
            rel = [rk - rq + WIN_H - 1 if r0 <= rk < r0 + WIN_H else None
                   for rk in range(band0, band0 + NA_BAND_ROWS)]
            for p in range(NA_BAND_ROWS // 2):
                lo, hi = rel[2 * p], rel[2 * p + 1]
                if lo is not None and hi is not None:
                    plan[kind, i, p] = lo
                elif hi is not None:
                    assert hi == NA_A_LO
                    plan[kind, i, p] = NA_PAIR_LEFT_MASKED
                elif lo is not None:
                    assert lo == NA_A_HI
                    plan[kind, i, p] = NA_PAIR_RIGHT_MASKED
    return plan


def na_pair_table(rpb):
    cq = np.arange(GRID_W)[:, None]
    ck = np.arange(GRID_W)[None, :]
    c0 = np.clip(cq - WIN_W // 2, 0, GRID_W - WIN_W)
    ok = (ck >= c0) & (ck < c0 + WIN_W)
    rp = jnp.pad(rpb.astype(F32), ((0, 0), (0, 0), (GRID_W, GRID_W)))
    t = jnp.stack([rp[:, :, GRID_W + WIN_W - 1 - c:2 * GRID_W + WIN_W - 1 - c]
                   for c in range(GRID_W)], axis=2)
    t = jnp.where(jnp.asarray(ok), t * LOG2E, NEG)
    masked = jnp.full((NA_HEADS, 1, GRID_W, GRID_W), NEG, F32)
    left = jnp.concatenate([t[:, :-1], masked, t[:, NA_A_HI:NA_A_HI + 1], masked], axis=1)
    right = jnp.concatenate([t[:, 1:], t[:, NA_A_LO:NA_A_LO + 1], masked, masked], axis=1)
    return jnp.concatenate([left, right], axis=-1)


def _na_kernel(q_ref, kc_ref, k0_ref, k1_ref, k2_ref, vc_ref, v0_ref, v1_ref, v2_ref,
               pair_ref, o_ref, bias_ref, *, heads, q0, nq):
    qb = pl.program_id(1) + q0
    plan = _na_pair_plan()
    pw = 2 * GRID_W

    for kind, first in ((3, 0), (0, 1), (1, 2), (2, nq - 1)):
        @pl.when(qb == first)
        def _(kind=kind):
            for h in range(heads):
                for i in range(NA_QROWS):
                    for p in range(NA_BAND_ROWS // 2):
                        bias_ref[h, i * GRID_W:(i + 1) * GRID_W, p * pw:(p + 1) * pw] = (
                            pair_ref[h, int(plan[kind, i, p])])

    for h in range(heads):
        sl = slice(h * NA_HEAD_DIM, (h + 1) * NA_HEAD_DIM)
        q = q_ref[:, sl]
        s = [_dot_nt(q, kc_ref[:, sl])]
        for b, k_ref in enumerate((k0_ref, k1_ref, k2_ref)):
            s.append(_dot_nt(q, k_ref[:, sl]) + bias_ref[h, :, b * NA_BLK:(b + 1) * NA_BLK])
        m = s[0].max(axis=-1, keepdims=True)
        for t in s[1:]:
            m = jnp.maximum(m, t.max(axis=-1, keepdims=True))
        p = [jnp.exp2(t - m) for t in s]
        den = p[0].sum(axis=-1, keepdims=True)
        for t in p[1:]:
            den = den + t.sum(axis=-1, keepdims=True)
        p_all = jnp.concatenate([t.astype(BF16) for t in p], axis=1)
        v_all = jnp.concatenate([v_ref[:, sl] for v_ref in (vc_ref, v0_ref, v1_ref, v2_ref)],
                                axis=0)
        o_ref[:, sl] = _dot(p_all, v_all) / den


def na_attention(qkv, pair_tab, with_ctx_out):
    hs = NA_HEADS_PER_STEP
    hw = hs * NA_HEAD_DIM
    nh = NA_HEADS // hs
    nq = TOKENS // NA_BLK
    q0 = 0 if with_ctx_out else 1
    nlat = GRID_ROWS // NA_QROWS

    def band(qb):
        return 1 + jnp.clip(qb - 2, 0, nlat - NA_WIN_BLKS)

    def spec(col_base, fn):
        return pl.BlockSpec((NA_BLK, hw), lambda h, g: (fn(g + q0), col_base + h))

    kspecs = [spec(nh, lambda qb: 0 * qb)] + [
        spec(nh, lambda qb, b=b: band(qb) + b) for b in range(NA_WIN_BLKS)]
    vspecs = [spec(2 * nh, lambda qb: 0 * qb)] + [
        spec(2 * nh, lambda qb, b=b: band(qb) + b) for b in range(NA_WIN_BLKS)]
    return pl.pallas_call(
        functools.partial(_na_kernel, heads=hs, q0=q0, nq=nq),
        out_shape=jax.ShapeDtypeStruct(((nq - q0) * NA_BLK, D_MODEL), F32),
        grid=(nh, nq - q0),
        in_specs=[spec(0, lambda qb: qb)] + kspecs + vspecs + [
            pl.BlockSpec((hs, NA_PAIR_ENTRIES, GRID_W, 2 * GRID_W),
                         lambda h, g: (h, 0, 0, 0))],
        out_specs=pl.BlockSpec((NA_BLK, hw), lambda h, g: (g, h)),
        scratch_shapes=[pltpu.VMEM((hs, NA_BLK, NA_WIN_BLKS * NA_BLK), F32)],
        compiler_params=_cparams("arbitrary", "arbitrary"),
    )(*([qkv] * 9), pair_tab)


def _ml_pre_kernel(x_ref, prev_ref, next_ref, cw_ref, cb_ref, wq_ref, wk_ref, wv_ref, wvt_ref,
                   wg_ref, bg_ref, xc_ref, q_ref, k_ref, vt_ref, g_ref, *, tm):
    i = pl.program_id(0)
    j = pl.program_id(1)
    x = x_ref[...]
    rows = i * tm + lax.broadcasted_iota(jnp.int32, (tm, 1), 0)
    local = lax.broadcasted_iota(jnp.int32, (tm, 1), 0)
    x_prev = jnp.where(local == 0, prev_ref[7:8, :], pltpu.roll(x, 1, 0))
    x_prev = jnp.where((rows == 0) | (rows == CTX_LEN), 0.0, x_prev)
    x_next = jnp.where(local == tm - 1, next_ref[0:1, :], pltpu.roll(x, tm - 1, 0))
    x_next = jnp.where((rows == CTX_LEN - 1) | (rows == TOKENS - 1), 0.0, x_next)
    @pl.when(j == 0)
    def _():
        g_ref[...] = jnp.broadcast_to(bg_ref[...], g_ref.shape)

    def activate(r):
        rs = slice(r * ML_STRIP, (r + 1) * ML_STRIP)
        conv = (cb_ref[...] + x_prev[rs] * cw_ref[0:1, :] + x[rs] * cw_ref[1:2, :]
                + x_next[rs] * cw_ref[2:3, :])
        xcb = _silu(conv).astype(BF16)
        xc_ref[rs, :] = xcb
        return xcb, x[rs].astype(BF16)

    def project(r, xcb, xb):
        rs = slice(r * ML_STRIP, (r + 1) * ML_STRIP)
        part = None
        for t in range(ML_COLS // ML_DIAG):
            cs = slice(t * ML_DIAG, (t + 1) * ML_DIAG)
            q = _dot(xcb[:, cs], wq_ref[t])
            k = _dot(xcb[:, cs], wk_ref[t])
            v = _dot(xb[:, cs], wv_ref[t])
            qb, kb, vb = q.astype(BF16), k.astype(BF16), v.astype(BF16)
            q_ref[rs, cs] = qb
            k_ref[rs, cs] = (k * (ML_HEAD_DIM ** -0.5)).astype(BF16)
            vt_ref[cs, rs] = _dot_nt(wvt_ref[t], xb[:, cs]).astype(BF16)
            gt = (_dot_nt(wg_ref[0, :, cs], qb) + _dot_nt(wg_ref[1, :, cs], kb)
                  + _dot_nt(wg_ref[2, :, cs], vb))
            part = gt if part is None else part + gt
        g_ref[:, rs] += part

    pending = activate(0)
    for r in range(tm // ML_STRIP):
        nxt = activate(r + 1) if (r + 1) * ML_STRIP < tm else None
        project(r, *pending)
        pending = nxt


def ml_pre(xz, conv_w, conv_b, wq_bd, wk_bd, wv_bd, wvt_bd, wg_t, b_gate, tm=768):
    e = ML_WIDTH
    c = ML_COLS
    nc = e // c
    last8 = TOKENS // 8 - 1
    tok = lambda i, j: (i, j)
    outs = pl.pallas_call(
        functools.partial(_ml_pre_kernel, tm=tm),
        out_shape=[jax.ShapeDtypeStruct((TOKENS, e), BF16),
                   jax.ShapeDtypeStruct((TOKENS, e), BF16),
                   jax.ShapeDtypeStruct((TOKENS, e), BF16),
                   jax.ShapeDtypeStruct((e, TOKENS), BF16),
                   jax.ShapeDtypeStruct((4 * ML_HEADS, TOKENS), F32)],
        grid=(TOKENS // tm, nc),
        in_specs=[pl.BlockSpec((tm, c), tok),
                  pl.BlockSpec((8, c), lambda i, j: (jnp.maximum(i * (tm // 8) - 1, 0), j)),
                  pl.BlockSpec((8, c), lambda i, j: (jnp.minimum((i + 1) * (tm // 8), last8), j)),
                  pl.BlockSpec((3, c), lambda i, j: (0, j)),
                  pl.BlockSpec((1, c), lambda i, j: (0, j)),
                  pl.BlockSpec((c // ML_DIAG, ML_DIAG, ML_DIAG), lambda i, j: (j, 0, 0)),
                  pl.BlockSpec((c // ML_DIAG, ML_DIAG, ML_DIAG), lambda i, j: (j, 0, 0)),
                  pl.BlockSpec((c // ML_DIAG, ML_DIAG, ML_DIAG), lambda i, j: (j, 0, 0)),
                  pl.BlockSpec((c // ML_DIAG, ML_DIAG, ML_DIAG), lambda i, j: (j, 0, 0)),
                  pl.BlockSpec((3, 4 * ML_HEADS, c), lambda i, j: (0, 0, j)),
                  pl.BlockSpec((4 * ML_HEADS, 1), lambda i, j: (0, 0))],
        out_specs=[pl.BlockSpec((tm, c), tok), pl.BlockSpec((tm, c), tok),
                   pl.BlockSpec((tm, c), tok),
                   pl.BlockSpec((c, tm), lambda i, j: (j, i)),
                   pl.BlockSpec((4 * ML_HEADS, tm), lambda i, j: (0, i))],
        compiler_params=_cparams("arbitrary", "arbitrary"),
    )(xz, xz, xz, conv_w, conv_b.reshape(1, e), wq_bd, wk_bd, wv_bd, wvt_bd, wg_t,
      b_gate.reshape(4 * ML_HEADS, 1))
    return outs


def _split3(x):
    hi = x.astype(BF16).astype(F32)
    r1 = x - hi
    mid = r1.astype(BF16).astype(F32)
    lo = (r1 - mid).astype(BF16).astype(F32)
    return hi, mid, lo


def _ml_step(q_ref, k_ref, vt_ref, i_row, f_row, upto_ref, upto_t_ref, mask_ref,
             c_ref, n_ref, m_ref, h_ref, reverse):
    L = ML_CHUNK
    logf = jax.nn.log_sigmoid(f_row)
    hi, mid, lo = _split3(logf)
    rid = lax.broadcasted_iota(jnp.int32, (16, L), 0)
    r16 = jnp.where(rid == 0, hi, jnp.where(rid == 1, mid, jnp.where(rid == 2, lo, 0.0)))
    r16 = r16.astype(BF16)
    rsum = _dot(r16, upto_ref[...])
    b_row = rsum[0:1] + rsum[1:2] + rsum[2:3]
    csum = _dot_nt(upto_t_ref[...], r16)
    b_col = csum[:, 0:1] + csum[:, 1:2] + csum[:, 2:3]
    m_prev = m_ref[...]
    a_col = b_col + m_prev
    yield
    dmat = (b_col - b_row + i_row) + mask_ref[...]
    m_t = jnp.maximum(a_col, dmat.max(axis=-1, keepdims=True))
    w_inter = jnp.exp(a_col - m_t)
    yield
    q, k, vt = q_ref[...], k_ref[...], vt_ref[...]
    s = _dot_nt(q, k) * jnp.exp(dmat - m_t)
    yield
    num = w_inter * _dot_nt(q, c_ref[...].astype(BF16)) + _dot_nt(s.astype(BF16), vt)
    qn = _dot_nt(q, n_ref[...].astype(BF16))[:, 0:1]
    den = w_inter * qn + s.sum(axis=-1, keepdims=True)
    h_ref[...] = (num / jnp.maximum(jnp.abs(den), jnp.exp(-m_t))).astype(h_ref.dtype)
    yield
    b_end = b_row[:, 0:1] if reverse else b_row[:, L - 1:L]
    g_row = b_end - b_row + i_row
    m_new = jnp.maximum(b_end + m_prev, g_row.max(axis=-1, keepdims=True))
    decay = jnp.exp(b_end + m_prev - m_new)
    w_row = jnp.exp(g_row - m_new)
    c_ref[...] = decay * c_ref[...] + _dot((vt.astype(F32) * w_row).astype(BF16), k)
    w8 = jnp.broadcast_to(w_row, (8, L)).astype(BF16)
    n_ref[...] = decay * n_ref[...] + _dot(w8, k)
    m_ref[...] = m_new


def _ml_rec_kernel(qf_ref, kf_ref, vtf_ref, gf_ref, qb_ref, kb_ref, vtb_ref, gb_ref,
                   lo_ref, up_ref, mlo_ref, mup_ref,
                   hf_ref, hb_ref, c_ref, n_ref, m_ref):
    @pl.when(pl.program_id(1) == 0)
    def _():
        for r in (c_ref, n_ref, m_ref):
            r[...] = jnp.zeros(r.shape, F32)

    nh = ML_HEADS
    dh = ML_HEAD_DIM
    steps = []
    for hh in range(ML_REC_HEADS):
        h = pl.program_id(0) * ML_REC_HEADS + hh
        cols = pl.ds(hh * dh, dh)
        steps += [
            _ml_step(qf_ref.at[:, cols], kf_ref.at[:, cols], vtf_ref.at[cols, :],
                     gf_ref[pl.ds(h, 1), :], gf_ref[pl.ds(nh + h, 1), :],
                     up_ref, lo_ref, mlo_ref, c_ref.at[2 * hh], n_ref.at[2 * hh],
                     m_ref.at[2 * hh], hf_ref.at[:, cols], reverse=False),
            _ml_step(qb_ref.at[:, cols], kb_ref.at[:, cols], vtb_ref.at[cols, :],
                     gb_ref[pl.ds(2 * nh + h, 1), :], gb_ref[pl.ds(3 * nh + h, 1), :],
                     lo_ref, up_ref, mup_ref, c_ref.at[2 * hh + 1], n_ref.at[2 * hh + 1],
                     m_ref.at[2 * hh + 1], hb_ref.at[:, cols], reverse=True)]
    for _ in itertools.zip_longest(*steps):
        pass


def ml_recurrence(q, k, vt, gates_t):
    L = ML_CHUNK
    dh = ML_HEAD_DIM
    nchunk = TOKENS // L
    fwd = lambda j: j
    bwd = lambda j: jnp.where(j == 0, 0, nchunk - j)
    specs = []
    hw = ML_REC_HEADS * dh
    for cm in (fwd, bwd):
        specs += [pl.BlockSpec((L, hw), lambda h, j, cm=cm: (cm(j), h)),
                  pl.BlockSpec((L, hw), lambda h, j, cm=cm: (cm(j), h)),
                  pl.BlockSpec((hw, L), lambda h, j, cm=cm: (h, cm(j))),
                  pl.BlockSpec((4 * ML_HEADS, L), lambda h, j, cm=cm: (0, cm(j)))]
    nstate = 2 * ML_REC_HEADS
    state = [pltpu.VMEM((nstate, dh, dh), F32), pltpu.VMEM((nstate, 8, dh), F32),
             pltpu.VMEM((nstate, 1, 1), F32)]
    row = np.arange(L)[:, None]
    col = np.arange(L)[None, :]
    consts = [jnp.asarray(row >= col, BF16), jnp.asarray(row <= col, BF16),
              jnp.asarray(np.where(row >= col, 0.0, NEG), F32),
              jnp.asarray(np.where(row <= col, 0.0, NEG), F32)]
    specs += [pl.BlockSpec((L, L), lambda h, j: (0, 0))] * len(consts)
    return pl.pallas_call(
        _ml_rec_kernel,
        out_shape=[jax.ShapeDtypeStruct((TOKENS, ML_WIDTH), BF16)] * 2,
        grid=(ML_HEADS // ML_REC_HEADS, nchunk),
        in_specs=specs,
        out_specs=[pl.BlockSpec((L, hw), lambda h, j: (fwd(j), h)),
                   pl.BlockSpec((L, hw), lambda h, j: (bwd(j), h))],
        scratch_shapes=state,
        compiler_params=_cparams("arbitrary", "arbitrary"),
    )(q, k, vt, gates_t, q, k, vt, gates_t, *consts)


def _ml_out_kernel(hf_ref, hb_ref, xc_ref, z_ref, ng_ref, skip_ref, w_ref, x_ref, gate_ref,
                   g2_ref, sh2_ref, sc2_ref, o_ref, h2_ref, *, tm):
    i = pl.program_id(0)
    kk = pl.program_id(1)
    @pl.when(kk == 0)
    def _():
        o_ref[...] = jnp.zeros(o_ref.shape, F32)

    def readout(r):
        rs = slice(r * ML_STRIP, (r + 1) * ML_STRIP)
        h = hf_ref[rs, :].astype(F32) + hb_ref[rs, :].astype(F32)
        mu = jnp.mean(h, axis=-1, keepdims=True)
        hc = h - mu
        var = jnp.mean(hc * hc, axis=-1, keepdims=True)
        hn = hc * lax.rsqrt(var + EPS) * ng_ref[...]
        return ((hn + skip_ref[...] * xc_ref[rs, :].astype(F32)) * _silu(z_ref[rs, :])).astype(BF16)

    pending = readout(0)
    for r in range(tm // ML_STRIP):
        nxt = readout(r + 1) if (r + 1) * ML_STRIP < tm else None
        o_ref[r * ML_STRIP:(r + 1) * ML_STRIP, :] += _dot(pending, w_ref[...])
        pending = nxt

    @pl.when(kk == pl.num_programs(1) - 1)
    def _():
        strip = 128

        def finish(r, carry):
            rows = pl.ds(pl.multiple_of(r * strip, strip), strip)
            row0 = i * tm + r * strip
            gate = _row_select(row0, strip, gate_ref[1:2, :], gate_ref[0:1, :])
            x = x_ref[rows, :] + gate * o_ref[rows, :]
            o_ref[rows, :] = x
            h2_ref[rows, :] = _norm_mod_rows(x, g2_ref, sh2_ref, sc2_ref, row0)
            return carry

        lax.fori_loop(0, tm // strip, finish, 0)


def ml_out(hf, hb, xconv, xz, norm_g, skip, w_out, xc, mods, next_g, next_mods, tm=768):
    e = ML_WIDTH
    dh = ML_HEAD_DIM
    d = D_MODEL
    nh = ML_HEADS
    hd = lambda i, k: (i, k)
    nargs, nspecs = _norm_specs(next_g, next_mods)
    row = pl.BlockSpec((tm, d), lambda i, k: (i, 0))
    return pl.pallas_call(
        functools.partial(_ml_out_kernel, tm=tm),
        out_shape=[jax.ShapeDtypeStruct((TOKENS, d), F32),
                   jax.ShapeDtypeStruct((TOKENS, d), BF16)],
        grid=(TOKENS // tm, nh),
        in_specs=[pl.BlockSpec((tm, dh), hd), pl.BlockSpec((tm, dh), hd),
                  pl.BlockSpec((tm, dh), hd),
                  pl.BlockSpec((tm, dh), lambda i, k: (i, nh + k)),
                  pl.BlockSpec((1, dh), lambda i, k: (0, k)),
                  pl.BlockSpec((1, dh), lambda i, k: (0, k)),
                  pl.BlockSpec((dh, d), lambda i, k: (k, 0)),
                  row,
                  pl.BlockSpec((8, d), lambda i, k: (0, 2))] + nspecs,
        out_specs=[row, row],
        compiler_params=_cparams("arbitrary", "arbitrary"),
    )(hf, hb, xconv, xz, norm_g.reshape(1, e), skip.reshape(1, e), w_out, xc, mods, *nargs)


def _cmul(ar, ai, br, bi):
    return ar * br - ai * bi, ar * bi + ai * br


def s5_operators(lam_re, lam_im, log_dt, b_re, b_im, c_re, c_im):
    G, P, GS, L = S5_GROUPS, S5_STATE, S5_GROUP, S5_L
    bre, bim = b_re.astype(F32).transpose(0, 2, 1), b_im.astype(F32).transpose(0, 2, 1)
    cre, cim = c_re.astype(F32), c_im.astype(F32)
    tau = jnp.arange(L + 1, dtype=F32)[None, :, None]
    seg_pows = jnp.asarray([1, S5_SEG_CHUNKS], F32) * L
    bcs, ccts, rows = [], [], []
    for j in range(2):
        lr, li = lam_re[j].astype(F32), lam_im[j].astype(F32)
        dt = jnp.exp(log_dt[j].astype(F32))[:, None]
        lrd, lid = (lr * dt)[:, None, :], (li * dt)[:, None, :]
        pr = jnp.exp(lrd * tau) * jnp.cos(lid * tau)
        pi = jnp.exp(lrd * tau) * jnp.sin(lid * tau)
        ar, ai = pr[:, 1], pi[:, 1]
        den = lr * lr + li * li
        kr = (((ar - 1.0) * lr + ai * li) / den)[:, None, :]
        ki = ((ai * lr - (ar - 1.0) * li) / den)[:, None, :]
        bbr = kr * bre - ki * bim
        bbi = kr * bim + ki * bre
        pw = np.arange(L - 1, -1, -1) if j == 0 else np.arange(L)
        er, ei = _cmul(pr[:, pw, None], pi[:, pw, None], bbr[:, None], bbi[:, None])
        bcs.append(jnp.concatenate([er, ei], -1).reshape(G, L * GS, 2 * P))
        pw = np.arange(1, L + 1) if j == 0 else np.arange(L, 0, -1)
        fr, fi = _cmul(cre[:, None], cim[:, None], pr[:, pw, None], pi[:, pw, None])
        ccts.append(jnp.concatenate([fr, -fi], -1).reshape(G, L * GS, 2 * P))
        sp = seg_pows[:, None, None]
        sr = jnp.exp(lr * dt * sp) * jnp.cos(li * dt * sp)
        si = jnp.exp(lr * dt * sp) * jnp.sin(li * dt * sp)
        a1 = jnp.concatenate([sr, sr], -1).reshape(2, 1, G * 2 * P)
        a2 = jnp.concatenate([-si, si], -1).reshape(2, 1, G * 2 * P)
        rows.append(jnp.stack([a1, a2], axis=1))
    c0 = jnp.concatenate([cre, -cim], -1).astype(BF16)
    return jnp.stack(bcs).astype(BF16), c0, jnp.stack(ccts).astype(BF16), jnp.stack(rows)


def _s5_chunk_cat(u_ref, dtype):
    n = u_ref.shape[0] // S5_L
    return jnp.concatenate([u_ref[pl.ds(s, n, stride=S5_L), :].astype(dtype)
                            for s in range(S5_L)], axis=-1)


def _s5_place_group_rows(dst_ref, src_ref, col0):
    for gl in range(S5_TILE_GROUPS):
        for s in range(S5_L):
            r = s * LANES + gl * S5_GROUP
            dst_ref[r:r + S5_GROUP, col0 + gl * LANES:col0 + (gl + 1) * LANES] = (
                src_ref[gl, s * S5_GROUP:(s + 1) * S5_GROUP, :])


def _s5_drive_kernel(u_ref, bcf_ref, bcb_ref, ef_ref, eb_ref, op_ref):
    first = (pl.program_id(0) == 0) & (pl.program_id(1) == 0)

    @pl.when(first)
    def _():
        op_ref[...] = jnp.zeros(op_ref.shape, BF16)

    @pl.when(pl.program_id(1) == 0)
    def _():
        _s5_place_group_rows(op_ref, bcf_ref, 0)
        _s5_place_group_rows(op_ref, bcb_ref, S5_TILE_STATE)

    e = _dot(_s5_chunk_cat(u_ref, BF16), op_ref[...])
    ef_ref[...] = e[:, :S5_TILE_STATE]
    eb_ref[...] = e[:, S5_TILE_STATE:]


def s5_drive(uz, bc):
    rows = TOKENS // S5_ROW_SPLIT
    out = jax.ShapeDtypeStruct((S5_CHUNKS, S5_GROUPS * 2 * S5_STATE), F32)
    ospec = pl.BlockSpec((rows // S5_L, S5_TILE_STATE), lambda q, i: (i, q))

    def bspec(d):
        return pl.BlockSpec((None, S5_TILE_GROUPS, S5_L * S5_GROUP, 2 * S5_STATE),
                            lambda q, i: (d, q, 0, 0))
    return pl.pallas_call(
        _s5_drive_kernel,
        out_shape=[out, out],
        grid=(S5_TILES, S5_ROW_SPLIT),
        in_specs=[pl.BlockSpec((rows, LANES), lambda q, i: (i, q)), bspec(0), bspec(1)],
        out_specs=[ospec, ospec],
        scratch_shapes=[pltpu.VMEM((S5_CAT, 2 * S5_TILE_STATE), BF16)],
        compiler_params=_cparams("arbitrary", "arbitrary"),
    )(uz, bc, bc)


def _s5_scan_kernel(*refs, reverse):
    ng = S5_TILE_GROUPS
    nj = S5_SEG_CHUNKS
    e_refs, a_ref, x_refs = refs[:ng], refs[ng], refs[ng + 1:2 * ng + 1]
    in_ref, es_ref = refs[-2:]
    lanes = [slice(g * LANES, (g + 1) * LANES) for g in range(ng)]
    swap = lambda z: pltpu.roll(z, S5_STATE, 1)
    for g in range(ng):
        es_ref[g] = swap(e_refs[g][...])
    a1 = [a_ref[0, 0][:, sl] for sl in lanes]
    a2 = [a_ref[0, 1][:, sl] for sl in lanes]
    seg0 = lax.broadcasted_iota(jnp.int32, (S5_SEG, LANES), 0) == 0

    def tile(j):
        jj = (nj - 1 - j) if reverse else j
        return jj, pl.ds(jj, S5_SEG, stride=nj)

    def step(j, carry, emit):
        jj, rows = tile(j)
        out = []
        for g in range(ng):
            z, zs = carry[2 * g], carry[2 * g + 1]
            if reverse:
                reset = seg0 & (jj == S5_CTX_CHUNKS - 1)
                z, zs = jnp.where(reset, 0.0, z), jnp.where(reset, 0.0, zs)
            if emit:
                x_refs[g][rows, :] = z
            out += [a1[g] * z + a2[g] * zs + e_refs[g][rows, :],
                    a1[g] * zs - a2[g] * z + es_ref[g, rows, :]]
        return tuple(out)

    zero = tuple(jnp.zeros((S5_SEG, LANES), F32) for _ in range(2 * ng))
    end = lax.fori_loop(0, nj, functools.partial(step, emit=False), zero, unroll=2)[0::2]
    order = range(S5_SEG - 1, -1, -1) if reverse else range(S5_SEG)
    for g, sl in enumerate(lanes):
        cur = end[g][0:1, :] if reverse else jnp.zeros((1, LANES), F32)
        for s in order:
            in_ref[s:s + 1, sl] = cur
            cur = a_ref[1, 0][:, sl] * cur + a_ref[1, 1][:, sl] * swap(cur) + end[g][s:s + 1, :]
    start = []
    for sl in lanes:
        start += [in_ref[:, sl], swap(in_ref[:, sl])]
    lax.fori_loop(0, nj, functools.partial(step, emit=True), tuple(start), unroll=2)


def s5_scan(e, rows, direction):
    ng = S5_TILE_GROUPS
    cols = S5_TILE_STATE
    blks = [pl.BlockSpec((S5_CHUNKS, LANES), lambda i, g=g: (0, ng * i + g)) for g in range(ng)]
    return pl.pallas_call(
        functools.partial(_s5_scan_kernel, reverse=direction == 1),
        out_shape=[jax.ShapeDtypeStruct((S5_CHUNKS, S5_TILES * LANES), F32)] * ng,
        grid=(S5_TILES,),
        in_specs=blks + [pl.BlockSpec((None, 2, 2, 1, cols),
                                      lambda i: (direction, 0, 0, 0, i))],
        out_specs=[pl.BlockSpec((S5_CHUNKS, LANES), lambda i: (0, i))] * ng,
        scratch_shapes=[pltpu.VMEM((S5_SEG, cols), F32),
                        pltpu.VMEM((ng, S5_CHUNKS, LANES), F32)],
        compiler_params=_cparams("arbitrary"),
    )(*([e] * ng), rows)


def _gelu_tanh(x):
    return 0.5 * x * (1.0 + jnp.tanh(math.sqrt(2.0 / math.pi) * (x + 0.044715 * (x * x * x))))


def _s5_read_kernel(u_ref, b0f_ref, b0b_ref, c0_ref, cctf_ref, cctb_ref, d_ref, *refs):
    x_refs = refs[:2 * S5_TILE_GROUPS]
    o_ref, m_ref, c_ref, t_ref, dk_ref = refs[2 * S5_TILE_GROUPS:]
    first = (pl.program_id(0) == 0) & (pl.program_id(1) == 0)
    ts = S5_TILE_STATE

    @pl.when(first)
    def _():
        c_ref[...] = jnp.zeros(c_ref.shape, BF16)
        t_ref[...] = jnp.zeros(t_ref.shape, BF16)

    @pl.when(pl.program_id(1) == 0)
    def _():
        _s5_place_group_rows(c_ref, cctf_ref, 0)
        _s5_place_group_rows(c_ref, cctb_ref, ts)
        for k, src in enumerate((b0f_ref, b0b_ref, c0_ref)):
            for gl in range(S5_TILE_GROUPS):
                t_ref[k, gl * S5_GROUP:(gl + 1) * S5_GROUP, gl * LANES:(gl + 1) * LANES] = src[gl]
        b0f, b0b, c0 = t_ref[0], t_ref[1], t_ref[2]
        dk_ref[S5_L - 1] = (_dot_nt(b0f, c0) + _dot_nt(b0b, c0)).astype(BF16)
        for lag in range(1, S5_L):
            cf = c_ref[(lag - 1) * LANES:lag * LANES, 0:ts]
            dk_ref[S5_L - 1 + lag] = _dot_nt(b0f, cf).astype(BF16)
            cb = c_ref[(S5_L - lag) * LANES:(S5_L - lag + 1) * LANES, ts:2 * ts]
            dk_ref[S5_L - 1 - lag] = _dot_nt(b0b, cb).astype(BF16)
        for s in range(S5_L):
            for t in range(S5_L):
                m_ref[s * LANES:(s + 1) * LANES, t * LANES:(t + 1) * LANES] = (
                    dk_ref[t - s + S5_L - 1])

    n = u_ref.shape[0] // S5_L
    xs = jnp.concatenate([x[...].astype(BF16) for x in x_refs], axis=-1)
    y = _dot(_s5_chunk_cat(u_ref, BF16), m_ref[...]) + _dot_nt(xs, c_ref[...])
    for t in range(S5_L):
        rows = pl.ds(t, n, stride=S5_L)
        o_ref[rows, :] = _gelu_tanh(y[:, t * LANES:(t + 1) * LANES] + d_ref[...] * u_ref[rows, :])


def s5_read(uz, bc, c0, cct, xf, xb, d_skip):
    rows = TOKENS // S5_ROW_SPLIT
    ublk = pl.BlockSpec((rows, LANES), lambda q, i: (i, q))
    xblk = pl.BlockSpec((rows // S5_L, LANES), lambda q, i: (i, q))
    gp = (S5_TILE_GROUPS, S5_GROUP, 2 * S5_STATE)

    def cspec(d):
        return pl.BlockSpec((None, S5_TILE_GROUPS, S5_L * S5_GROUP, 2 * S5_STATE),
                            lambda q, i: (d, q, 0, 0))

    def b0spec(d, s):
        return pl.BlockSpec((None, S5_TILE_GROUPS, None, S5_GROUP, 2 * S5_STATE),
                            lambda q, i: (d, q, s, 0, 0))
    bc5 = bc.reshape(2, S5_GROUPS, S5_L, S5_GROUP, 2 * S5_STATE)
    return pl.pallas_call(
        _s5_read_kernel,
        out_shape=jax.ShapeDtypeStruct((TOKENS, S5_WIDTH), F32),
        grid=(S5_TILES, S5_ROW_SPLIT),
        in_specs=[ublk, b0spec(0, S5_L - 1), b0spec(1, 0),
                  pl.BlockSpec(gp, lambda q, i: (q, 0, 0)),
                  cspec(0), cspec(1),
                  pl.BlockSpec((1, LANES), lambda q, i: (0, q))] + [xblk] * (len(xf) + len(xb)),
        out_specs=ublk,
        scratch_shapes=[pltpu.VMEM((S5_CAT, S5_CAT), BF16),
                        pltpu.VMEM((S5_CAT, 2 * S5_TILE_STATE), BF16),
                        pltpu.VMEM((3, LANES, S5_TILE_STATE), BF16),
                        pltpu.VMEM((2 * S5_L - 1, LANES, LANES), BF16)],
        compiler_params=_cparams("arbitrary", "arbitrary"),
    )(uz, bc5, bc5, c0, cct, cct, d_skip.astype(F32).reshape(1, S5_WIDTH), *xf, *xb)


def _glu_kernel(s_ref, w_ref, b_ref, o_ref, lhs_ref, *, tn):
    j = pl.program_id(1)

    @pl.when(j == 0)
    def _():
        lhs_ref[...] = s_ref[...].astype(BF16)

    v = _dot(lhs_ref[...], w_ref[...]) + b_ref[...]
    sj = s_ref[:, pl.ds(pl.multiple_of(j * tn, tn), tn)]
    o_ref[...] = sj * jax.nn.sigmoid(v)


def s5_glu(s, w_glu, b_glu, tm=768, tn=1024):
    d = S5_WIDTH
    return pl.pallas_call(
        functools.partial(_glu_kernel, tn=tn),
        out_shape=jax.ShapeDtypeStruct((TOKENS, d), F32),
        grid=(TOKENS // tm, d // tn),
        in_specs=[pl.BlockSpec((tm, d), lambda i, j: (i, 0)),
                  pl.BlockSpec((d, tn), lambda i, j: (0, j)),
                  pl.BlockSpec((1, tn), lambda i, j: (0, j))],
        out_specs=pl.BlockSpec((tm, tn), lambda i, j: (i, j)),
        scratch_shapes=[pltpu.VMEM((tm, d), BF16)],
        compiler_params=_cparams("arbitrary", "arbitrary"),
    )(s, w_glu, b_glu.reshape(1, d))


def _block_diag(w, size):
    lane = np.arange(size)
    spread = jnp.asarray(lane[None, :] % ML_BLOCK == np.arange(ML_BLOCK)[:, None], BF16)
    tiled = jnp.dot(w.astype(BF16).reshape(-1, ML_BLOCK), spread, preferred_element_type=BF16)
    same_block = jnp.asarray(lane[:, None] // ML_BLOCK == lane[None, :] // ML_BLOCK)
    return jnp.where(same_block, tiled.reshape(-1, size, size), jnp.zeros((), BF16))


def na_layer(xc, h, mods, tail, w_in, li, rpb, w_out):
    e = D_MODEL
    last = isinstance(tail[0], str)
    qkv = in_proj(h, w_in, li, 0, 3 * e, BF16, scaled_cols=e, scale=NA_Q_SCALE)
    z = in_proj(h, w_in, li, 3 * e, e, F32)
    o = na_attention(qkv, na_pair_table(rpb), with_ctx_out=not last)
    if last:
        return gated_out_final(o, z, 0, w_out.astype(BF16), xc, mods, tail[1])
    tm = CTX_LEN if isinstance(xc, tuple) else 384
    return gated_out(o, z, 0, w_out.astype(BF16), xc, mods, *tail, tm=tm)


def mlstm_layer(xc, h, mods, tail, w_in, li, conv_w, conv_b, wq, wk, wv, w_gate, b_gate, skip,
                norm_g, w_out):
    e = ML_WIDTH
    xz = in_proj(h, w_in, li, 0, 2 * e, F32)
    wq_bd = _block_diag(wq, ML_DIAG)
    wk_bd = _block_diag(wk, ML_DIAG)
    wv_bd = _block_diag(wv, ML_DIAG)
    wvt_bd = _block_diag(wv.transpose(0, 2, 1), ML_DIAG)
    wg_t = w_gate.reshape(3, e, 4 * ML_HEADS).transpose(0, 2, 1).astype(BF16)
    xconv, q, k, vt, gates_t = ml_pre(xz, conv_w, conv_b, wq_bd, wk_bd, wv_bd, wvt_bd, wg_t,
                                      b_gate)
    hf, hb = ml_recurrence(q, k, vt, gates_t)
    return ml_out(hf, hb, xconv, xz, norm_g, skip, w_out.astype(BF16), xc, mods, *tail)


def s5_layer(xc, h, mods, tail, w_in, li, lam_re, lam_im, log_dt, b_re, b_im, c_re, c_im,
             d_skip, w_glu, b_glu, w_out):
    e = S5_WIDTH
    uz = in_proj(h, w_in, li, 0, 2 * e, F32)
    bc, c0, cct, rows = s5_operators(lam_re, lam_im, log_dt, b_re, b_im, c_re, c_im)
    ef, eb = s5_drive(uz, bc)
    xf = s5_scan(ef, rows, 0)
    xb = s5_scan(eb, rows, 1)
    s = s5_read(uz, bc, c0, cct, xf, xb, d_skip)
    s2 = s5_glu(s, w_glu.astype(BF16), b_glu)
    return gated_out(s2, uz, e, w_out.astype(BF16), xc, mods, *tail)


def kernel(x, c, ctx, c_ctx, norm_g, ada_w, ada_b, na_w_in, na_rpb, na_w_out, ml_w_in, ml_conv_w, ml_conv_b, ml_wq, ml_wk, ml_wv, ml_w_gate, ml_b_gate, ml_skip, ml_norm_g, ml_w_out, s5_w_in, s5_lam_re, s5_lam_im, s5_log_dt, s5_b_re, s5_b_im, s5_c_re, s5_c_im, s5_d, s5_w_glu, s5_b_glu, s5_w_out, final_g):
    xc = (ctx[0], x[0])
    c8 = jnp.concatenate([c, c_ctx[None, :], jnp.zeros((6, D_MODEL), F32)], axis=0)
    mods = ada_mods(c8, ada_w, ada_b)
    h = norm_mod(xc, norm_g[0], mods[0], tm=CTX_LEN)
    ia = ib = ic = 0
    assert (DEPTH - 1) % 3 == 0, "the last layer must be the attention layer that ends the trunk"
    for layer in range(DEPTH):
        tail = ((norm_g[layer + 1], mods[layer + 1]) if layer + 1 < DEPTH
                else ('final', final_g))
        kind = layer % 3
        if kind == 0:
            out = na_layer(xc, h, mods[layer], tail, na_w_in, ia, na_rpb[ia], na_w_out[ia])
            ia += 1
        elif kind == 1:
            out = mlstm_layer(xc, h, mods[layer], tail, ml_w_in, ib, ml_conv_w[ib],
                              ml_conv_b[ib], ml_wq[ib], ml_wk[ib], ml_wv[ib], ml_w_gate[ib],
                              ml_b_gate[ib], ml_skip[ib], ml_norm_g[ib], ml_w_out[ib])
            ib += 1
        else:
            out = s5_layer(xc, h, mods[layer], tail, s5_w_in, ic, s5_lam_re[ic],
                           s5_lam_im[ic], s5_log_dt[ic], s5_b_re[ic], s5_b_im[ic], s5_c_re[ic],
                           s5_c_im[ic], s5_d[ic], s5_w_glu[ic], s5_b_glu[ic], s5_w_out[ic])
            ic += 1
        if layer + 1 < DEPTH:
            xc, h = out
    return out[None]
```

```python
import functools
import itertools
import math

import numpy as np
import jax
import jax.numpy as jnp
from jax import lax
from jax.experimental import pallas as pl
from jax.experimental.pallas import tpu as pltpu

F32 = jnp.float32
BF16 = jnp.bfloat16

D_MODEL = 2048
SEQ = 8192
CTX_LEN = 256
TOKENS = CTX_LEN + SEQ
DEPTH = 4
EPS = 1e-6
NEG = -1e30

GRID_W = 64
GRID_ROWS = SEQ // GRID_W
NA_HEADS = 16
NA_HEAD_DIM = 128
WIN_H = 8
WIN_W = 16
NA_QROWS = 4
NA_BLK = NA_QROWS * GRID_W
NA_WIN_BLKS = 3
NA_HEADS_PER_STEP = 8
LOG2E = math.log2(math.e)
NA_Q_SCALE = NA_HEAD_DIM ** -0.5 * LOG2E

ML_WIDTH = 2 * D_MODEL
ML_HEADS = 8
ML_HEAD_DIM = ML_WIDTH // ML_HEADS
ML_BLOCK = 4
ML_CHUNK = 256
ML_REC_HEADS = 4
ML_COLS = 512
ML_DIAG = 256
ML_STRIP = 256

S5_WIDTH = D_MODEL
S5_GROUP = 16
S5_GROUPS = S5_WIDTH // S5_GROUP
S5_STATE = 64
S5_L = 16
S5_SEG = 8
S5_CHUNKS = TOKENS // S5_L
S5_CTX_CHUNKS = CTX_LEN // S5_L
S5_SEG_CHUNKS = S5_CHUNKS // S5_SEG
LANES = 128
S5_TILE_GROUPS = LANES // S5_GROUP
S5_TILES = S5_WIDTH // LANES
S5_CAT = S5_L * LANES
S5_TILE_STATE = S5_TILE_GROUPS * 2 * S5_STATE
S5_ROW_SPLIT = 2

VMEM_LIMIT = 56 * 1024 * 1024


def _cparams(*sem):
    return pltpu.CompilerParams(dimension_semantics=sem, vmem_limit_bytes=VMEM_LIMIT)


def _dot(a, b):
    return jnp.dot(a, b, preferred_element_type=F32)


def _dot_nt(a, b):
    return lax.dot_general(a, b, (((1,), (1,)), ((), ())), preferred_element_type=F32)


def _silu(x):
    return x * jax.nn.sigmoid(x)


def _row_select(row0, n, ctx_row, x_row):
    rows = row0 + lax.broadcasted_iota(jnp.int32, (n, 1), 0)
    return jnp.where(rows < CTX_LEN, ctx_row, x_row)


def _ada_kernel(c_ref, w_ref, b_ref, o_ref):
    cs = _silu(c_ref[...]).astype(BF16)
    o_ref[0] = _dot(cs, w_ref[0].astype(BF16)) + b_ref[0]


def ada_mods(c8, ada_w, ada_b):
    tn = 512
    n = 3 * D_MODEL
    return pl.pallas_call(
        _ada_kernel,
        out_shape=jax.ShapeDtypeStruct((DEPTH, 8, n), F32),
        grid=(DEPTH, n // tn),
        in_specs=[pl.BlockSpec((8, D_MODEL), lambda l, j: (0, 0)),
                  pl.BlockSpec((1, D_MODEL, tn), lambda l, j: (l, 0, j)),
                  pl.BlockSpec((1, 1, tn), lambda l, j: (l, 0, j))],
        out_specs=pl.BlockSpec((1, 8, tn), lambda l, j: (l, 0, j)),
        compiler_params=_cparams("arbitrary", "arbitrary"),
    )(c8, ada_w, ada_b.reshape(DEPTH, 1, n))


def _rmsnorm(x, g):
    return x * lax.rsqrt(jnp.mean(x * x, axis=-1, keepdims=True) + EPS) * g


def _norm_mod_rows(x, g_ref, sh_ref, sc_ref, row0):
    sc = _row_select(row0, x.shape[0], sc_ref[1:2, :], sc_ref[0:1, :])
    sh = _row_select(row0, x.shape[0], sh_ref[1:2, :], sh_ref[0:1, :])
    return (_rmsnorm(x, g_ref[...]) * (1.0 + sc) + sh).astype(BF16)


def _norm_specs(g, mods):
    d = D_MODEL
    return ([g.reshape(1, d), mods, mods],
            [pl.BlockSpec((1, d), lambda *a: (0, 0)), pl.BlockSpec((8, d), lambda *a: (0, 0)),
             pl.BlockSpec((8, d), lambda *a: (0, 1))])


def _stream_specs(xc, tm):
    d = D_MODEL
    if not isinstance(xc, tuple):
        return [xc], [pl.BlockSpec((tm, d), lambda i: (i, 0))]
    assert tm == CTX_LEN
    return list(xc), [pl.BlockSpec((tm, d), lambda i: (0, 0)),
                      pl.BlockSpec((tm, d), lambda i: (jnp.maximum(i - 1, 0), 0))]


def _stream_tile(refs):
    if len(refs) == 1:
        return refs[0][...]
    return jnp.where(pl.program_id(0) == 0, refs[0][...], refs[1][...])


def _norm_mod_kernel(*refs, tm):
    g_ref, sh_ref, sc_ref, o_ref = refs[-4:]
    o_ref[...] = _norm_mod_rows(_stream_tile(refs[:-4]), g_ref, sh_ref, sc_ref,
                                pl.program_id(0) * tm)


def norm_mod(xc, g, mods, tm=384):
    d = D_MODEL
    nargs, nspecs = _norm_specs(g, mods)
    sargs, sspecs = _stream_specs(xc, tm)
    return pl.pallas_call(
        functools.partial(_norm_mod_kernel, tm=tm),
        out_shape=jax.ShapeDtypeStruct((TOKENS, d), BF16),
        grid=(TOKENS // tm,),
        in_specs=sspecs + nspecs,
        out_specs=pl.BlockSpec((tm, d), lambda i: (i, 0)),
        compiler_params=_cparams("arbitrary"),
    )(*sargs, *nargs)


def _in_proj_kernel(h_ref, w_ref, o_ref, wb_ref, *, scaled_blocks, scale):
    j = pl.program_id(0)

    @pl.when(pl.program_id(1) == 0)
    def _():
        wb_ref[...] = w_ref[...].astype(BF16)

    acc = _dot(h_ref[...], wb_ref[...])
    if scaled_blocks:
        acc = acc * jnp.where(j < scaled_blocks, scale, 1.0)
    o_ref[...] = acc.astype(o_ref.dtype)


def in_proj(h, w, layer, col0, ncols, out_dtype, scaled_cols=0, scale=1.0, tm=1408, tn=1024):
    d = D_MODEL
    cb = col0 // tn
    return pl.pallas_call(
        functools.partial(_in_proj_kernel, scaled_blocks=scaled_cols // tn, scale=scale),
        out_shape=jax.ShapeDtypeStruct((TOKENS, ncols), out_dtype),
        grid=(ncols // tn, TOKENS // tm),
        in_specs=[pl.BlockSpec((tm, d), lambda j, i: (i, 0)),
                  pl.BlockSpec((None, d, tn), lambda j, i: (layer, 0, cb + j))],
        out_specs=pl.BlockSpec((tm, tn), lambda j, i: (i, j)),
        scratch_shapes=[pltpu.VMEM((d, tn), BF16)],
        compiler_params=_cparams("arbitrary", "arbitrary"),
    )(h, w)


def _gated_out_kernel(a_ref, z_ref, w_ref, gate_ref, g_ref, sh_ref, sc_ref, *refs, tm):
    o_ref, h_ref = refs[-2:]
    i = pl.program_id(0)
    lhs = (a_ref[...].astype(F32) * _silu(z_ref[...].astype(F32))).astype(BF16)
    gate = _row_select(i * tm, tm, gate_ref[1:2, :], gate_ref[0:1, :])
    x = _stream_tile(refs[:-2]) + gate * _dot(lhs, w_ref[...])
    o_ref[...] = x
    h_ref[...] = _norm_mod_rows(x, g_ref, sh_ref, sc_ref, i * tm)


def gated_out(a, zsrc, zcol0, w, xc, mods, next_g, next_mods, tm=384):
    k = a.shape[1]
    d = D_MODEL
    zb = zcol0 // k
    nargs, nspecs = _norm_specs(next_g, next_mods)
    sargs, sspecs = _stream_specs(xc, tm)
    row = pl.BlockSpec((tm, d), lambda i: (i, 0))
    return pl.pallas_call(
        functools.partial(_gated_out_kernel, tm=tm),
        out_shape=[jax.ShapeDtypeStruct((TOKENS, d), F32),
                   jax.ShapeDtypeStruct((TOKENS, d), BF16)],
        grid=(TOKENS // tm,),
        in_specs=[pl.BlockSpec((tm, k), lambda i: (i, 0)),
                  pl.BlockSpec((tm, k), lambda i: (i, zb)),
                  pl.BlockSpec((k, d), lambda i: (0, 0)),
                  pl.BlockSpec((8, d), lambda i: (0, 2))] + nspecs + sspecs,
        out_specs=[row, row],
        compiler_params=_cparams("arbitrary"),
    )(a, zsrc, w, mods, *nargs, *sargs)


def _gated_out_final_kernel(a_ref, z_ref, w_ref, x_ref, gate_ref, g_ref, o_ref):
    lhs = (a_ref[...].astype(F32) * _silu(z_ref[...].astype(F32))).astype(BF16)
    x = x_ref[...] + gate_ref[0:1, :] * _dot(lhs, w_ref[...])
    o_ref[...] = _rmsnorm(x, g_ref[...])


def gated_out_final(a, zsrc, zcol0, w, xc, mods, final_g, tm=256):
    k = a.shape[1]
    d = D_MODEL
    zb = zcol0 // k
    off = CTX_LEN // tm
    return pl.pallas_call(
        _gated_out_final_kernel,
        out_shape=jax.ShapeDtypeStruct((SEQ, d), F32),
        grid=(SEQ // tm,),
        in_specs=[pl.BlockSpec((tm, k), lambda i: (i, 0)),
                  pl.BlockSpec((tm, k), lambda i: (i + off, zb)),
                  pl.BlockSpec((k, d), lambda i: (0, 0)),
                  pl.BlockSpec((tm, d), lambda i: (i + off, 0)),
                  pl.BlockSpec((8, d), lambda i: (0, 2)),
                  pl.BlockSpec((1, d), lambda i: (0, 0))],
        out_specs=pl.BlockSpec((tm, d), lambda i: (i, 0)),
        compiler_params=_cparams("arbitrary"),
    )(a, zsrc, w, xc, mods, final_g.reshape(1, d))


NA_REL_ROWS = 2 * WIN_H - 1
NA_A_LO = WIN_H - 1 - WIN_H // 2
NA_A_HI = NA_A_LO + WIN_H - 1
NA_PAIR_BOTH = NA_REL_ROWS - 1
NA_PAIR_LEFT_MASKED = NA_PAIR_BOTH
NA_PAIR_RIGHT_MASKED = NA_PAIR_BOTH + 1
NA_PAIR_MASKED = NA_PAIR_BOTH + 2
NA_PAIR_ENTRIES = NA_PAIR_BOTH + 3
NA_BAND_ROWS = NA_WIN_BLKS * NA_QROWS


def _na_pair_plan():
    nblk = GRID_ROWS // NA_QROWS
    plan = np.full((4, NA_QROWS, NA_BAND_ROWS // 2), NA_PAIR_MASKED, np.int32)
    for kind, g in ((0, 0), (1, 1), (2, nblk - 1)):
        band0 = NA_QROWS * int(np.clip(g - 1, 0, nblk - NA_WIN_BLKS))
        for i in range(NA_QROWS):
            rq = NA_QROWS * g + i
            r0 = int(np.clip(rq - WIN_H // 2, 0, GRID_ROWS - WIN_H))
            rel = [rk - rq + WIN_H - 1 if r0 <= rk < r0 + WIN_H else None
                   for rk in range(band0, band0 + NA_BAND_ROWS)]
            for p in range(NA_BAND_ROWS // 2):
                lo, hi = rel[2 * p], rel[2 * p + 1]
                if lo is not None and hi is not None:
                    plan[kind, i, p] = lo
                elif hi is not None:
                    assert hi == NA_A_LO
                    plan[kind, i, p] = NA_PAIR_LEFT_MASKED
                elif lo is not None:
                    assert lo == NA_A_HI
                    plan[kind, i, p] = NA_PAIR_RIGHT_MASKED
    return plan


def na_pair_table(rpb):
    cq = np.arange(GRID_W)[:, None]
    ck = np.arange(GRID_W)[None, :]
    c0 = np.clip(cq - WIN_W // 2, 0, GRID_W - WIN_W)
    ok = (ck >= c0) & (ck < c0 + WIN_W)
    rp = jnp.pad(rpb.astype(F32), ((0, 0), (0, 0), (GRID_W, GRID_W + 1)))
    m = rp.shape[-1]
    skew = jnp.tile(rp, (1, 1, GRID_W))[:, :, :GRID_W * (m - 1)]
    skew = skew.reshape(NA_HEADS, NA_REL_ROWS, GRID_W, m - 1)
    t = skew[:, :, :, GRID_W + WIN_W - 1:2 * GRID_W + WIN_W - 1]
    t = jnp.where(jnp.asarray(ok), t * LOG2E, NEG)
    masked = jnp.full((NA_HEADS, 1, GRID_W, GRID_W), NEG, F32)
    left = jnp.concatenate([t[:, :-1], masked, t[:, NA_A_HI:NA_A_HI + 1], masked], axis=1)
    right = jnp.concatenate([t[:, 1:], t[:, NA_A_LO:NA_A_LO + 1], masked, masked], axis=1)
    return jnp.concatenate([left, right], axis=-1)


def _na_kernel(q_ref, kc_ref, k0_ref, k1_ref, k2_ref, vc_ref, v0_ref, v1_ref, v2_ref,
               pair_ref, o_ref, bias_ref, *, heads, q0, nq):
    qb = pl.program_id(1) + q0
    plan = _na_pair_plan()
    pw = 2 * GRID_W

    for kind, first in ((3, 0), (0, 1), (1, 2), (2, nq - 1)):
        @pl.when(qb == first)
        def _(kind=kind):
            for h in range(heads):
                for i in range(NA_QROWS):
                    for p in range(NA_BAND_ROWS // 2):
                        bias_ref[h, i * GRID_W:(i + 1) * GRID_W, p * pw:(p + 1) * pw] = (
                            pair_ref[h, int(plan[kind, i, p])])

    for h in range(heads):
        sl = slice(h * NA_HEAD_DIM, (h + 1) * NA_HEAD_DIM)
        q = q_ref[:, sl]
        s = [_dot_nt(q, kc_ref[:, sl])]
        for b, k_ref in enumerate((k0_ref, k1_ref, k2_ref)):
            s.append(_dot_nt(q, k_ref[:, sl]) + bias_ref[h, :, b * NA_BLK:(b + 1) * NA_BLK])
        m = s[0].max(axis=-1, keepdims=True)
        for t in s[1:]:
            m = jnp.maximum(m, t.max(axis=-1, keepdims=True))
        p = [jnp.exp2(t - m) for t in s]
        den = p[0].sum(axis=-1, keepdims=True)
        for t in p[1:]:
            den = den + t.sum(axis=-1, keepdims=True)
        p_all = jnp.concatenate([t.astype(BF16) for t in p], axis=1)
        v_all = jnp.concatenate([v_ref[:, sl] for v_ref in (vc_ref, v0_ref, v1_ref, v2_ref)],
                                axis=0)
        o_ref[:, sl] = (_dot(p_all, v_all) / den).astype(o_ref.dtype)


def na_attention(qkv, pair_tab, with_ctx_out):
    hs = NA_HEADS_PER_STEP
    hw = hs * NA_HEAD_DIM
    nh = NA_HEADS // hs
    nq = TOKENS // NA_BLK
    q0 = 0 if with_ctx_out else 1
    nlat = GRID_ROWS // NA_QROWS

    def band(qb):
        return 1 + jnp.clip(qb - 2, 0, nlat - NA_WIN_BLKS)

    def spec(col_base, fn):
        return pl.BlockSpec((NA_BLK, hw), lambda h, g: (fn(g + q0), col_base + h))

    kspecs = [spec(nh, lambda qb: 0 * qb)] + [
        spec(nh, lambda qb, b=b: band(qb) + b) for b in range(NA_WIN_BLKS)]
    vspecs = [spec(2 * nh, lambda qb: 0 * qb)] + [
        spec(2 * nh, lambda qb, b=b: band(qb) + b) for b in range(NA_WIN_BLKS)]
    return pl.pallas_call(
        functools.partial(_na_kernel, heads=hs, q0=q0, nq=nq),
        out_shape=jax.ShapeDtypeStruct(((nq - q0) * NA_BLK, D_MODEL), BF16),
        grid=(nh, nq - q0),
        in_specs=[spec(0, lambda qb: qb)] + kspecs + vspecs + [
            pl.BlockSpec((hs, NA_PAIR_ENTRIES, GRID_W, 2 * GRID_W),
                         lambda h, g: (h, 0, 0, 0))],
        out_specs=pl.BlockSpec((NA_BLK, hw), lambda h, g: (g, h)),
        scratch_shapes=[pltpu.VMEM((hs, NA_BLK, NA_WIN_BLKS * NA_BLK), F32)],
        compiler_params=_cparams("arbitrary", "arbitrary"),
    )(*([qkv] * 9), pair_tab)


def _ml_pre_kernel(x_ref, prev_ref, next_ref, cw_ref, cb_ref, wq_ref, wk_ref, wv_ref, wvt_ref,
                   wg_ref, bg_ref, xc_ref, q_ref, k_ref, vt_ref, g_ref, *, tm):
    i = pl.program_id(0)
    j = pl.program_id(1)
    x = x_ref[...]
    rows = i * tm + lax.broadcasted_iota(jnp.int32, (tm, 1), 0)
    local = lax.broadcasted_iota(jnp.int32, (tm, 1), 0)
    x_prev = jnp.where(local == 0, prev_ref[7:8, :], pltpu.roll(x, 1, 0))
    x_prev = jnp.where((rows == 0) | (rows == CTX_LEN), 0.0, x_prev)
    x_next = jnp.where(local == tm - 1, next_ref[0:1, :], pltpu.roll(x, tm - 1, 0))
    x_next = jnp.where((rows == CTX_LEN - 1) | (rows == TOKENS - 1), 0.0, x_next)
    @pl.when(j == 0)
    def _():
        g_ref[...] = jnp.broadcast_to(bg_ref[...], g_ref.shape)

    def activate(r):
        rs = slice(r * ML_STRIP, (r + 1) * ML_STRIP)
        conv = (cb_ref[...] + x_prev[rs] * cw_ref[0:1, :] + x[rs] * cw_ref[1:2, :]
                + x_next[rs] * cw_ref[2:3, :])
        xcb = _silu(conv).astype(BF16)
        xc_ref[rs, :] = xcb
        return xcb, x[rs].astype(BF16)

    def project(r, xcb, xb):
        rs = slice(r * ML_STRIP, (r + 1) * ML_STRIP)
        part = None
        for t in range(ML_COLS // ML_DIAG):
            cs = slice(t * ML_DIAG, (t + 1) * ML_DIAG)
            q = _dot(xcb[:, cs], wq_ref[t])
            k = _dot(xcb[:, cs], wk_ref[t])
            v = _dot(xb[:, cs], wv_ref[t])
            qb, kb, vb = q.astype(BF16), k.astype(BF16), v.astype(BF16)
            q_ref[rs, cs] = qb
            k_ref[rs, cs] = (k * (ML_HEAD_DIM ** -0.5)).astype(BF16)
            vt_ref[cs, rs] = _dot_nt(wvt_ref[t], xb[:, cs]).astype(BF16)
            gt = (_dot_nt(wg_ref[0, :, cs], qb) + _dot_nt(wg_ref[1, :, cs], kb)
                  + _dot_nt(wg_ref[2, :, cs], vb))
            part = gt if part is None else part + gt
        g_ref[:, rs] += part

    pending = activate(0)
    for r in range(tm // ML_STRIP):
        nxt = activate(r + 1) if (r + 1) * ML_STRIP < tm else None
        project(r, *pending)
        pending = nxt


def ml_pre(xz, conv_w, conv_b, wq_bd, wk_bd, wv_bd, wvt_bd, wg_t, b_gate, tm=768):
    e = ML_WIDTH
    c = ML_COLS
    nc = e // c
    last8 = TOKENS // 8 - 1
    tok = lambda i, j: (i, j)
    outs = pl.pallas_call(
        functools.partial(_ml_pre_kernel, tm=tm),
        out_shape=[jax.ShapeDtypeStruct((TOKENS, e), BF16),
                   jax.ShapeDtypeStruct((TOKENS, e), BF16),
                   jax.ShapeDtypeStruct((TOKENS, e), BF16),
                   jax.ShapeDtypeStruct((e, TOKENS), BF16),
                   jax.ShapeDtypeStruct((4 * ML_HEADS, TOKENS), F32)],
        grid=(TOKENS // tm, nc),
        in_specs=[pl.BlockSpec((tm, c), tok),
                  pl.BlockSpec((8, c), lambda i, j: (jnp.maximum(i * (tm // 8) - 1, 0), j)),
                  pl.BlockSpec((8, c), lambda i, j: (jnp.minimum((i + 1) * (tm // 8), last8), j)),
                  pl.BlockSpec((3, c), lambda i, j: (0, j)),
                  pl.BlockSpec((1, c), lambda i, j: (0, j)),
                  pl.BlockSpec((c // ML_DIAG, ML_DIAG, ML_DIAG), lambda i, j: (j, 0, 0)),
                  pl.BlockSpec((c // ML_DIAG, ML_DIAG, ML_DIAG), lambda i, j: (j, 0, 0)),
                  pl.BlockSpec((c // ML_DIAG, ML_DIAG, ML_DIAG), lambda i, j: (j, 0, 0)),
                  pl.BlockSpec((c // ML_DIAG, ML_DIAG, ML_DIAG), lambda i, j: (j, 0, 0)),
                  pl.BlockSpec((3, 4 * ML_HEADS, c), lambda i, j: (0, 0, j)),
                  pl.BlockSpec((4 * ML_HEADS, 1), lambda i, j: (0, 0))],
        out_specs=[pl.BlockSpec((tm, c), tok), pl.BlockSpec((tm, c), tok),
                   pl.BlockSpec((tm, c), tok),
                   pl.BlockSpec((c, tm), lambda i, j: (j, i)),
                   pl.BlockSpec((4 * ML_HEADS, tm), lambda i, j: (0, i))],
        compiler_params=_cparams("arbitrary", "arbitrary"),
    )(xz, xz, xz, conv_w, conv_b.reshape(1, e), wq_bd, wk_bd, wv_bd, wvt_bd, wg_t,
      b_gate.reshape(4 * ML_HEADS, 1))
    return outs


def _split3(x):
    hi = x.astype(BF16).astype(F32)
    r1 = x - hi
    mid = r1.astype(BF16).astype(F32)
    lo = (r1 - mid).astype(BF16).astype(F32)
    return hi, mid, lo


def _ml_step(q_ref, k_ref, vt_ref, i_row, f_row, upto_ref, upto_t_ref, mask_ref,
             c_ref, n_ref, m_ref, h_ref, reverse):
    L = ML_CHUNK
    logf = jax.nn.log_sigmoid(f_row)
    hi, mid, lo = _split3(logf)
    rid = lax.broadcasted_iota(jnp.int32, (16, L), 0)
    r16 = jnp.where(rid == 0, hi, jnp.where(rid == 1, mid, jnp.where(rid == 2, lo, 0.0)))
    r16 = r16.astype(BF16)
    rsum = _dot(r16, upto_ref[...])
    b_row = rsum[0:1] + rsum[1:2] + rsum[2:3]
    csum = _dot_nt(upto_t_ref[...], r16)
    b_col = csum[:, 0:1] + csum[:, 1:2] + csum[:, 2:3]
    m_prev = m_ref[...]
    a_col = b_col + m_prev
    yield
    dmat = (b_col - b_row + i_row) + mask_ref[...]
    m_t = jnp.maximum(a_col, dmat.max(axis=-1, keepdims=True))
    w_inter = jnp.exp(a_col - m_t)
    yield
    q, k, vt = q_ref[...], k_ref[...], vt_ref[...]
    s = _dot_nt(q, k) * jnp.exp(dmat - m_t)
    yield
    num = w_inter * _dot_nt(q, c_ref[...].astype(BF16)) + _dot_nt(s.astype(BF16), vt)
    qn = _dot_nt(q, n_ref[...].astype(BF16))[:, 0:1]
    den = w_inter * qn + s.sum(axis=-1, keepdims=True)
    h_ref[...] = (num / jnp.maximum(jnp.abs(den), jnp.exp(-m_t))).astype(h_ref.dtype)
    yield
    b_end = b_row[:, 0:1] if reverse else b_row[:, L - 1:L]
    g_row = b_end - b_row + i_row
    m_new = jnp.maximum(b_end + m_prev, g_row.max(axis=-1, keepdims=True))
    decay = jnp.exp(b_end + m_prev - m_new)
    w_row = jnp.exp(g_row - m_new)
    c_ref[...] = decay * c_ref[...] + _dot((vt.astype(F32) * w_row).astype(BF16), k)
    w8 = jnp.broadcast_to(w_row, (8, L)).astype(BF16)
    n_ref[...] = decay * n_ref[...] + _dot(w8, k)
    m_ref[...] = m_new


def _ml_rec_kernel(qf_ref, kf_ref, vtf_ref, gf_ref, qb_ref, kb_ref, vtb_ref, gb_ref,
                   lo_ref, up_ref, mlo_ref, mup_ref,
                   hf_ref, hb_ref, c_ref, n_ref, m_ref):
    @pl.when(pl.program_id(1) == 0)
    def _():
        for r in (c_ref, n_ref, m_ref):
            r[...] = jnp.zeros(r.shape, F32)

    nh = ML_HEADS
    dh = ML_HEAD_DIM
    steps = []
    for hh in range(ML_REC_HEADS):
        h = pl.program_id(0) * ML_REC_HEADS + hh
        cols = pl.ds(hh * dh, dh)
        steps += [
            _ml_step(qf_ref.at[:, cols], kf_ref.at[:, cols], vtf_ref.at[cols, :],
                     gf_ref[pl.ds(h, 1), :], gf_ref[pl.ds(nh + h, 1), :],
                     up_ref, lo_ref, mlo_ref, c_ref.at[2 * hh], n_ref.at[2 * hh],
                     m_ref.at[2 * hh], hf_ref.at[:, cols], reverse=False),
            _ml_step(qb_ref.at[:, cols], kb_ref.at[:, cols], vtb_ref.at[cols, :],
                     gb_ref[pl.ds(2 * nh + h, 1), :], gb_ref[pl.ds(3 * nh + h, 1), :],
                     lo_ref, up_ref, mup_ref, c_ref.at[2 * hh + 1], n_ref.at[2 * hh + 1],
                     m_ref.at[2 * hh + 1], hb_ref.at[:, cols], reverse=True)]
    for _ in itertools.zip_longest(*steps):
        pass


def ml_recurrence(q, k, vt, gates_t):
    L = ML_CHUNK
    dh = ML_HEAD_DIM
    nchunk = TOKENS // L
    fwd = lambda j: j
    bwd = lambda j: jnp.where(j == 0, 0, nchunk - j)
    specs = []
    hw = ML_REC_HEADS * dh
    for cm in (fwd, bwd):
        specs += [pl.BlockSpec((L, hw), lambda h, j, cm=cm: (cm(j), h)),
                  pl.BlockSpec((L, hw), lambda h, j, cm=cm: (cm(j), h)),
                  pl.BlockSpec((hw, L), lambda h, j, cm=cm: (h, cm(j))),
                  pl.BlockSpec((4 * ML_HEADS, L), lambda h, j, cm=cm: (0, cm(j)))]
    nstate = 2 * ML_REC_HEADS
    state = [pltpu.VMEM((nstate, dh, dh), F32), pltpu.VMEM((nstate, 8, dh), F32),
             pltpu.VMEM((nstate, 1, 1), F32)]
    row = np.arange(L)[:, None]
    col = np.arange(L)[None, :]
    consts = [jnp.asarray(row >= col, BF16), jnp.asarray(row <= col, BF16),
              jnp.asarray(np.where(row >= col, 0.0, NEG), F32),
              jnp.asarray(np.where(row <= col, 0.0, NEG), F32)]
    specs += [pl.BlockSpec((L, L), lambda h, j: (0, 0))] * len(consts)
    return pl.pallas_call(
        _ml_rec_kernel,
        out_shape=[jax.ShapeDtypeStruct((TOKENS, ML_WIDTH), BF16)] * 2,
        grid=(ML_HEADS // ML_REC_HEADS, nchunk),
        in_specs=specs,
        out_specs=[pl.BlockSpec((L, hw), lambda h, j: (fwd(j), h)),
                   pl.BlockSpec((L, hw), lambda h, j: (bwd(j), h))],
        scratch_shapes=state,
        compiler_params=_cparams("arbitrary", "arbitrary"),
    )(q, k, vt, gates_t, q, k, vt, gates_t, *consts)


def _ml_out_kernel(hf_ref, hb_ref, xc_ref, z_ref, ng_ref, skip_ref, w_ref, x_ref, gate_ref,
                   g2_ref, sh2_ref, sc2_ref, o_ref, h2_ref, *, tm):
    i = pl.program_id(0)
    kk = pl.program_id(1)
    @pl.when(kk == 0)
    def _():
        o_ref[...] = jnp.zeros(o_ref.shape, F32)

    def readout(r):
        rs = slice(r * ML_STRIP, (r + 1) * ML_STRIP)
        h = hf_ref[rs, :].astype(F32) + hb_ref[rs, :].astype(F32)
        mu = jnp.mean(h, axis=-1, keepdims=True)
        hc = h - mu
        var = jnp.mean(hc * hc, axis=-1, keepdims=True)
        hn = hc * lax.rsqrt(var + EPS) * ng_ref[...]
        return ((hn + skip_ref[...] * xc_ref[rs, :].astype(F32)) * _silu(z_ref[rs, :])).astype(BF16)

    pending = readout(0)
    for r in range(tm // ML_STRIP):
        nxt = readout(r + 1) if (r + 1) * ML_STRIP < tm else None
        o_ref[r * ML_STRIP:(r + 1) * ML_STRIP, :] += _dot(pending, w_ref[...])
        pending = nxt

    @pl.when(kk == pl.num_programs(1) - 1)
    def _():
        strip = 128

        def finish(r, carry):
            rows = pl.ds(pl.multiple_of(r * strip, strip), strip)
            row0 = i * tm + r * strip
            gate = _row_select(row0, strip, gate_ref[1:2, :], gate_ref[0:1, :])
            x = x_ref[rows, :] + gate * o_ref[rows, :]
            o_ref[rows, :] = x
            h2_ref[rows, :] = _norm_mod_rows(x, g2_ref, sh2_ref, sc2_ref, row0)
            return carry

        lax.fori_loop(0, tm // strip, finish, 0)


def ml_out(hf, hb, xconv, xz, norm_g, skip, w_out, xc, mods, next_g, next_mods, tm=768):
    e = ML_WIDTH
    dh = ML_HEAD_DIM
    d = D_MODEL
    nh = ML_HEADS
    hd = lambda i, k: (i, k)
    nargs, nspecs = _norm_specs(next_g, next_mods)
    row = pl.BlockSpec((tm, d), lambda i, k: (i, 0))
    return pl.pallas_call(
        functools.partial(_ml_out_kernel, tm=tm),
        out_shape=[jax.ShapeDtypeStruct((TOKENS, d), F32),
                   jax.ShapeDtypeStruct((TOKENS, d), BF16)],
        grid=(TOKENS // tm, nh),
        in_specs=[pl.BlockSpec((tm, dh), hd), pl.BlockSpec((tm, dh), hd),
                  pl.BlockSpec((tm, dh), hd),
                  pl.BlockSpec((tm, dh), lambda i, k: (i, nh + k)),
                  pl.BlockSpec((1, dh), lambda i, k: (0, k)),
                  pl.BlockSpec((1, dh), lambda i, k: (0, k)),
                  pl.BlockSpec((dh, d), lambda i, k: (k, 0)),
                  row,
                  pl.BlockSpec((8, d), lambda i, k: (0, 2))] + nspecs,
        out_specs=[row, row],
        compiler_params=_cparams("arbitrary", "arbitrary"),
    )(hf, hb, xconv, xz, norm_g.reshape(1, e), skip.reshape(1, e), w_out, xc, mods, *nargs)


def _cmul(ar, ai, br, bi):
    return ar * br - ai * bi, ar * bi + ai * br


def s5_operators(lam_re, lam_im, log_dt, b_re, b_im, c_re, c_im):
    G, P, GS, L = S5_GROUPS, S5_STATE, S5_GROUP, S5_L
    bre, bim = b_re.astype(F32).transpose(0, 2, 1), b_im.astype(F32).transpose(0, 2, 1)
    cre, cim = c_re.astype(F32), c_im.astype(F32)
    tau = jnp.arange(L + 1, dtype=F32)[None, :, None]
    seg_pows = jnp.asarray([1, S5_SEG_CHUNKS], F32) * L
    bcs, ccts, rows = [], [], []
    for j in range(2):
        lr, li = lam_re[j].astype(F32), lam_im[j].astype(F32)
        dt = jnp.exp(log_dt[j].astype(F32))[:, None]
        lrd, lid = (lr * dt)[:, None, :], (li * dt)[:, None, :]
        pr = jnp.exp(lrd * tau) * jnp.cos(lid * tau)
        pi = jnp.exp(lrd * tau) * jnp.sin(lid * tau)
        ar, ai = pr[:, 1], pi[:, 1]
        den = lr * lr + li * li
        kr = (((ar - 1.0) * lr + ai * li) / den)[:, None, :]
        ki = ((ai * lr - (ar - 1.0) * li) / den)[:, None, :]
        bbr = kr * bre - ki * bim
        bbi = kr * bim + ki * bre
        pw = np.arange(L - 1, -1, -1) if j == 0 else np.arange(L)
        er, ei = _cmul(pr[:, pw, None], pi[:, pw, None], bbr[:, None], bbi[:, None])
        bcs.append(jnp.concatenate([er, ei], -1).reshape(G, L * GS, 2 * P))
        pw = np.arange(1, L + 1) if j == 0 else np.arange(L, 0, -1)
        fr, fi = _cmul(cre[:, None], cim[:, None], pr[:, pw, None], pi[:, pw, None])
        ccts.append(jnp.concatenate([fr, -fi], -1).reshape(G, L * GS, 2 * P))
        sp = seg_pows[:, None, None]
        sr = jnp.exp(lr * dt * sp) * jnp.cos(li * dt * sp)
        si = jnp.exp(lr * dt * sp) * jnp.sin(li * dt * sp)
        a1 = jnp.concatenate([sr, sr], -1).reshape(2, 1, G * 2 * P)
        a2 = jnp.concatenate([-si, si], -1).reshape(2, 1, G * 2 * P)
        rows.append(jnp.stack([a1, a2], axis=1))
    c0 = jnp.concatenate([cre, -cim], -1).astype(BF16)
    return jnp.stack(bcs).astype(BF16), c0, jnp.stack(ccts).astype(BF16), jnp.stack(rows)


def _s5_chunk_cat(u_ref, dtype):
    n = u_ref.shape[0] // S5_L
    return jnp.concatenate([u_ref[pl.ds(s, n, stride=S5_L), :].astype(dtype)
                            for s in range(S5_L)], axis=-1)


def _s5_place_group_rows(dst_ref, src_ref, col0):
    for gl in range(S5_TILE_GROUPS):
        for s in range(S5_L):
            r = s * LANES + gl * S5_GROUP
            dst_ref[r:r + S5_GROUP, col0 + gl * LANES:col0 + (gl + 1) * LANES] = (
                src_ref[gl, s * S5_GROUP:(s + 1) * S5_GROUP, :])


def _s5_drive_kernel(u_ref, bcf_ref, bcb_ref, ef_ref, eb_ref, op_ref):
    first = (pl.program_id(0) == 0) & (pl.program_id(1) == 0)

    @pl.when(first)
    def _():
        op_ref[...] = jnp.zeros(op_ref.shape, BF16)

    @pl.when(pl.program_id(1) == 0)
    def _():
        _s5_place_group_rows(op_ref, bcf_ref, 0)
        _s5_place_group_rows(op_ref, bcb_ref, S5_TILE_STATE)

    e = _dot(_s5_chunk_cat(u_ref, BF16), op_ref[...])
    ef_ref[...] = e[:, :S5_TILE_STATE]
    eb_ref[...] = e[:, S5_TILE_STATE:]


def s5_drive(uz, bc):
    rows = TOKENS // S5_ROW_SPLIT
    out = jax.ShapeDtypeStruct((S5_CHUNKS, S5_GROUPS * 2 * S5_STATE), F32)
    ospec = pl.BlockSpec((rows // S5_L, S5_TILE_STATE), lambda q, i: (i, q))

    def bspec(d):
        return pl.BlockSpec((None, S5_TILE_GROUPS, S5_L * S5_GROUP, 2 * S5_STATE),
                            lambda q, i: (d, q, 0, 0))
    return pl.pallas_call(
        _s5_drive_kernel,
        out_shape=[out, out],
        grid=(S5_TILES, S5_ROW_SPLIT),
        in_specs=[pl.BlockSpec((rows, LANES), lambda q, i: (i, q)), bspec(0), bspec(1)],
        out_specs=[ospec, ospec],
        scratch_shapes=[pltpu.VMEM((S5_CAT, 2 * S5_TILE_STATE), BF16)],
        compiler_params=_cparams("arbitrary", "arbitrary"),
    )(uz, bc, bc)


def _s5_scan_kernel(*refs, reverse):
    ng = S5_TILE_GROUPS
    nj = S5_SEG_CHUNKS
    e_refs, a_ref, x_refs = refs[:ng], refs[ng], refs[ng + 1:2 * ng + 1]
    in_ref, es_ref = refs[-2:]
    lanes = [slice(g * LANES, (g + 1) * LANES) for g in range(ng)]
    swap = lambda z: pltpu.roll(z, S5_STATE, 1)
    for g in range(ng):
        es_ref[g] = swap(e_refs[g][...])
    a1 = [a_ref[0, 0][:, sl] for sl in lanes]
    a2 = [a_ref[0, 1][:, sl] for sl in lanes]
    seg0 = lax.broadcasted_iota(jnp.int32, (S5_SEG, LANES), 0) == 0

    def tile(j):
        jj = (nj - 1 - j) if reverse else j
        return jj, pl.ds(jj, S5_SEG, stride=nj)

    def step(j, carry, emit):
        jj, rows = tile(j)
        out = []
        for g in range(ng):
            z, zs = carry[2 * g], carry[2 * g + 1]
            if reverse:
                reset = seg0 & (jj == S5_CTX_CHUNKS - 1)
                z, zs = jnp.where(reset, 0.0, z), jnp.where(reset, 0.0, zs)
            if emit:
                x_refs[g][rows, :] = z
            out += [a1[g] * z + a2[g] * zs + e_refs[g][rows, :],
                    a1[g] * zs - a2[g] * z + es_ref[g, rows, :]]
        return tuple(out)

    zero = tuple(jnp.zeros((S5_SEG, LANES), F32) for _ in range(2 * ng))
    end = lax.fori_loop(0, nj, functools.partial(step, emit=False), zero, unroll=2)[0::2]
    order = range(S5_SEG - 1, -1, -1) if reverse else range(S5_SEG)
    for g, sl in enumerate(lanes):
        cur = end[g][0:1, :] if reverse else jnp.zeros((1, LANES), F32)
        for s in order:
            in_ref[s:s + 1, sl] = cur
            cur = a_ref[1, 0][:, sl] * cur + a_ref[1, 1][:, sl] * swap(cur) + end[g][s:s + 1, :]
    start = []
    for sl in lanes:
        start += [in_ref[:, sl], swap(in_ref[:, sl])]
    lax.fori_loop(0, nj, functools.partial(step, emit=True), tuple(start), unroll=2)


def s5_scan(e, rows, direction):
    ng = S5_TILE_GROUPS
    cols = S5_TILE_STATE
    blks = [pl.BlockSpec((S5_CHUNKS, LANES), lambda i, g=g: (0, ng * i + g)) for g in range(ng)]
    return pl.pallas_call(
        functools.partial(_s5_scan_kernel, reverse=direction == 1),
        out_shape=[jax.ShapeDtypeStruct((S5_CHUNKS, S5_TILES * LANES), F32)] * ng,
        grid=(S5_TILES,),
        in_specs=blks + [pl.BlockSpec((None, 2, 2, 1, cols),
                                      lambda i: (direction, 0, 0, 0, i))],
        out_specs=[pl.BlockSpec((S5_CHUNKS, LANES), lambda i: (0, i))] * ng,
        scratch_shapes=[pltpu.VMEM((S5_SEG, cols), F32),
                        pltpu.VMEM((ng, S5_CHUNKS, LANES), F32)],
        compiler_params=_cparams("arbitrary"),
    )(*([e] * ng), rows)


def _gelu_tanh(x):
    return 0.5 * x * (1.0 + jnp.tanh(math.sqrt(2.0 / math.pi) * (x + 0.044715 * (x * x * x))))


def _s5_read_kernel(u_ref, b0f_ref, b0b_ref, c0_ref, cctf_ref, cctb_ref, d_ref, *refs):
    x_refs = refs[:2 * S5_TILE_GROUPS]
    o_ref, m_ref, c_ref, t_ref, dk_ref = refs[2 * S5_TILE_GROUPS:]
    first = (pl.program_id(0) == 0) & (pl.program_id(1) == 0)
    ts = S5_TILE_STATE

    @pl.when(first)
    def _():
        c_ref[...] = jnp.zeros(c_ref.shape, BF16)
        t_ref[...] = jnp.zeros(t_ref.shape, BF16)

    @pl.when(pl.program_id(1) == 0)
    def _():
        _s5_place_group_rows(c_ref, cctf_ref, 0)
        _s5_place_group_rows(c_ref, cctb_ref, ts)
        for k, src in enumerate((b0f_ref, b0b_ref, c0_ref)):
            for gl in range(S5_TILE_GROUPS):
                t_ref[k, gl * S5_GROUP:(gl + 1) * S5_GROUP, gl * LANES:(gl + 1) * LANES] = src[gl]
        b0f, b0b, c0 = t_ref[0], t_ref[1], t_ref[2]
        dk_ref[S5_L - 1] = (_dot_nt(b0f, c0) + _dot_nt(b0b, c0)).astype(BF16)
        for lag in range(1, S5_L):
            cf = c_ref[(lag - 1) * LANES:lag * LANES, 0:ts]
            dk_ref[S5_L - 1 + lag] = _dot_nt(b0f, cf).astype(BF16)
            cb = c_ref[(S5_L - lag) * LANES:(S5_L - lag + 1) * LANES, ts:2 * ts]
            dk_ref[S5_L - 1 - lag] = _dot_nt(b0b, cb).astype(BF16)
        for s in range(S5_L):
            for t in range(S5_L):
                m_ref[s * LANES:(s + 1) * LANES, t * LANES:(t + 1) * LANES] = (
                    dk_ref[t - s + S5_L - 1])

    n = u_ref.shape[0] // S5_L
    xs = jnp.concatenate([x[...].astype(BF16) for x in x_refs], axis=-1)
    y = _dot(_s5_chunk_cat(u_ref, BF16), m_ref[...]) + _dot_nt(xs, c_ref[...])
    for t in range(S5_L):
        rows = pl.ds(t, n, stride=S5_L)
        o_ref[rows, :] = _gelu_tanh(y[:, t * LANES:(t + 1) * LANES] + d_ref[...] * u_ref[rows, :])


def s5_read(uz, bc, c0, cct, xf, xb, d_skip):
    rows = TOKENS // S5_ROW_SPLIT
    ublk = pl.BlockSpec((rows, LANES), lambda q, i: (i, q))
    xblk = pl.BlockSpec((rows // S5_L, LANES), lambda q, i: (i, q))
    gp = (S5_TILE_GROUPS, S5_GROUP, 2 * S5_STATE)

    def cspec(d):
        return pl.BlockSpec((None, S5_TILE_GROUPS, S5_L * S5_GROUP, 2 * S5_STATE),
                            lambda q, i: (d, q, 0, 0))

    def b0spec(d, s):
        return pl.BlockSpec((None, S5_TILE_GROUPS, None, S5_GROUP, 2 * S5_STATE),
                            lambda q, i: (d, q, s, 0, 0))
    bc5 = bc.reshape(2, S5_GROUPS, S5_L, S5_GROUP, 2 * S5_STATE)
    return pl.pallas_call(
        _s5_read_kernel,
        out_shape=jax.ShapeDtypeStruct((TOKENS, S5_WIDTH), F32),
        grid=(S5_TILES, S5_ROW_SPLIT),
        in_specs=[ublk, b0spec(0, S5_L - 1), b0spec(1, 0),
                  pl.BlockSpec(gp, lambda q, i: (q, 0, 0)),
                  cspec(0), cspec(1),
                  pl.BlockSpec((1, LANES), lambda q, i: (0, q))] + [xblk] * (len(xf) + len(xb)),
        out_specs=ublk,
        scratch_shapes=[pltpu.VMEM((S5_CAT, S5_CAT), BF16),
                        pltpu.VMEM((S5_CAT, 2 * S5_TILE_STATE), BF16),
                        pltpu.VMEM((3, LANES, S5_TILE_STATE), BF16),
                        pltpu.VMEM((2 * S5_L - 1, LANES, LANES), BF16)],
        compiler_params=_cparams("arbitrary", "arbitrary"),
    )(uz, bc5, bc5, c0, cct, cct, d_skip.astype(F32).reshape(1, S5_WIDTH), *xf, *xb)


def _glu_kernel(s_ref, w_ref, b_ref, o_ref, lhs_ref, *, tn):
    j = pl.program_id(1)

    @pl.when(j == 0)
    def _():
        lhs_ref[...] = s_ref[...].astype(BF16)

    v = _dot(lhs_ref[...], w_ref[...]) + b_ref[...]
    sj = s_ref[:, pl.ds(pl.multiple_of(j * tn, tn), tn)]
    o_ref[...] = sj * jax.nn.sigmoid(v)


def s5_glu(s, w_glu, b_glu, tm=768, tn=1024):
    d = S5_WIDTH
    return pl.pallas_call(
        functools.partial(_glu_kernel, tn=tn),
        out_shape=jax.ShapeDtypeStruct((TOKENS, d), F32),
        grid=(TOKENS // tm, d // tn),
        in_specs=[pl.BlockSpec((tm, d), lambda i, j: (i, 0)),
                  pl.BlockSpec((d, tn), lambda i, j: (0, j)),
                  pl.BlockSpec((1, tn), lambda i, j: (0, j))],
        out_specs=pl.BlockSpec((tm, tn), lambda i, j: (i, j)),
        scratch_shapes=[pltpu.VMEM((tm, d), BF16)],
        compiler_params=_cparams("arbitrary", "arbitrary"),
    )(s, w_glu, b_glu.reshape(1, d))


def _block_diag(w, size):
    lane = np.arange(size)
    spread = jnp.asarray(lane[None, :] % ML_BLOCK == np.arange(ML_BLOCK)[:, None], BF16)
    tiled = jnp.dot(w.astype(BF16).reshape(-1, ML_BLOCK), spread, preferred_element_type=BF16)
    same_block = jnp.asarray(lane[:, None] // ML_BLOCK == lane[None, :] // ML_BLOCK)
    return jnp.where(same_block, tiled.reshape(-1, size, size), jnp.zeros((), BF16))


def na_layer(xc, h, mods, tail, w_in, li, rpb, w_out):
    e = D_MODEL
    last = isinstance(tail[0], str)
    qkvz = in_proj(h, w_in, li, 0, 4 * e, BF16, scaled_cols=e, scale=NA_Q_SCALE)
    o = na_attention(qkvz, na_pair_table(rpb), with_ctx_out=not last)
    if last:
        return gated_out_final(o, qkvz, 3 * e, w_out.astype(BF16), xc, mods, tail[1])
    tm = CTX_LEN if isinstance(xc, tuple) else 384
    return gated_out(o, qkvz, 3 * e, w_out.astype(BF16), xc, mods, *tail, tm=tm)


def mlstm_layer(xc, h, mods, tail, w_in, li, conv_w, conv_b, wq, wk, wv, w_gate, b_gate, skip,
                norm_g, w_out):
    e = ML_WIDTH
    xz = in_proj(h, w_in, li, 0, 2 * e, F32)
    wq_bd = _block_diag(wq, ML_DIAG)
    wk_bd = _block_diag(wk, ML_DIAG)
    wv_bd = _block_diag(wv, ML_DIAG)
    wvt_bd = _block_diag(wv.transpose(0, 2, 1), ML_DIAG)
    wg_t = w_gate.reshape(3, e, 4 * ML_HEADS).transpose(0, 2, 1).astype(BF16)
    xconv, q, k, vt, gates_t = ml_pre(xz, conv_w, conv_b, wq_bd, wk_bd, wv_bd, wvt_bd, wg_t,
                                      b_gate)
    hf, hb = ml_recurrence(q, k, vt, gates_t)
    return ml_out(hf, hb, xconv, xz, norm_g, skip, w_out.astype(BF16), xc, mods, *tail)


def s5_layer(xc, h, mods, tail, w_in, li, lam_re, lam_im, log_dt, b_re, b_im, c_re, c_im,
             d_skip, w_glu, b_glu, w_out):
    e = S5_WIDTH
    uz = in_proj(h, w_in, li, 0, 2 * e, F32)
    bc, c0, cct, rows = s5_operators(lam_re, lam_im, log_dt, b_re, b_im, c_re, c_im)
    ef, eb = s5_drive(uz, bc)
    xf = s5_scan(ef, rows, 0)
    xb = s5_scan(eb, rows, 1)
    s = s5_read(uz, bc, c0, cct, xf, xb, d_skip)
    s2 = s5_glu(s, w_glu.astype(BF16), b_glu)
    return gated_out(s2, uz, e, w_out.astype(BF16), xc, mods, *tail)


def kernel(x, c, ctx, c_ctx, norm_g, ada_w, ada_b, na_w_in, na_rpb, na_w_out, ml_w_in, ml_conv_w, ml_conv_b, ml_wq, ml_wk, ml_wv, ml_w_gate, ml_b_gate, ml_skip, ml_norm_g, ml_w_out, s5_w_in, s5_lam_re, s5_lam_im, s5_log_dt, s5_b_re, s5_b_im, s5_c_re, s5_c_im, s5_d, s5_w_glu, s5_b_glu, s5_w_out, final_g):
    xc = (ctx[0], x[0])
    c8 = jnp.concatenate([c, c_ctx[None, :], jnp.zeros((6, D_MODEL), F32)], axis=0)
    mods = ada_mods(c8, ada_w, ada_b)
    h = norm_mod(xc, norm_g[0], mods[0], tm=CTX_LEN)
    ia = ib = ic = 0
    assert (DEPTH - 1) % 3 == 0, "the last layer must be the attention layer that ends the trunk"
    for layer in range(DEPTH):
        tail = ((norm_g[layer + 1], mods[layer + 1]) if layer + 1 < DEPTH
                else ('final', final_g))
        kind = layer % 3
        if kind == 0:
            out = na_layer(xc, h, mods[layer], tail, na_w_in, ia, na_rpb[ia], na_w_out[ia])
            ia += 1
        elif kind == 1:
            out = mlstm_layer(xc, h, mods[layer], tail, ml_w_in, ib, ml_conv_w[ib],
                              ml_conv_b[ib], ml_wq[ib], ml_wk[ib], ml_wv[ib], ml_w_gate[ib],
                              ml_b_gate[ib], ml_skip[ib], ml_norm_g[ib], ml_w_out[ib])
            ib += 1
        else:
            out = s5_layer(xc, h, mods[layer], tail, s5_w_in, ic, s5_lam_re[ic],
                           s5_lam_im[ic], s5_log_dt[ic], s5_b_re[ic], s5_b_im[ic], s5_c_re[ic],
                           s5_c_im[ic], s5_d[ic], s5_w_glu[ic], s5_b_glu[ic], s5_w_out[ic])
            ic += 1
        if layer + 1 < DEPTH:
            xc, h = out
    return out[None]
```

```python
import functools
import itertools
import math

import numpy as np
import jax
import jax.numpy as jnp
from jax import lax
from jax.experimental import pallas as pl
from jax.experimental.pallas import tpu as pltpu

F32 = jnp.float32
BF16 = jnp.bfloat16

D_MODEL = 2048
SEQ = 8192
CTX_LEN = 256
TOKENS = CTX_LEN + SEQ
DEPTH = 4
EPS = 1e-6
NEG = -1e30

GRID_W = 64
GRID_ROWS = SEQ // GRID_W
NA_HEADS = 16
NA_HEAD_DIM = 128
WIN_H = 8
WIN_W = 16
NA_QROWS = 4
NA_BLK = NA_QROWS * GRID_W
NA_WIN_BLKS = 3
NA_HEADS_PER_STEP = 8
LOG2E = math.log2(math.e)
NA_Q_SCALE = NA_HEAD_DIM ** -0.5 * LOG2E

ML_WIDTH = 2 * D_MODEL
ML_HEADS = 8
ML_HEAD_DIM = ML_WIDTH // ML_HEADS
ML_BLOCK = 4
ML_CHUNK = 256
ML_REC_HEADS = 4
ML_COLS = 512
ML_DIAG = 256
ML_STRIP = 256

S5_WIDTH = D_MODEL
S5_GROUP = 16
S5_GROUPS = S5_WIDTH // S5_GROUP
S5_STATE = 64
S5_L = 16
S5_SEG = 8
S5_CHUNKS = TOKENS // S5_L
S5_CTX_CHUNKS = CTX_LEN // S5_L
S5_SEG_CHUNKS = S5_CHUNKS // S5_SEG
LANES = 128
S5_TILE_GROUPS = LANES // S5_GROUP
S5_TILES = S5_WIDTH // LANES
S5_CAT = S5_L * LANES
S5_TILE_STATE = S5_TILE_GROUPS * 2 * S5_STATE
S5_ROW_SPLIT = 1

VMEM_LIMIT = 56 * 1024 * 1024


def _cparams(*sem):
    return pltpu.CompilerParams(dimension_semantics=sem, vmem_limit_bytes=VMEM_LIMIT)


def _dot(a, b):
    return jnp.dot(a, b, preferred_element_type=F32)


def _dot_nt(a, b):
    return lax.dot_general(a, b, (((1,), (1,)), ((), ())), preferred_element_type=F32)


def _silu(x):
    return x * jax.nn.sigmoid(x)


def _row_select(row0, n, ctx_row, x_row):
    rows = row0 + lax.broadcasted_iota(jnp.int32, (n, 1), 0)
    return jnp.where(rows < CTX_LEN, ctx_row, x_row)


def _ada_kernel(c_ref, w_ref, b_ref, o_ref):
    cs = _silu(c_ref[...]).astype(BF16)
    o_ref[0] = _dot(cs, w_ref[0].astype(BF16)) + b_ref[0]


def ada_mods(c8, ada_w, ada_b):
    tn = 512
    n = 3 * D_MODEL
    return pl.pallas_call(
        _ada_kernel,
        out_shape=jax.ShapeDtypeStruct((DEPTH, 8, n), F32),
        grid=(DEPTH, n // tn),
        in_specs=[pl.BlockSpec((8, D_MODEL), lambda l, j: (0, 0)),
                  pl.BlockSpec((1, D_MODEL, tn), lambda l, j: (l, 0, j)),
                  pl.BlockSpec((1, 1, tn), lambda l, j: (l, 0, j))],
        out_specs=pl.BlockSpec((1, 8, tn), lambda l, j: (l, 0, j)),
        compiler_params=_cparams("arbitrary", "arbitrary"),
    )(c8, ada_w, ada_b.reshape(DEPTH, 1, n))


def _rmsnorm(x, g):
    return x * lax.rsqrt(jnp.mean(x * x, axis=-1, keepdims=True) + EPS) * g


def _norm_mod_rows(x, g_ref, sh_ref, sc_ref, row0):
    sc = _row_select(row0, x.shape[0], sc_ref[1:2, :], sc_ref[0:1, :])
    sh = _row_select(row0, x.shape[0], sh_ref[1:2, :], sh_ref[0:1, :])
    return (_rmsnorm(x, g_ref[...]) * (1.0 + sc) + sh).astype(BF16)


def _norm_specs(g, mods):
    d = D_MODEL
    return ([g.reshape(1, d), mods, mods],
            [pl.BlockSpec((1, d), lambda *a: (0, 0)), pl.BlockSpec((8, d), lambda *a: (0, 0)),
             pl.BlockSpec((8, d), lambda *a: (0, 1))])


def _stream_specs(xc, tm):
    d = D_MODEL
    if not isinstance(xc, tuple):
        return [xc], [pl.BlockSpec((tm, d), lambda i: (i, 0))]
    assert tm == CTX_LEN
    return list(xc), [pl.BlockSpec((tm, d), lambda i: (0, 0)),
                      pl.BlockSpec((tm, d), lambda i: (jnp.maximum(i - 1, 0), 0))]


def _stream_tile(refs):
    if len(refs) == 1:
        return refs[0][...]
    return jnp.where(pl.program_id(0) == 0, refs[0][...], refs[1][...])


def _norm_mod_kernel(*refs, tm):
    g_ref, sh_ref, sc_ref, o_ref = refs[-4:]
    o_ref[...] = _norm_mod_rows(_stream_tile(refs[:-4]), g_ref, sh_ref, sc_ref,
                                pl.program_id(0) * tm)


def norm_mod(xc, g, mods, tm=384):
    d = D_MODEL
    nargs, nspecs = _norm_specs(g, mods)
    sargs, sspecs = _stream_specs(xc, tm)
    return pl.pallas_call(
        functools.partial(_norm_mod_kernel, tm=tm),
        out_shape=jax.ShapeDtypeStruct((TOKENS, d), BF16),
        grid=(TOKENS // tm,),
        in_specs=sspecs + nspecs,
        out_specs=pl.BlockSpec((tm, d), lambda i: (i, 0)),
        compiler_params=_cparams("arbitrary"),
    )(*sargs, *nargs)


def _in_proj_kernel(h_ref, w_ref, o_ref, wb_ref, *, scaled_blocks, scale):
    j = pl.program_id(0)

    @pl.when(pl.program_id(1) == 0)
    def _():
        wb_ref[...] = w_ref[...].astype(BF16)

    acc = _dot(h_ref[...], wb_ref[...])
    if scaled_blocks:
        acc = acc * jnp.where(j < scaled_blocks, scale, 1.0)
    o_ref[...] = acc.astype(o_ref.dtype)


def in_proj(h, w, layer, col0, ncols, out_dtype, scaled_cols=0, scale=1.0, tm=1408, tn=1024):
    d = D_MODEL
    cb = col0 // tn
    return pl.pallas_call(
        functools.partial(_in_proj_kernel, scaled_blocks=scaled_cols // tn, scale=scale),
        out_shape=jax.ShapeDtypeStruct((TOKENS, ncols), out_dtype),
        grid=(ncols // tn, TOKENS // tm),
        in_specs=[pl.BlockSpec((tm, d), lambda j, i: (i, 0)),
                  pl.BlockSpec((None, d, tn), lambda j, i: (layer, 0, cb + j))],
        out_specs=pl.BlockSpec((tm, tn), lambda j, i: (i, j)),
        scratch_shapes=[pltpu.VMEM((d, tn), BF16)],
        compiler_params=_cparams("arbitrary", "arbitrary"),
    )(h, w)


def _gated_out_kernel(a_ref, z_ref, w_ref, gate_ref, g_ref, sh_ref, sc_ref, *refs, tm):
    o_ref, h_ref = refs[-2:]
    i = pl.program_id(0)
    lhs = (a_ref[...].astype(F32) * _silu(z_ref[...].astype(F32))).astype(BF16)
    gate = _row_select(i * tm, tm, gate_ref[1:2, :], gate_ref[0:1, :])
    x = _stream_tile(refs[:-2]) + gate * _dot(lhs, w_ref[...])
    o_ref[...] = x
    h_ref[...] = _norm_mod_rows(x, g_ref, sh_ref, sc_ref, i * tm)


def gated_out(a, zsrc, zcol0, w, xc, mods, next_g, next_mods, tm=384):
    k = a.shape[1]
    d = D_MODEL
    zb = zcol0 // k
    nargs, nspecs = _norm_specs(next_g, next_mods)
    sargs, sspecs = _stream_specs(xc, tm)
    row = pl.BlockSpec((tm, d), lambda i: (i, 0))
    return pl.pallas_call(
        functools.partial(_gated_out_kernel, tm=tm),
        out_shape=[jax.ShapeDtypeStruct((TOKENS, d), F32),
                   jax.ShapeDtypeStruct((TOKENS, d), BF16)],
        grid=(TOKENS // tm,),
        in_specs=[pl.BlockSpec((tm, k), lambda i: (i, 0)),
                  pl.BlockSpec((tm, k), lambda i: (i, zb)),
                  pl.BlockSpec((k, d), lambda i: (0, 0)),
                  pl.BlockSpec((8, d), lambda i: (0, 2))] + nspecs + sspecs,
        out_specs=[row, row],
        compiler_params=_cparams("arbitrary"),
    )(a, zsrc, w, mods, *nargs, *sargs)


def _gated_out_final_kernel(a_ref, z_ref, w_ref, x_ref, gate_ref, g_ref, o_ref):
    lhs = (a_ref[...].astype(F32) * _silu(z_ref[...].astype(F32))).astype(BF16)
    x = x_ref[...] + gate_ref[0:1, :] * _dot(lhs, w_ref[...])
    o_ref[...] = _rmsnorm(x, g_ref[...])


def gated_out_final(a, zsrc, zcol0, w, xc, mods, final_g, tm=256):
    k = a.shape[1]
    d = D_MODEL
    zb = zcol0 // k
    off = CTX_LEN // tm
    return pl.pallas_call(
        _gated_out_final_kernel,
        out_shape=jax.ShapeDtypeStruct((SEQ, d), F32),
        grid=(SEQ // tm,),
        in_specs=[pl.BlockSpec((tm, k), lambda i: (i, 0)),
                  pl.BlockSpec((tm, k), lambda i: (i + off, zb)),
                  pl.BlockSpec((k, d), lambda i: (0, 0)),
                  pl.BlockSpec((tm, d), lambda i: (i + off, 0)),
                  pl.BlockSpec((8, d), lambda i: (0, 2)),
                  pl.BlockSpec((1, d), lambda i: (0, 0))],
        out_specs=pl.BlockSpec((tm, d), lambda i: (i, 0)),
        compiler_params=_cparams("arbitrary"),
    )(a, zsrc, w, xc, mods, final_g.reshape(1, d))


NA_REL_ROWS = 2 * WIN_H - 1
NA_A_LO = WIN_H - 1 - WIN_H // 2
NA_A_HI = NA_A_LO + WIN_H - 1
NA_PAIR_BOTH = NA_REL_ROWS - 1
NA_PAIR_LEFT_MASKED = NA_PAIR_BOTH
NA_PAIR_RIGHT_MASKED = NA_PAIR_BOTH + 1
NA_PAIR_MASKED = NA_PAIR_BOTH + 2
NA_PAIR_ENTRIES = NA_PAIR_BOTH + 3
NA_BAND_ROWS = NA_WIN_BLKS * NA_QROWS


def _na_pair_plan():
    nblk = GRID_ROWS // NA_QROWS
    plan = np.full((4, NA_QROWS, NA_BAND_ROWS // 2), NA_PAIR_MASKED, np.int32)
    for kind, g in ((0, 0), (1, 1), (2, nblk - 1)):
        band0 = NA_QROWS * int(np.clip(g - 1, 0, nblk - NA_WIN_BLKS))
        for i in range(NA_QROWS):
            rq = NA_QROWS * g + i
            r0 = int(np.clip(rq - WIN_H // 2, 0, GRID_ROWS - WIN_H))
            rel = [rk - rq + WIN_H - 1 if r0 <= rk < r0 + WIN_H else None
                   for rk in range(band0, band0 + NA_BAND_ROWS)]
            for p in range(NA_BAND_ROWS // 2):
                lo, hi = rel[2 * p], rel[2 * p + 1]
                if lo is not None and hi is not None:
                    plan[kind, i, p] = lo
                elif hi is not None:
                    assert hi == NA_A_LO
                    plan[kind, i, p] = NA_PAIR_LEFT_MASKED
                elif lo is not None:
                    assert lo == NA_A_HI
                    plan[kind, i, p] = NA_PAIR_RIGHT_MASKED
    return plan


def na_pair_table(rpb):
    cq = np.arange(GRID_W)[:, None]
    ck = np.arange(GRID_W)[None, :]
    c0 = np.clip(cq - WIN_W // 2, 0, GRID_W - WIN_W)
    ok = (ck >= c0) & (ck < c0 + WIN_W)
    rp = jnp.pad(rpb.astype(F32), ((0, 0), (0, 0), (GRID_W, GRID_W + 1)))
    m = rp.shape[-1]
    skew = jnp.tile(rp, (1, 1, GRID_W))[:, :, :GRID_W * (m - 1)]
    skew = skew.reshape(NA_HEADS, NA_REL_ROWS, GRID_W, m - 1)
    t = skew[:, :, :, GRID_W + WIN_W - 1:2 * GRID_W + WIN_W - 1]
    t = jnp.where(jnp.asarray(ok), t * LOG2E, NEG)
    masked = jnp.full((NA_HEADS, 1, GRID_W, GRID_W), NEG, F32)
    left = jnp.concatenate([t[:, :-1], masked, t[:, NA_A_HI:NA_A_HI + 1], masked], axis=1)
    right = jnp.concatenate([t[:, 1:], t[:, NA_A_LO:NA_A_LO + 1], masked, masked], axis=1)
    return jnp.concatenate([left, right], axis=-1)


def _na_kernel(q_ref, kc_ref, k0_ref, k1_ref, k2_ref, vc_ref, v0_ref, v1_ref, v2_ref,
               pair_ref, o_ref, bias_ref, *, heads, q0, nq):
    qb = pl.program_id(1) + q0
    plan = _na_pair_plan()
    pw = 2 * GRID_W

    for kind, first in ((3, 0), (0, 1), (1, 2), (2, nq - 1)):
        @pl.when(qb == first)
        def _(kind=kind):
            for h in range(heads):
                for i in range(NA_QROWS):
                    for p in range(NA_BAND_ROWS // 2):
                        bias_ref[h, i * GRID_W:(i + 1) * GRID_W, p * pw:(p + 1) * pw] = (
                            pair_ref[h, int(plan[kind, i, p])])

    for h in range(heads):
        sl = slice(h * NA_HEAD_DIM, (h + 1) * NA_HEAD_DIM)
        q = q_ref[:, sl]
        s = [_dot_nt(q, kc_ref[:, sl])]
        for b, k_ref in enumerate((k0_ref, k1_ref, k2_ref)):
            s.append(_dot_nt(q, k_ref[:, sl]) + bias_ref[h, :, b * NA_BLK:(b + 1) * NA_BLK])
        m = s[0].max(axis=-1, keepdims=True)
        for t in s[1:]:
            m = jnp.maximum(m, t.max(axis=-1, keepdims=True))
        p = [jnp.exp2(t - m) for t in s]
        den = p[0].sum(axis=-1, keepdims=True)
        for t in p[1:]:
            den = den + t.sum(axis=-1, keepdims=True)
        p_all = jnp.concatenate([t.astype(BF16) for t in p], axis=1)
        v_all = jnp.concatenate([v_ref[:, sl] for v_ref in (vc_ref, v0_ref, v1_ref, v2_ref)],
                                axis=0)
        o_ref[:, sl] = (_dot(p_all, v_all) / den).astype(o_ref.dtype)


def na_attention(qkv, pair_tab, with_ctx_out):
    hs = NA_HEADS_PER_STEP
    hw = hs * NA_HEAD_DIM
    nh = NA_HEADS // hs
    nq = TOKENS // NA_BLK
    q0 = 0 if with_ctx_out else 1
    nlat = GRID_ROWS // NA_QROWS

    def band(qb):
        return 1 + jnp.clip(qb - 2, 0, nlat - NA_WIN_BLKS)

    def spec(col_base, fn):
        return pl.BlockSpec((NA_BLK, hw), lambda h, g: (fn(g + q0), col_base + h))

    kspecs = [spec(nh, lambda qb: 0 * qb)] + [
        spec(nh, lambda qb, b=b: band(qb) + b) for b in range(NA_WIN_BLKS)]
    vspecs = [spec(2 * nh, lambda qb: 0 * qb)] + [
        spec(2 * nh, lambda qb, b=b: band(qb) + b) for b in range(NA_WIN_BLKS)]
    return pl.pallas_call(
        functools.partial(_na_kernel, heads=hs, q0=q0, nq=nq),
        out_shape=jax.ShapeDtypeStruct(((nq - q0) * NA_BLK, D_MODEL), BF16),
        grid=(nh, nq - q0),
        in_specs=[spec(0, lambda qb: qb)] + kspecs + vspecs + [
            pl.BlockSpec((hs, NA_PAIR_ENTRIES, GRID_W, 2 * GRID_W),
                         lambda h, g: (h, 0, 0, 0))],
        out_specs=pl.BlockSpec((NA_BLK, hw), lambda h, g: (g, h)),
        scratch_shapes=[pltpu.VMEM((hs, NA_BLK, NA_WIN_BLKS * NA_BLK), F32)],
        compiler_params=_cparams("arbitrary", "arbitrary"),
    )(*([qkv] * 9), pair_tab)


def _ml_pre_kernel(x_ref, prev_ref, next_ref, cw_ref, cb_ref, wq_ref, wk_ref, wv_ref, wvt_ref,
                   wg_ref, bg_ref, xc_ref, q_ref, k_ref, vt_ref, g_ref, *, tm):
    i = pl.program_id(0)
    j = pl.program_id(1)
    x = x_ref[...]
    rows = i * tm + lax.broadcasted_iota(jnp.int32, (tm, 1), 0)
    local = lax.broadcasted_iota(jnp.int32, (tm, 1), 0)
    x_prev = jnp.where(local == 0, prev_ref[7:8, :], pltpu.roll(x, 1, 0))
    x_prev = jnp.where((rows == 0) | (rows == CTX_LEN), 0.0, x_prev)
    x_next = jnp.where(local == tm - 1, next_ref[0:1, :], pltpu.roll(x, tm - 1, 0))
    x_next = jnp.where((rows == CTX_LEN - 1) | (rows == TOKENS - 1), 0.0, x_next)
    @pl.when(j == 0)
    def _():
        g_ref[...] = jnp.broadcast_to(bg_ref[...], g_ref.shape)

    def activate(r):
        rs = slice(r * ML_STRIP, (r + 1) * ML_STRIP)
        conv = (cb_ref[...] + x_prev[rs] * cw_ref[0:1, :] + x[rs] * cw_ref[1:2, :]
                + x_next[rs] * cw_ref[2:3, :])
        xcb = _silu(conv).astype(BF16)
        xc_ref[rs, :] = xcb
        return xcb, x[rs].astype(BF16)

    def project(r, xcb, xb):
        rs = slice(r * ML_STRIP, (r + 1) * ML_STRIP)
        part = None
        for t in range(ML_COLS // ML_DIAG):
            cs = slice(t * ML_DIAG, (t + 1) * ML_DIAG)
            q = _dot(xcb[:, cs], wq_ref[t])
            k = _dot(xcb[:, cs], wk_ref[t])
            v = _dot(xb[:, cs], wv_ref[t])
            qb, kb, vb = q.astype(BF16), k.astype(BF16), v.astype(BF16)
            q_ref[rs, cs] = qb
            k_ref[rs, cs] = (k * (ML_HEAD_DIM ** -0.5)).astype(BF16)
            vt_ref[cs, rs] = _dot_nt(wvt_ref[t], xb[:, cs]).astype(BF16)
            gt = (_dot_nt(wg_ref[0, :, cs], qb) + _dot_nt(wg_ref[1, :, cs], kb)
                  + _dot_nt(wg_ref[2, :, cs], vb))
            part = gt if part is None else part + gt
        g_ref[:, rs] += part

    pending = activate(0)
    for r in range(tm // ML_STRIP):
        nxt = activate(r + 1) if (r + 1) * ML_STRIP < tm else None
        project(r, *pending)
        pending = nxt


def ml_pre(xz, conv_w, conv_b, wq_bd, wk_bd, wv_bd, wvt_bd, wg_t, b_gate, tm=768):
    e = ML_WIDTH
    c = ML_COLS
    nc = e // c
    last8 = TOKENS // 8 - 1
    tok = lambda i, j: (i, j)
    outs = pl.pallas_call(
        functools.partial(_ml_pre_kernel, tm=tm),
        out_shape=[jax.ShapeDtypeStruct((TOKENS, e), BF16),
                   jax.ShapeDtypeStruct((TOKENS, e), BF16),
                   jax.ShapeDtypeStruct((TOKENS, e), BF16),
                   jax.ShapeDtypeStruct((e, TOKENS), BF16),
                   jax.ShapeDtypeStruct((4 * ML_HEADS, TOKENS), F32)],
        grid=(TOKENS // tm, nc),
        in_specs=[pl.BlockSpec((tm, c), tok),
                  pl.BlockSpec((8, c), lambda i, j: (jnp.maximum(i * (tm // 8) - 1, 0), j)),
                  pl.BlockSpec((8, c), lambda i, j: (jnp.minimum((i + 1) * (tm // 8), last8), j)),
                  pl.BlockSpec((3, c), lambda i, j: (0, j)),
                  pl.BlockSpec((1, c), lambda i, j: (0, j)),
                  pl.BlockSpec((c // ML_DIAG, ML_DIAG, ML_DIAG), lambda i, j: (j, 0, 0)),
                  pl.BlockSpec((c // ML_DIAG, ML_DIAG, ML_DIAG), lambda i, j: (j, 0, 0)),
                  pl.BlockSpec((c // ML_DIAG, ML_DIAG, ML_DIAG), lambda i, j: (j, 0, 0)),
                  pl.BlockSpec((c // ML_DIAG, ML_DIAG, ML_DIAG), lambda i, j: (j, 0, 0)),
                  pl.BlockSpec((3, 4 * ML_HEADS, c), lambda i, j: (0, 0, j)),
                  pl.BlockSpec((4 * ML_HEADS, 1), lambda i, j: (0, 0))],
        out_specs=[pl.BlockSpec((tm, c), tok), pl.BlockSpec((tm, c), tok),
                   pl.BlockSpec((tm, c), tok),
                   pl.BlockSpec((c, tm), lambda i, j: (j, i)),
                   pl.BlockSpec((4 * ML_HEADS, tm), lambda i, j: (0, i))],
        compiler_params=_cparams("arbitrary", "arbitrary"),
    )(xz, xz, xz, conv_w, conv_b.reshape(1, e), wq_bd, wk_bd, wv_bd, wvt_bd, wg_t,
      b_gate.reshape(4 * ML_HEADS, 1))
    return outs


def _split3(x):
    hi = x.astype(BF16).astype(F32)
    r1 = x - hi
    mid = r1.astype(BF16).astype(F32)
    lo = (r1 - mid).astype(BF16).astype(F32)
    return hi, mid, lo


def _ml_step(q_ref, k_ref, vt_ref, i_row, f_row, upto_ref, upto_t_ref, mask_ref,
             c_ref, n_ref, m_ref, h_ref, reverse):
    L = ML_CHUNK
    logf = jax.nn.log_sigmoid(f_row)
    hi, mid, lo = _split3(logf)
    rid = lax.broadcasted_iota(jnp.int32, (16, L), 0)
    r16 = jnp.where(rid == 0, hi, jnp.where(rid == 1, mid, jnp.where(rid == 2, lo, 0.0)))
    r16 = r16.astype(BF16)
    rsum = _dot(r16, upto_ref[...])
    b_row = rsum[0:1] + rsum[1:2] + rsum[2:3]
    csum = _dot_nt(upto_t_ref[...], r16)
    b_col = csum[:, 0:1] + csum[:, 1:2] + csum[:, 2:3]
    m_prev = m_ref[...]
    a_col = b_col + m_prev
    yield
    dmat = (b_col - b_row + i_row) + mask_ref[...]
    m_t = jnp.maximum(a_col, dmat.max(axis=-1, keepdims=True))
    w_inter = jnp.exp(a_col - m_t)
    yield
    q, k, vt = q_ref[...], k_ref[...], vt_ref[...]
    s = _dot_nt(q, k) * jnp.exp(dmat - m_t)
    yield
    num = w_inter * _dot_nt(q, c_ref[...].astype(BF16)) + _dot_nt(s.astype(BF16), vt)
    qn = _dot_nt(q, n_ref[...].astype(BF16))[:, 0:1]
    den = w_inter * qn + s.sum(axis=-1, keepdims=True)
    h_ref[...] = (num / jnp.maximum(jnp.abs(den), jnp.exp(-m_t))).astype(h_ref.dtype)
    yield
    b_end = b_row[:, 0:1] if reverse else b_row[:, L - 1:L]
    g_row = b_end - b_row + i_row
    m_new = jnp.maximum(b_end + m_prev, g_row.max(axis=-1, keepdims=True))
    decay = jnp.exp(b_end + m_prev - m_new)
    w_row = jnp.exp(g_row - m_new)
    c_ref[...] = decay * c_ref[...] + _dot((vt.astype(F32) * w_row).astype(BF16), k)
    w8 = jnp.broadcast_to(w_row, (8, L)).astype(BF16)
    n_ref[...] = decay * n_ref[...] + _dot(w8, k)
    m_ref[...] = m_new


def _ml_rec_kernel(qf_ref, kf_ref, vtf_ref, gf_ref, qb_ref, kb_ref, vtb_ref, gb_ref,
                   lo_ref, up_ref, mlo_ref, mup_ref,
                   hf_ref, hb_ref, c_ref, n_ref, m_ref):
    @pl.when(pl.program_id(1) == 0)
    def _():
        for r in (c_ref, n_ref, m_ref):
            r[...] = jnp.zeros(r.shape, F32)

    nh = ML_HEADS
    dh = ML_HEAD_DIM
    steps = []
    for hh in range(ML_REC_HEADS):
        h = pl.program_id(0) * ML_REC_HEADS + hh
        cols = pl.ds(hh * dh, dh)
        steps += [
            _ml_step(qf_ref.at[:, cols], kf_ref.at[:, cols], vtf_ref.at[cols, :],
                     gf_ref[pl.ds(h, 1), :], gf_ref[pl.ds(nh + h, 1), :],
                     up_ref, lo_ref, mlo_ref, c_ref.at[2 * hh], n_ref.at[2 * hh],
                     m_ref.at[2 * hh], hf_ref.at[:, cols], reverse=False),
            _ml_step(qb_ref.at[:, cols], kb_ref.at[:, cols], vtb_ref.at[cols, :],
                     gb_ref[pl.ds(2 * nh + h, 1), :], gb_ref[pl.ds(3 * nh + h, 1), :],
                     lo_ref, up_ref, mup_ref, c_ref.at[2 * hh + 1], n_ref.at[2 * hh + 1],
                     m_ref.at[2 * hh + 1], hb_ref.at[:, cols], reverse=True)]
    for _ in itertools.zip_longest(*steps):
        pass


def ml_recurrence(q, k, vt, gates_t):
    L = ML_CHUNK
    dh = ML_HEAD_DIM
    nchunk = TOKENS // L
    fwd = lambda j: j
    bwd = lambda j: jnp.where(j == 0, 0, nchunk - j)
    specs = []
    hw = ML_REC_HEADS * dh
    for cm in (fwd, bwd):
        specs += [pl.BlockSpec((L, hw), lambda h, j, cm=cm: (cm(j), h)),
                  pl.BlockSpec((L, hw), lambda h, j, cm=cm: (cm(j), h)),
                  pl.BlockSpec((hw, L), lambda h, j, cm=cm: (h, cm(j))),
                  pl.BlockSpec((4 * ML_HEADS, L), lambda h, j, cm=cm: (0, cm(j)))]
    nstate = 2 * ML_REC_HEADS
    state = [pltpu.VMEM((nstate, dh, dh), F32), pltpu.VMEM((nstate, 8, dh), F32),
             pltpu.VMEM((nstate, 1, 1), F32)]
    row = np.arange(L)[:, None]
    col = np.arange(L)[None, :]
    consts = [jnp.asarray(row >= col, BF16), jnp.asarray(row <= col, BF16),
              jnp.asarray(np.where(row >= col, 0.0, NEG), F32),
              jnp.asarray(np.where(row <= col, 0.0, NEG), F32)]
    specs += [pl.BlockSpec((L, L), lambda h, j: (0, 0))] * len(consts)
    return pl.pallas_call(
        _ml_rec_kernel,
        out_shape=[jax.ShapeDtypeStruct((TOKENS, ML_WIDTH), BF16)] * 2,
        grid=(ML_HEADS // ML_REC_HEADS, nchunk),
        in_specs=specs,
        out_specs=[pl.BlockSpec((L, hw), lambda h, j: (fwd(j), h)),
                   pl.BlockSpec((L, hw), lambda h, j: (bwd(j), h))],
        scratch_shapes=state,
        compiler_params=_cparams("arbitrary", "arbitrary"),
    )(q, k, vt, gates_t, q, k, vt, gates_t, *consts)


def _ml_out_kernel(hf_ref, hb_ref, xc_ref, z_ref, ng_ref, skip_ref, w_ref, x_ref, gate_ref,
                   g2_ref, sh2_ref, sc2_ref, o_ref, h2_ref, *, tm):
    i = pl.program_id(0)
    kk = pl.program_id(1)
    @pl.when(kk == 0)
    def _():
        o_ref[...] = jnp.zeros(o_ref.shape, F32)

    def readout(r):
        rs = slice(r * ML_STRIP, (r + 1) * ML_STRIP)
        h = hf_ref[rs, :].astype(F32) + hb_ref[rs, :].astype(F32)
        mu = jnp.mean(h, axis=-1, keepdims=True)
        hc = h - mu
        var = jnp.mean(hc * hc, axis=-1, keepdims=True)
        hn = hc * lax.rsqrt(var + EPS) * ng_ref[...]
        return ((hn + skip_ref[...] * xc_ref[rs, :].astype(F32)) * _silu(z_ref[rs, :])).astype(BF16)

    pending = readout(0)
    for r in range(tm // ML_STRIP):
        nxt = readout(r + 1) if (r + 1) * ML_STRIP < tm else None
        o_ref[r * ML_STRIP:(r + 1) * ML_STRIP, :] += _dot(pending, w_ref[...])
        pending = nxt

    @pl.when(kk == pl.num_programs(1) - 1)
    def _():
        strip = 128

        def finish(r, carry):
            rows = pl.ds(pl.multiple_of(r * strip, strip), strip)
            row0 = i * tm + r * strip
            gate = _row_select(row0, strip, gate_ref[1:2, :], gate_ref[0:1, :])
            x = x_ref[rows, :] + gate * o_ref[rows, :]
            o_ref[rows, :] = x
            h2_ref[rows, :] = _norm_mod_rows(x, g2_ref, sh2_ref, sc2_ref, row0)
            return carry

        lax.fori_loop(0, tm // strip, finish, 0)


def ml_out(hf, hb, xconv, xz, norm_g, skip, w_out, xc, mods, next_g, next_mods, tm=768):
    e = ML_WIDTH
    dh = ML_HEAD_DIM
    d = D_MODEL
    nh = ML_HEADS
    hd = lambda i, k: (i, k)
    nargs, nspecs = _norm_specs(next_g, next_mods)
    row = pl.BlockSpec((tm, d), lambda i, k: (i, 0))
    return pl.pallas_call(
        functools.partial(_ml_out_kernel, tm=tm),
        out_shape=[jax.ShapeDtypeStruct((TOKENS, d), F32),
                   jax.ShapeDtypeStruct((TOKENS, d), BF16)],
        grid=(TOKENS // tm, nh),
        in_specs=[pl.BlockSpec((tm, dh), hd), pl.BlockSpec((tm, dh), hd),
                  pl.BlockSpec((tm, dh), hd),
                  pl.BlockSpec((tm, dh), lambda i, k: (i, nh + k)),
                  pl.BlockSpec((1, dh), lambda i, k: (0, k)),
                  pl.BlockSpec((1, dh), lambda i, k: (0, k)),
                  pl.BlockSpec((dh, d), lambda i, k: (k, 0)),
                  row,
                  pl.BlockSpec((8, d), lambda i, k: (0, 2))] + nspecs,
        out_specs=[row, row],
        compiler_params=_cparams("arbitrary", "arbitrary"),
    )(hf, hb, xconv, xz, norm_g.reshape(1, e), skip.reshape(1, e), w_out, xc, mods, *nargs)


def _cmul(ar, ai, br, bi):
    return ar * br - ai * bi, ar * bi + ai * br


def s5_operators(lam_re, lam_im, log_dt, b_re, b_im, c_re, c_im):
    G, P, GS, L = S5_GROUPS, S5_STATE, S5_GROUP, S5_L
    bre, bim = b_re.astype(F32).transpose(0, 2, 1), b_im.astype(F32).transpose(0, 2, 1)
    cre, cim = c_re.astype(F32), c_im.astype(F32)
    tau = jnp.arange(L + 1, dtype=F32)[None, :, None]
    seg_pows = jnp.asarray([1, S5_SEG_CHUNKS], F32) * L
    bcs, ccts, rows = [], [], []
    for j in range(2):
        lr, li = lam_re[j].astype(F32), lam_im[j].astype(F32)
        dt = jnp.exp(log_dt[j].astype(F32))[:, None]
        lrd, lid = (lr * dt)[:, None, :], (li * dt)[:, None, :]
        pr = jnp.exp(lrd * tau) * jnp.cos(lid * tau)
        pi = jnp.exp(lrd * tau) * jnp.sin(lid * tau)
        ar, ai = pr[:, 1], pi[:, 1]
        den = lr * lr + li * li
        kr = (((ar - 1.0) * lr + ai * li) / den)[:, None, :]
        ki = ((ai * lr - (ar - 1.0) * li) / den)[:, None, :]
        bbr = kr * bre - ki * bim
        bbi = kr * bim + ki * bre
        pw = np.arange(L - 1, -1, -1) if j == 0 else np.arange(L)
        er, ei = _cmul(pr[:, pw, None], pi[:, pw, None], bbr[:, None], bbi[:, None])
        bcs.append(jnp.concatenate([er, ei], -1).reshape(G, L * GS, 2 * P))
        pw = np.arange(1, L + 1) if j == 0 else np.arange(L, 0, -1)
        fr, fi = _cmul(cre[:, None], cim[:, None], pr[:, pw, None], pi[:, pw, None])
        ccts.append(jnp.concatenate([fr, -fi], -1).reshape(G, L * GS, 2 * P))
        sp = seg_pows[:, None, None]
        sr = jnp.exp(lr * dt * sp) * jnp.cos(li * dt * sp)
        si = jnp.exp(lr * dt * sp) * jnp.sin(li * dt * sp)
        a1 = jnp.concatenate([sr, sr], -1).reshape(2, 1, G * 2 * P)
        a2 = jnp.concatenate([-si, si], -1).reshape(2, 1, G * 2 * P)
        rows.append(jnp.stack([a1, a2], axis=1))
    c0 = jnp.concatenate([cre, -cim], -1).astype(BF16)
    return jnp.stack(bcs).astype(BF16), c0, jnp.stack(ccts).astype(BF16), jnp.stack(rows)


def _s5_chunk_cat(u_ref, dtype):
    n = u_ref.shape[0] // S5_L
    return jnp.concatenate([u_ref[pl.ds(s, n, stride=S5_L), :].astype(dtype)
                            for s in range(S5_L)], axis=-1)


def _s5_place_group_rows(dst_ref, src_ref, col0):
    for gl in range(S5_TILE_GROUPS):
        for s in range(S5_L):
            r = s * LANES + gl * S5_GROUP
            dst_ref[r:r + S5_GROUP, col0 + gl * LANES:col0 + (gl + 1) * LANES] = (
                src_ref[gl, s * S5_GROUP:(s + 1) * S5_GROUP, :])


def _s5_drive_kernel(u_ref, bcf_ref, bcb_ref, ef_ref, eb_ref, op_ref):
    first = (pl.program_id(0) == 0) & (pl.program_id(1) == 0)

    @pl.when(first)
    def _():
        op_ref[...] = jnp.zeros(op_ref.shape, BF16)

    @pl.when(pl.program_id(1) == 0)
    def _():
        _s5_place_group_rows(op_ref, bcf_ref, 0)
        _s5_place_group_rows(op_ref, bcb_ref, S5_TILE_STATE)

    e = _dot(_s5_chunk_cat(u_ref, BF16), op_ref[...])
    ef_ref[...] = e[:, :S5_TILE_STATE]
    eb_ref[...] = e[:, S5_TILE_STATE:]


def s5_drive(uz, bc):
    rows = TOKENS // S5_ROW_SPLIT
    out = jax.ShapeDtypeStruct((S5_CHUNKS, S5_GROUPS * 2 * S5_STATE), F32)
    ospec = pl.BlockSpec((rows // S5_L, S5_TILE_STATE), lambda q, i: (i, q))

    def bspec(d):
        return pl.BlockSpec((None, S5_TILE_GROUPS, S5_L * S5_GROUP, 2 * S5_STATE),
                            lambda q, i: (d, q, 0, 0))
    return pl.pallas_call(
        _s5_drive_kernel,
        out_shape=[out, out],
        grid=(S5_TILES, S5_ROW_SPLIT),
        in_specs=[pl.BlockSpec((rows, LANES), lambda q, i: (i, q)), bspec(0), bspec(1)],
        out_specs=[ospec, ospec],
        scratch_shapes=[pltpu.VMEM((S5_CAT, 2 * S5_TILE_STATE), BF16)],
        compiler_params=_cparams("arbitrary", "arbitrary"),
    )(uz, bc, bc)


def _s5_scan_kernel(*refs, reverse):
    ng = S5_TILE_GROUPS
    nj = S5_SEG_CHUNKS
    e_refs, a_ref, x_refs = refs[:ng], refs[ng], refs[ng + 1:2 * ng + 1]
    in_ref, es_ref = refs[-2:]
    lanes = [slice(g * LANES, (g + 1) * LANES) for g in range(ng)]
    swap = lambda z: pltpu.roll(z, S5_STATE, 1)
    for g in range(ng):
        es_ref[g] = swap(e_refs[g][...])
    a1 = [a_ref[0, 0][:, sl] for sl in lanes]
    a2 = [a_ref[0, 1][:, sl] for sl in lanes]
    seg0 = lax.broadcasted_iota(jnp.int32, (S5_SEG, LANES), 0) == 0

    def tile(j):
        jj = (nj - 1 - j) if reverse else j
        return jj, pl.ds(jj, S5_SEG, stride=nj)

    def step(j, carry, emit):
        jj, rows = tile(j)
        out = []
        for g in range(ng):
            z, zs = carry[2 * g], carry[2 * g + 1]
            if reverse:
                reset = seg0 & (jj == S5_CTX_CHUNKS - 1)
                z, zs = jnp.where(reset, 0.0, z), jnp.where(reset, 0.0, zs)
            if emit:
                x_refs[g][rows, :] = z
            out += [a1[g] * z + a2[g] * zs + e_refs[g][rows, :],
                    a1[g] * zs - a2[g] * z + es_ref[g, rows, :]]
        return tuple(out)

    zero = tuple(jnp.zeros((S5_SEG, LANES), F32) for _ in range(2 * ng))
    end = lax.fori_loop(0, nj, functools.partial(step, emit=False), zero, unroll=2)[0::2]
    order = range(S5_SEG - 1, -1, -1) if reverse else range(S5_SEG)
    for g, sl in enumerate(lanes):
        cur = end[g][0:1, :] if reverse else jnp.zeros((1, LANES), F32)
        for s in order:
            in_ref[s:s + 1, sl] = cur
            cur = a_ref[1, 0][:, sl] * cur + a_ref[1, 1][:, sl] * swap(cur) + end[g][s:s + 1, :]
    start = []
    for sl in lanes:
        start += [in_ref[:, sl], swap(in_ref[:, sl])]
    lax.fori_loop(0, nj, functools.partial(step, emit=True), tuple(start), unroll=2)


def s5_scan(e, rows, direction):
    ng = S5_TILE_GROUPS
    cols = S5_TILE_STATE
    blks = [pl.BlockSpec((S5_CHUNKS, LANES), lambda i, g=g: (0, ng * i + g)) for g in range(ng)]
    return pl.pallas_call(
        functools.partial(_s5_scan_kernel, reverse=direction == 1),
        out_shape=[jax.ShapeDtypeStruct((S5_CHUNKS, S5_TILES * LANES), F32)] * ng,
        grid=(S5_TILES,),
        in_specs=blks + [pl.BlockSpec((None, 2, 2, 1, cols),
                                      lambda i: (direction, 0, 0, 0, i))],
        out_specs=[pl.BlockSpec((S5_CHUNKS, LANES), lambda i: (0, i))] * ng,
        scratch_shapes=[pltpu.VMEM((S5_SEG, cols), F32),
                        pltpu.VMEM((ng, S5_CHUNKS, LANES), F32)],
        compiler_params=_cparams("arbitrary"),
    )(*([e] * ng), rows)


def _gelu_tanh(x):
    return 0.5 * x * (1.0 + jnp.tanh(math.sqrt(2.0 / math.pi) * (x + 0.044715 * (x * x * x))))


def _s5_read_kernel(u_ref, b0f_ref, b0b_ref, c0_ref, cctf_ref, cctb_ref, d_ref, *refs):
    x_refs = refs[:2 * S5_TILE_GROUPS]
    o_ref, m_ref, c_ref, t_ref, dk_ref = refs[2 * S5_TILE_GROUPS:]
    first = (pl.program_id(0) == 0) & (pl.program_id(1) == 0)
    ts = S5_TILE_STATE

    @pl.when(first)
    def _():
        c_ref[...] = jnp.zeros(c_ref.shape, BF16)
        t_ref[...] = jnp.zeros(t_ref.shape, BF16)

    @pl.when(pl.program_id(1) == 0)
    def _():
        _s5_place_group_rows(c_ref, cctf_ref, 0)
        _s5_place_group_rows(c_ref, cctb_ref, ts)
        for k, src in enumerate((b0f_ref, b0b_ref, c0_ref)):
            for gl in range(S5_TILE_GROUPS):
                t_ref[k, gl * S5_GROUP:(gl + 1) * S5_GROUP, gl * LANES:(gl + 1) * LANES] = src[gl]
        b0f, b0b, c0 = t_ref[0], t_ref[1], t_ref[2]
        dk_ref[S5_L - 1] = (_dot_nt(b0f, c0) + _dot_nt(b0b, c0)).astype(BF16)
        for lag in range(1, S5_L):
            cf = c_ref[(lag - 1) * LANES:lag * LANES, 0:ts]
            dk_ref[S5_L - 1 + lag] = _dot_nt(b0f, cf).astype(BF16)
            cb = c_ref[(S5_L - lag) * LANES:(S5_L - lag + 1) * LANES, ts:2 * ts]
            dk_ref[S5_L - 1 - lag] = _dot_nt(b0b, cb).astype(BF16)
        for s in range(S5_L):
            for t in range(S5_L):
                m_ref[s * LANES:(s + 1) * LANES, t * LANES:(t + 1) * LANES] = (
                    dk_ref[t - s + S5_L - 1])

    n = u_ref.shape[0] // S5_L
    xs = jnp.concatenate([x[...].astype(BF16) for x in x_refs], axis=-1)
    y = _dot(_s5_chunk_cat(u_ref, BF16), m_ref[...]) + _dot_nt(xs, c_ref[...])
    for t in range(S5_L):
        rows = pl.ds(t, n, stride=S5_L)
        o_ref[rows, :] = _gelu_tanh(y[:, t * LANES:(t + 1) * LANES] + d_ref[...] * u_ref[rows, :])


def s5_read(uz, bc, c0, cct, xf, xb, d_skip):
    rows = TOKENS // S5_ROW_SPLIT
    ublk = pl.BlockSpec((rows, LANES), lambda q, i: (i, q))
    xblk = pl.BlockSpec((rows // S5_L, LANES), lambda q, i: (i, q))
    gp = (S5_TILE_GROUPS, S5_GROUP, 2 * S5_STATE)

    def cspec(d):
        return pl.BlockSpec((None, S5_TILE_GROUPS, S5_L * S5_GROUP, 2 * S5_STATE),
                            lambda q, i: (d, q, 0, 0))

    def b0spec(d, s):
        return pl.BlockSpec((None, S5_TILE_GROUPS, None, S5_GROUP, 2 * S5_STATE),
                            lambda q, i: (d, q, s, 0, 0))
    bc5 = bc.reshape(2, S5_GROUPS, S5_L, S5_GROUP, 2 * S5_STATE)
    return pl.pallas_call(
        _s5_read_kernel,
        out_shape=jax.ShapeDtypeStruct((TOKENS, S5_WIDTH), F32),
        grid=(S5_TILES, S5_ROW_SPLIT),
        in_specs=[ublk, b0spec(0, S5_L - 1), b0spec(1, 0),
                  pl.BlockSpec(gp, lambda q, i: (q, 0, 0)),
                  cspec(0), cspec(1),
                  pl.BlockSpec((1, LANES), lambda q, i: (0, q))] + [xblk] * (len(xf) + len(xb)),
        out_specs=ublk,
        scratch_shapes=[pltpu.VMEM((S5_CAT, S5_CAT), BF16),
                        pltpu.VMEM((S5_CAT, 2 * S5_TILE_STATE), BF16),
                        pltpu.VMEM((3, LANES, S5_TILE_STATE), BF16),
                        pltpu.VMEM((2 * S5_L - 1, LANES, LANES), BF16)],
        compiler_params=_cparams("arbitrary", "arbitrary"),
    )(uz, bc5, bc5, c0, cct, cct, d_skip.astype(F32).reshape(1, S5_WIDTH), *xf, *xb)


def _glu_kernel(s_ref, w_ref, b_ref, o_ref, lhs_ref, *, tn):
    j = pl.program_id(1)

    @pl.when(j == 0)
    def _():
        lhs_ref[...] = s_ref[...].astype(BF16)

    v = _dot(lhs_ref[...], w_ref[...]) + b_ref[...]
    sj = s_ref[:, pl.ds(pl.multiple_of(j * tn, tn), tn)]
    o_ref[...] = sj * jax.nn.sigmoid(v)


def s5_glu(s, w_glu, b_glu, tm=768, tn=1024):
    d = S5_WIDTH
    return pl.pallas_call(
        functools.partial(_glu_kernel, tn=tn),
        out_shape=jax.ShapeDtypeStruct((TOKENS, d), F32),
        grid=(TOKENS // tm, d // tn),
        in_specs=[pl.BlockSpec((tm, d), lambda i, j: (i, 0)),
                  pl.BlockSpec((d, tn), lambda i, j: (0, j)),
                  pl.BlockSpec((1, tn), lambda i, j: (0, j))],
        out_specs=pl.BlockSpec((tm, tn), lambda i, j: (i, j)),
        scratch_shapes=[pltpu.VMEM((tm, d), BF16)],
        compiler_params=_cparams("arbitrary", "arbitrary"),
    )(s, w_glu, b_glu.reshape(1, d))


def _block_diag(w, size):
    lane = np.arange(size)
    spread = jnp.asarray(lane[None, :] % ML_BLOCK == np.arange(ML_BLOCK)[:, None], BF16)
    tiled = jnp.dot(w.astype(BF16).reshape(-1, ML_BLOCK), spread, preferred_element_type=BF16)
    same_block = jnp.asarray(lane[:, None] // ML_BLOCK == lane[None, :] // ML_BLOCK)
    return jnp.where(same_block, tiled.reshape(-1, size, size), jnp.zeros((), BF16))


def na_layer(xc, h, mods, tail, w_in, li, rpb, w_out):
    e = D_MODEL
    last = isinstance(tail[0], str)
    qkvz = in_proj(h, w_in, li, 0, 4 * e, BF16, scaled_cols=e, scale=NA_Q_SCALE)
    o = na_attention(qkvz, na_pair_table(rpb), with_ctx_out=not last)
    if last:
        return gated_out_final(o, qkvz, 3 * e, w_out.astype(BF16), xc, mods, tail[1])
    tm = CTX_LEN if isinstance(xc, tuple) else 384
    return gated_out(o, qkvz, 3 * e, w_out.astype(BF16), xc, mods, *tail, tm=tm)


def mlstm_layer(xc, h, mods, tail, w_in, li, conv_w, conv_b, wq, wk, wv, w_gate, b_gate, skip,
                norm_g, w_out):
    e = ML_WIDTH
    xz = in_proj(h, w_in, li, 0, 2 * e, F32)
    wq_bd = _block_diag(wq, ML_DIAG)
    wk_bd = _block_diag(wk, ML_DIAG)
    wv_bd = _block_diag(wv, ML_DIAG)
    wvt_bd = _block_diag(wv.transpose(0, 2, 1), ML_DIAG)
    wg_t = w_gate.reshape(3, e, 4 * ML_HEADS).transpose(0, 2, 1).astype(BF16)
    xconv, q, k, vt, gates_t = ml_pre(xz, conv_w, conv_b, wq_bd, wk_bd, wv_bd, wvt_bd, wg_t,
                                      b_gate)
    hf, hb = ml_recurrence(q, k, vt, gates_t)
    return ml_out(hf, hb, xconv, xz, norm_g, skip, w_out.astype(BF16), xc, mods, *tail)


def s5_layer(xc, h, mods, tail, w_in, li, lam_re, lam_im, log_dt, b_re, b_im, c_re, c_im,
             d_skip, w_glu, b_glu, w_out):
    e = S5_WIDTH
    uz = in_proj(h, w_in, li, 0, 2 * e, F32)
    bc, c0, cct, rows = s5_operators(lam_re, lam_im, log_dt, b_re, b_im, c_re, c_im)
    ef, eb = s5_drive(uz, bc)
    xf = s5_scan(ef, rows, 0)
    xb = s5_scan(eb, rows, 1)
    s = s5_read(uz, bc, c0, cct, xf, xb, d_skip)
    s2 = s5_glu(s, w_glu.astype(BF16), b_glu)
    return gated_out(s2, uz, e, w_out.astype(BF16), xc, mods, *tail)


def kernel(x, c, ctx, c_ctx, norm_g, ada_w, ada_b, na_w_in, na_rpb, na_w_out, ml_w_in, ml_conv_w, ml_conv_b, ml_wq, ml_wk, ml_wv, ml_w_gate, ml_b_gate, ml_skip, ml_norm_g, ml_w_out, s5_w_in, s5_lam_re, s5_lam_im, s5_log_dt, s5_b_re, s5_b_im, s5_c_re, s5_c_im, s5_d, s5_w_glu, s5_b_glu, s5_w_out, final_g):
    xc = (ctx[0], x[0])
    c8 = jnp.concatenate([c, c_ctx[None, :], jnp.zeros((6, D_MODEL), F32)], axis=0)
    mods = ada_mods(c8, ada_w, ada_b)
    h = norm_mod(xc, norm_g[0], mods[0], tm=CTX_LEN)
    ia = ib = ic = 0
    assert (DEPTH - 1) % 3 == 0, "the last layer must be the attention layer that ends the trunk"
    for layer in range(DEPTH):
        tail = ((norm_g[layer + 1], mods[layer + 1]) if layer + 1 < DEPTH
                else ('final', final_g))
        kind = layer % 3
        if kind == 0:
            out = na_layer(xc, h, mods[layer], tail, na_w_in, ia, na_rpb[ia], na_w_out[ia])
            ia += 1
        elif kind == 1:
            out = mlstm_layer(xc, h, mods[layer], tail, ml_w_in, ib, ml_conv_w[ib],
                              ml_conv_b[ib], ml_wq[ib], ml_wk[ib], ml_wv[ib], ml_w_gate[ib],
                              ml_b_gate[ib], ml_skip[ib], ml_norm_g[ib], ml_w_out[ib])
            ib += 1
        else:
            out = s5_layer(xc, h, mods[layer], tail, s5_w_in, ic, s5_lam_re[ic],
                           s5_lam_im[ic], s5_log_dt[ic], s5_b_re[ic], s5_b_im[ic], s5_c_re[ic],
                           s5_c_im[ic], s5_d[ic], s5_w_glu[ic], s5_b_glu[ic], s5_w_out[ic])
            ic += 1
        if layer + 1 < DEPTH:
            xc, h = out
    return out[None]
```
